```python
import math
import jax, jax.numpy as jnp
from jax import lax
import numpy as np

D_MODEL = 1024
BATCH = 4
SEQ = 8192
DEPTH = 2

HEAD_DIM = 64
A_HEADS = D_MODEL // (2 * HEAD_DIM)
B_HEADS = D_MODEL // (4 * HEAD_DIM)
C_HEADS = D_MODEL // (2 * HEAD_DIM)
D_HEADS = D_MODEL // (2 * HEAD_DIM)
D_KV_HEADS = D_HEADS // 4
A_W = A_HEADS * HEAD_DIM
B_W = B_HEADS * 2 * HEAD_DIM
C_W = C_HEADS * HEAD_DIM
D_W = D_HEADS * HEAD_DIM
D_KV_W = D_KV_HEADS * HEAD_DIM
EVEN_WIDTH = A_W + B_W
ODD_WIDTH = C_W + D_W
EVEN_SPLITS = (A_W, A_W, A_W, A_W, B_W, B_W, B_W, B_W)
ODD_SPLITS = (C_W, C_W, C_W, C_W, D_W, D_KV_W, D_KV_W, D_KV_W, D_KV_W, D_KV_W, D_KV_W, D_W, 3 * D_HEADS)
EVEN_COLS = 4 * A_W + 4 * B_W
ODD_COLS = 4 * C_W + 2 * D_W + 6 * D_KV_W + 3 * D_HEADS
N_EVEN = (DEPTH + 1) // 2
N_ODD = DEPTH // 2
DILATED_PATTERNS = ((128, 1), (512, 4), (2048, 16))
BAND_BLK = 128
DENSE_BLK = 128
MOBA_BLK = 256
MOBA_TOPK = 3
MOBA_CQ = 32
CMP_LEN = 32
CMP_STRIDE = 16
CMP_HIDDEN = 2 * HEAD_DIM
SLC_BLK = 64
SLC_TOPK = 16
NSA_WINDOW = 512
NSA_CQ = 64
RMS_EPS = 1e-6

kernel_name = 'hybrid_dilated_diff_moba_nsa'


def rmsnorm(x, g):
    xf = x.astype(jnp.float32)
    y = xf * lax.rsqrt(jnp.mean(xf * xf, axis=-1, keepdims=True) + RMS_EPS)
    return (y * g.astype(jnp.float32)).astype(x.dtype)


def alibi_slopes(n):
    return jnp.exp2(-8.0 * jnp.arange(1, n + 1, dtype=jnp.float32) / n)


def split_cols(h, sizes):
    out, off = [], 0
    for s in sizes:
        out.append(h[..., off:off + s])
        off += s
    return out


def softmax_parts(s, mask):
    s = jnp.where(mask, s, -jnp.inf)
    m = jnp.max(s, axis=-1, keepdims=True)
    m = jnp.where(jnp.isfinite(m), m, 0.0)
    p = jnp.exp(s - m)
    return p, m, jnp.sum(p, axis=-1, keepdims=True)


def take_blocks(blocks, idx):
    return jax.vmap(jax.vmap(lambda t, i: t[i]))(blocks, idx)


def banded_attention(q, k, v, max_dist, slopes, dist_scale=1):
    N, L, Hq, dh = q.shape
    Hkv = k.shape[2]
    G = Hq // Hkv
    nb = L // BAND_BLK
    nprev = -(-max_dist // BAND_BLK)
    C = (nprev + 1) * BAND_BLK

    def band(t):
        tb = jnp.pad(t.reshape(N, nb, BAND_BLK, Hkv, dh), ((0, 0), (nprev, 0), (0, 0), (0, 0), (0, 0)))
        return jnp.concatenate([tb[:, i:i + nb] for i in range(nprev + 1)], axis=2)

    kb, vb = band(k), band(v)
    qb = q.reshape(N, nb, BAND_BLK, Hkv, G, dh)
    s = jnp.einsum('nbqhgd,nbchd->nbhgqc', qb, kb, preferred_element_type=jnp.float32) * dh ** -0.5
    dist = jnp.arange(BAND_BLK)[:, None] + nprev * BAND_BLK - jnp.arange(C)[None, :]
    kglob = jnp.arange(nb)[:, None] * BAND_BLK - nprev * BAND_BLK + jnp.arange(C)[None, :]
    mask = ((dist >= 0) & (dist <= max_dist))[None, None, None, None] & (kglob >= 0)[None, :, None, None, None, :]
    bias = -slopes.astype(jnp.float32).reshape(Hkv, G, 1, 1) * (dist * dist_scale).astype(jnp.float32)
    p, m, l = softmax_parts(s + bias, mask)
    o = jnp.einsum('nbhgqc,nbchd->nbhgqd', p.astype(v.dtype), vb, preferred_element_type=jnp.float32) / l
    o = o.transpose(0, 1, 4, 2, 3, 5).reshape(N, L, Hq, dh)
    stat = lambda t: t[..., 0].transpose(0, 1, 4, 2, 3).reshape(N, L, Hq)
    return o, stat(m), stat(l)


def dilated_attention(q, k, v, slopes):
    B, S, H, dh = q.shape
    outs, maxes, dens = [], [], []
    for window, dil in DILATED_PATTERNS:
        span = dil * BAND_BLK
        Sp = -(-S // span) * span
        Lp = Sp // dil

        def to_phase(t):
            t = jnp.pad(t, ((0, 0), (0, Sp - S), (0, 0), (0, 0)))
            return jnp.swapaxes(t.reshape(B, Lp, dil, H, dh), 1, 2).reshape(B * dil, Lp, H, dh)

        def from_phase(t):
            rest = t.shape[2:]
            t = jnp.swapaxes(t.reshape(B, dil, Lp, *rest), 1, 2).reshape(B, Sp, *rest)
            return t[:, :S]

        o, m, l = banded_attention(to_phase(q), to_phase(k), to_phase(v), window // dil, slopes, dil)
        outs.append(from_phase(o))
        maxes.append(from_phase(m))
        dens.append(from_phase(l))
    m_all = jnp.stack(maxes)
    w = jnp.stack(dens) * jnp.exp(m_all - jnp.max(m_all, axis=0, keepdims=True))
    w = w / jnp.sum(w, axis=0, keepdims=True)
    return jnp.sum(w[..., None] * jnp.stack(outs), axis=0)


def diff_attention(q, k, v, lam, slopes):
    B, S, H, _, dh = q.shape
    nq = S // DENSE_BLK
    qb = jnp.moveaxis(q.reshape(B, nq, DENSE_BLK, H, 2, dh), 1, 0)
    kpos = jnp.arange(S)
    sl = slopes[None, :, None, None, None]

    def block(args):
        qi, i = args
        s = jnp.einsum('bqhcd,bkhcd->bhcqk', qi, k, preferred_element_type=jnp.float32) * dh ** -0.5
        dist = (i * DENSE_BLK + jnp.arange(DENSE_BLK))[:, None] - kpos[None, :]
        s = jnp.where(dist >= 0, s - sl * dist.astype(jnp.float32), -jnp.inf)
        p = jax.nn.softmax(s, axis=-1)
        a = p[:, :, 0] - lam * p[:, :, 1]
        return jnp.einsum('bhqk,bkhe->bqhe', a.astype(v.dtype), v, preferred_element_type=jnp.float32)

    o = lax.map(block, (qb, jnp.arange(nq)))
    return jnp.moveaxis(o, 0, 1).reshape(B, S, H, 2 * dh)


def moba_attention(q, k, v, slopes):
    B, S, H, dh = q.shape
    Sp = -(-S // MOBA_BLK) * MOBA_BLK
    pad = ((0, 0), (0, Sp - S), (0, 0), (0, 0))
    q, k, v = jnp.pad(q, pad), jnp.pad(k, pad), jnp.pad(v, pad)
    nblk = Sp // MOBA_BLK
    kbh = k.reshape(B, nblk, MOBA_BLK, H, dh).transpose(0, 3, 1, 2, 4)
    vbh = v.reshape(B, nblk, MOBA_BLK, H, dh).transpose(0, 3, 1, 2, 4)
    kmean = jnp.mean(kbh.astype(jnp.float32), axis=3)
    gate = jnp.einsum('bshd,bhnd->bhsn', q.astype(jnp.float32), kmean)
    past = jnp.arange(nblk)[None, :] < (jnp.arange(Sp) // MOBA_BLK)[:, None]
    n_sel = min(MOBA_TOPK, nblk)
    _, idx = lax.top_k(jnp.where(past, gate, -jnp.inf), n_sel)
    nch = Sp // MOBA_CQ
    qs = jnp.moveaxis(q.reshape(B, nch, MOBA_CQ, H, dh), 1, 0)
    idxs = idx.reshape(B, H, nch, MOBA_CQ, n_sel).transpose(2, 0, 1, 3, 4)
    scale = dh ** -0.5

    def chunk(args):
        qc, ic, c = args
        qpos = c * MOBA_CQ + jnp.arange(MOBA_CQ)
        own = (c * MOBA_CQ) // MOBA_BLK
        k_own = lax.dynamic_index_in_dim(kbh, own, axis=2, keepdims=False)
        v_own = lax.dynamic_index_in_dim(vbh, own, axis=2, keepdims=False)
        k_sel = take_blocks(kbh, ic)
        v_sel = take_blocks(vbh, ic)
        kpos_sel = ic[..., None] * MOBA_BLK + jnp.arange(MOBA_BLK)
        s_sel = jnp.einsum('bqhd,bhqnkd->bhqnk', qc, k_sel, preferred_element_type=jnp.float32) * scale
        s_sel = s_sel - slopes[None, :, None, None, None] * (qpos[None, None, :, None, None] - kpos_sel).astype(jnp.float32)
        valid_sel = jnp.broadcast_to((ic < own)[..., None], s_sel.shape)
        kpos_own = own * MOBA_BLK + jnp.arange(MOBA_BLK)
        s_own = jnp.einsum('bqhd,bhkd->bhqk', qc, k_own, preferred_element_type=jnp.float32) * scale
        s_own = s_own - slopes[None, :, None, None] * (qpos[:, None] - kpos_own[None, :]).astype(jnp.float32)
        mask_own = jnp.broadcast_to(kpos_own[None, :] <= qpos[:, None], s_own.shape)
        nk = n_sel * MOBA_BLK
        s = jnp.concatenate([s_sel.reshape(B, H, MOBA_CQ, nk), s_own], axis=-1)
        mask = jnp.concatenate([valid_sel.reshape(B, H, MOBA_CQ, nk), mask_own], axis=-1)
        p = jax.nn.softmax(jnp.where(mask, s, -jnp.inf), axis=-1)
        p_sel = p[..., :nk].reshape(B, H, MOBA_CQ, n_sel, MOBA_BLK).astype(v.dtype)
        p_own = p[..., nk:].astype(v.dtype)
        return (jnp.einsum('bhqnk,bhqnkd->bqhd', p_sel, v_sel, preferred_element_type=jnp.float32)
                + jnp.einsum('bhqk,bhkd->bqhd', p_own, v_own, preferred_element_type=jnp.float32))

    o = lax.map(chunk, (qs, idxs, jnp.arange(nch)))
    return jnp.moveaxis(o, 0, 1).reshape(B, Sp, H, dh)[:, :S]


def compress_blocks(t, pe, w1, w2):
    B, S, Hkv, dh = t.shape
    n_cmp = (S - CMP_LEN) // CMP_STRIDE + 1
    r = CMP_LEN // CMP_STRIDE
    ch = t.reshape(B, S // CMP_STRIDE, CMP_STRIDE, Hkv, dh)
    blocks = jnp.concatenate([ch[:, i:i + n_cmp] for i in range(r)], axis=2) + pe[None, None, :, None, :]
    flat = blocks.transpose(0, 1, 3, 2, 4).reshape(B, n_cmp, Hkv, CMP_LEN * dh)
    return jax.nn.silu(flat @ w1) @ w2


def nsa_attention(q, k_cmp, v_cmp, k_slc, v_slc, k_win, v_win, gates, slopes):
    B, S, Hq, dh = q.shape
    Hkv = k_slc.shape[2]
    G = Hq // Hkv
    n_cmp = k_cmp.shape[1]
    n_slc = S // SLC_BLK
    n_top = min(SLC_TOPK, n_slc)
    scale = dh ** -0.5
    cstart = jnp.arange(n_cmp) * CMP_STRIDE
    cend = cstart + CMP_LEN - 1
    sstart = jnp.arange(n_slc) * SLC_BLK
    cmp_in_slc = ((cstart[:, None] <= sstart[None, :] + SLC_BLK - 1) & (cend[:, None] >= sstart[None, :])).astype(jnp.float32)
    ksb = k_slc.reshape(B, n_slc, SLC_BLK, Hkv, dh).transpose(0, 3, 1, 2, 4)
    vsb = v_slc.reshape(B, n_slc, SLC_BLK, Hkv, dh).transpose(0, 3, 1, 2, 4)
    sl = slopes.reshape(Hkv, G)
    nch = S // NSA_CQ
    qs = jnp.moveaxis(q.reshape(B, nch, NSA_CQ, Hkv, G, dh), 1, 0)
    jb = jnp.arange(n_slc)[None, :]

    def chunk(args):
        qc, c = args
        qpos = c * NSA_CQ + jnp.arange(NSA_CQ)
        s = jnp.einsum('bqhgd,bnhd->bhgqn', qc, k_cmp, preferred_element_type=jnp.float32) * scale
        s = s - sl[:, :, None, None] * (qpos[:, None] - cend[None, :]).astype(jnp.float32)
        p, _, l = softmax_parts(s, cend[None, :] <= qpos[:, None])
        p = p / jnp.where(l > 0, l, 1.0)
        o_cmp = jnp.einsum('bhgqn,bnhd->bqhgd', p.astype(v_cmp.dtype), v_cmp, preferred_element_type=jnp.float32)
        imp = jnp.einsum('bhgqn,nj->bhqj', p, cmp_in_slc)
        qb = (qpos // SLC_BLK)[:, None]
        forced = (jb == 0) | (jb == qb) | (jb == qb - 1)
        imp = jnp.where(forced, jnp.inf, jnp.where(jb > qb, -jnp.inf, imp))
        _, idx = lax.top_k(imp, n_top)
        k_sel = take_blocks(ksb, idx)
        v_sel = take_blocks(vsb, idx)
        dist = qpos[None, None, :, None, None] - (idx[..., None] * SLC_BLK + jnp.arange(SLC_BLK))
        s2 = jnp.einsum('bqhgd,bhqnkd->bhgqnk', qc, k_sel, preferred_element_type=jnp.float32) * scale
        s2 = s2 - sl[None, :, :, None, None, None] * dist[:, :, None].astype(jnp.float32)
        mask2 = jnp.broadcast_to((dist >= 0)[:, :, None], s2.shape)
        nk = n_top * SLC_BLK
        p2 = jax.nn.softmax(jnp.where(mask2, s2, -jnp.inf).reshape(B, Hkv, G, NSA_CQ, nk), axis=-1)
        p2 = p2.reshape(B, Hkv, G, NSA_CQ, n_top, SLC_BLK).astype(v_sel.dtype)
        o_slc = jnp.einsum('bhgqnk,bhqnkd->bqhgd', p2, v_sel, preferred_element_type=jnp.float32)
        return o_cmp, o_slc

    o_cmp, o_slc = lax.map(chunk, (qs, jnp.arange(nch)))
    o_cmp = jnp.moveaxis(o_cmp, 0, 1).reshape(B, S, Hq, dh)
    o_slc = jnp.moveaxis(o_slc, 0, 1).reshape(B, S, Hq, dh)
    o_win, _, _ = banded_attention(q, k_win, v_win, NSA_WINDOW - 1, slopes)
    g = gates.astype(jnp.float32)
    return g[..., 0:1] * o_cmp + g[..., 1:2] * o_slc + g[..., 2:3] * o_win


def even_layer(x, ln, w_in, qkn, lam, subln, w_out, layer):
    B, S, _ = x.shape
    h = rmsnorm(x, ln)
    aq, ak, av, az, bq, bk, bv, bz = split_cols(h @ w_in, EVEN_SPLITS)
    heads = lambda t, n: t.reshape(B, S, n, -1)
    oa = dilated_attention(rmsnorm(heads(aq, A_HEADS), qkn[0]), rmsnorm(heads(ak, A_HEADS), qkn[1]),
                           heads(av, A_HEADS), alibi_slopes(A_HEADS)).astype(x.dtype)
    bq = rmsnorm(bq.reshape(B, S, B_HEADS, 2, HEAD_DIM), qkn[2])
    bk = rmsnorm(bk.reshape(B, S, B_HEADS, 2, HEAD_DIM), qkn[3])
    lam_init = 0.8 - 0.6 * math.exp(-0.3 * layer)
    lf = lam.astype(jnp.float32)
    lam_f = jnp.exp(jnp.sum(lf[0] * lf[1])) - jnp.exp(jnp.sum(lf[2] * lf[3])) + lam_init
    ob = diff_attention(bq, bk, heads(bv, B_HEADS), lam_f, alibi_slopes(B_HEADS))
    ob = (rmsnorm(ob, subln) * (1.0 - lam_init)).astype(x.dtype)
    mixed = jnp.concatenate([oa.reshape(B, S, A_W) * jax.nn.silu(az),
                             ob.reshape(B, S, B_W) * jax.nn.silu(bz)], axis=-1)
    return x + mixed @ w_out


def odd_layer(x, ln, w_in, qkn, phi_pe, phi_w1, phi_w2, w_out):
    B, S, _ = x.shape
    h = rmsnorm(x, ln)
    cq, ck, cv, cz, dq, dkc, dvc, dks, dvs, dkw, dvw, dz, dg = split_cols(h @ w_in, ODD_SPLITS)
    heads = lambda t, n: t.reshape(B, S, n, HEAD_DIM)
    oc = moba_attention(rmsnorm(heads(cq, C_HEADS), qkn[0]), rmsnorm(heads(ck, C_HEADS), qkn[1]),
                        heads(cv, C_HEADS), alibi_slopes(C_HEADS)).astype(x.dtype)
    k_cmp = rmsnorm(compress_blocks(heads(dkc, D_KV_HEADS), phi_pe[0], phi_w1[0], phi_w2[0]), qkn[3])
    v_cmp = compress_blocks(heads(dvc, D_KV_HEADS), phi_pe[1], phi_w1[1], phi_w2[1])
    od = nsa_attention(rmsnorm(heads(dq, D_HEADS), qkn[2]), k_cmp, v_cmp,
                       rmsnorm(heads(dks, D_KV_HEADS), qkn[4]), heads(dvs, D_KV_HEADS),
                       rmsnorm(heads(dkw, D_KV_HEADS), qkn[5]), heads(dvw, D_KV_HEADS),
                       jax.nn.sigmoid(dg.reshape(B, S, D_HEADS, 3)), alibi_slopes(D_HEADS)).astype(x.dtype)
    mixed = jnp.concatenate([oc.reshape(B, S, C_W) * jax.nn.silu(cz),
                             od.reshape(B, S, D_W) * jax.nn.silu(dz)], axis=-1)
    return x + mixed @ w_out


def setup_inputs(seed: int = 0) -> dict:
    key = jax.random.key(seed)
    ks = jax.random.split(key, 14)
    nrm = lambda k, shape: jax.random.normal(k, shape, jnp.float32)
    w = lambda k, shape, fan_in: nrm(k, shape) * fan_in ** -0.5
    gain = lambda k, shape: 1.0 + 0.05 * nrm(k, shape)
    return {
        'x': nrm(ks[0], (BATCH, SEQ, D_MODEL)),
        'ln_e': gain(ks[1], (N_EVEN, D_MODEL)),
        'w_in_e': w(ks[2], (N_EVEN, D_MODEL, EVEN_COLS), D_MODEL),
        'qkn_e': gain(ks[3], (N_EVEN, 4, HEAD_DIM)),
        'lam_e': 0.1 * nrm(ks[4], (N_EVEN, 4, HEAD_DIM)),
        'subln_e': gain(ks[5], (N_EVEN, 2 * HEAD_DIM)),
        'w_out_e': w(ks[6], (N_EVEN, EVEN_WIDTH, D_MODEL), EVEN_WIDTH),
        'ln_o': gain(ks[7], (N_ODD, D_MODEL)),
        'w_in_o': w(ks[8], (N_ODD, D_MODEL, ODD_COLS), D_MODEL),
        'qkn_o': gain(ks[9], (N_ODD, 6, HEAD_DIM)),
        'phi_pe': 0.1 * nrm(ks[10], (N_ODD, 2, CMP_LEN, HEAD_DIM)),
        'phi_w1': w(ks[11], (N_ODD, 2, CMP_LEN * HEAD_DIM, CMP_HIDDEN), CMP_LEN * HEAD_DIM),
        'phi_w2': w(ks[12], (N_ODD, 2, CMP_HIDDEN, HEAD_DIM), CMP_HIDDEN),
        'w_out_o': w(ks[13], (N_ODD, ODD_WIDTH, D_MODEL), ODD_WIDTH),
    }


def reference(x, ln_e, w_in_e, qkn_e, lam_e, subln_e, w_out_e, ln_o, w_in_o, qkn_o,
              phi_pe, phi_w1, phi_w2, w_out_o):
    for layer in range(DEPTH):
        i = layer // 2
        if layer % 2 == 0:
            x = even_layer(x, ln_e[i], w_in_e[i], qkn_e[i], lam_e[i], subln_e[i], w_out_e[i], layer)
        else:
            x = odd_layer(x, ln_o[i], w_in_o[i], qkn_o[i], phi_pe[i], phi_w1[i], phi_w2[i], w_out_o[i])
    return x
```

```python
import functools
import math

import jax
import jax.numpy as jnp
from jax import lax
from jax.experimental import pallas as pl
from jax.experimental.pallas import tpu as pltpu

HEAD_DIM = 64
LANES = 128
RMS_EPS = 1e-6
NEG = -1e30
VMEM_LIMIT = 56 * 1024 * 1024

F32 = jnp.float32
BF16 = jnp.bfloat16

_NT = (((1,), (1,)), ((), ()))


def _dot(a, b):
    return jnp.dot(a, b, preferred_element_type=F32)


def _dot_nt(a, b):
    return lax.dot_general(a, b, _NT, preferred_element_type=F32)


def _alibi_slopes(n):
    return [2.0 ** (-8.0 * (i + 1) / n) for i in range(n)]


def _split_bf16(x):
    hi = x.astype(BF16)
    lo = (x - hi.astype(F32)).astype(BF16)
    return hi, lo


def _proj_kernel(flags_ref, x_ref, ln_ref, w_ref, gain_ref, nmask_ref, bd_ref, o_ref, h_ref):
    j = pl.program_id(1)

    @pl.when(j == 0)
    def _():
        x = x_ref[...]
        ms = jnp.mean(x * x, axis=-1, keepdims=True)
        h_ref[...] = (x * lax.rsqrt(ms + RMS_EPS) * ln_ref[...]).astype(BF16)

    y = _dot(h_ref[...], w_ref[...])

    @pl.when(flags_ref[j] == 0)
    def _():
        o_ref[...] = y.astype(o_ref.dtype)

    @pl.when(flags_ref[j] != 0)
    def _():
        tn = y.shape[1]
        bd = bd_ref[...]
        for c in range(tn // 256):
            yc = y[:, c * 256:(c + 1) * 256]
            hi, lo = _split_bf16(yc * yc)
            ms = _dot(hi, bd) + _dot(lo, bd)
            yn = yc * lax.rsqrt(ms + RMS_EPS) * gain_ref[:, c * 256:(c + 1) * 256]
            keep = nmask_ref[:, c * 256:(c + 1) * 256] > 0.5
            o_ref[:, c * 256:(c + 1) * 256] = jnp.where(keep, yn, yc).astype(o_ref.dtype)


def _project(x2d, ln, w, gain, nmask, out_dtype, tm=512, tn=512):
    M, D = x2d.shape
    N = w.shape[1]
    assert M % tm == 0 and N % tn == 0 and tn % 256 == 0
    flags = (jnp.max(nmask.reshape(N // tn, tn), axis=1) > 0.5).astype(jnp.int32)
    r = jnp.arange(256) // HEAD_DIM
    bd = jnp.where(r[:, None] == r[None, :], 1.0 / HEAD_DIM, 0.0).astype(BF16)
    grid_spec = pltpu.PrefetchScalarGridSpec(
        num_scalar_prefetch=1,
        grid=(M // tm, N // tn),
        in_specs=[
            pl.BlockSpec((tm, D), lambda i, j, f: (i, 0)),
            pl.BlockSpec((1, D), lambda i, j, f: (0, 0)),
            pl.BlockSpec((D, tn), lambda i, j, f: (0, j)),
            pl.BlockSpec((1, tn), lambda i, j, f: (0, j)),
            pl.BlockSpec((1, tn), lambda i, j, f: (0, j)),
            pl.BlockSpec((256, 256), lambda i, j, f: (0, 0)),
        ],
        out_specs=pl.BlockSpec((tm, tn), lambda i, j, f: (i, j)),
        scratch_shapes=[pltpu.VMEM((tm, D), BF16)],
    )
    return pl.pallas_call(
        _proj_kernel,
        grid_spec=grid_spec,
        out_shape=jax.ShapeDtypeStruct((M, N), out_dtype),
        compiler_params=pltpu.CompilerParams(
            dimension_semantics=("parallel", "arbitrary"), vmem_limit_bytes=VMEM_LIMIT),
        name="proj",
    )(flags, x2d, ln.reshape(1, D).astype(F32), w, gain.reshape(1, N).astype(F32),
      nmask.reshape(1, N).astype(F32), bd)


def _outproj_kernel(x_ref, oa_ref, ob_ref, za_ref, zb_ref, wa_ref, wb_ref, o_ref):
    za = za_ref[...]
    zb = zb_ref[...]
    ma = (oa_ref[...] * (za * jax.nn.sigmoid(za))).astype(BF16)
    mb = (ob_ref[...] * (zb * jax.nn.sigmoid(zb))).astype(BF16)
    o_ref[...] = x_ref[...] + _dot(ma, wa_ref[...]) + _dot(mb, wb_ref[...])


def _out_project(x2d, oa, ob, z, za_blk, zb_blk, w_out, tm=512):
    M, D = x2d.shape
    W = oa.shape[1]
    wa = w_out[:W].astype(BF16)
    wb = w_out[W:].astype(BF16)
    return pl.pallas_call(
        _outproj_kernel,
        grid=(M // tm,),
        in_specs=[
            pl.BlockSpec((tm, D), lambda i: (i, 0)),
            pl.BlockSpec((tm, W), lambda i: (i, 0)),
            pl.BlockSpec((tm, W), lambda i: (i, 0)),
            pl.BlockSpec((tm, W), lambda i: (i, za_blk)),
            pl.BlockSpec((tm, W), lambda i: (i, zb_blk)),
            pl.BlockSpec((W, D), lambda i: (0, 0)),
            pl.BlockSpec((W, D), lambda i: (0, 0)),
        ],
        out_specs=pl.BlockSpec((tm, D), lambda i: (i, 0)),
        out_shape=jax.ShapeDtypeStruct((M, D), F32),
        compiler_params=pltpu.CompilerParams(
            dimension_semantics=("parallel",), vmem_limit_bytes=VMEM_LIMIT),
        name="outproj",
    )(x2d, oa, ob, z, z, wa, wb)


def _flash_step(t, vt, m, l, acc_ref, shift_c, row_bias=None):
    tmax = jnp.max(t, axis=0, keepdims=True)
    off = shift_c if row_bias is None else shift_c + row_bias
    m_new = jnp.maximum(m, tmax + off)
    p = jnp.exp(t - (m_new - off))
    alpha = jnp.exp(m - m_new)
    l_new = alpha * l + jnp.sum(p, axis=0, keepdims=True)
    acc_ref[...] = alpha * acc_ref[...] + _dot(vt, p.astype(BF16))
    return m_new, l_new


def _aug_const(rows, col_vals):
    lane = lax.broadcasted_iota(jnp.int32, (rows, HEAD_DIM), 1)
    out = jnp.zeros((rows, HEAD_DIM), F32)
    for c, v in enumerate(col_vals):
        out = jnp.where(lane == c, v, out)
    return out


def _diff_kernel(lam_ref, q_ref, k_ref, v_ref, subln_ref, o_ref,
                 k1_ref, k2_ref, vt_ref, acc1_ref, acc2_ref, *, tq, n_heads, out_scale):
    h = pl.program_id(1)
    qi = pl.program_id(2)
    S = k_ref.shape[0]
    tk = tq
    slope = jnp.exp2(-8.0 * (h + 1).astype(F32) / n_heads)

    @pl.when(qi == 0)
    def _():
        jj = lax.broadcasted_iota(jnp.int32, (tk, 1), 0).astype(F32)
        kaug = _aug_const(tk, [jj]).astype(BF16)

        def body(c, carry):
            rows = pl.ds(pl.multiple_of(c * tk, tk), tk)
            kc = k_ref[rows, :]
            k1_ref[rows, :] = jnp.concatenate([kc[:, :HEAD_DIM], kaug], axis=1)
            k2_ref[rows, :] = jnp.concatenate([kc[:, HEAD_DIM:], kaug], axis=1)
            vt_ref[:, rows] = v_ref[rows, :].astype(F32).T.astype(BF16)
            return carry

        lax.fori_loop(0, S // tk, body, 0)

    qt = q_ref[...]
    qaug = _aug_const(tq, [slope]).astype(BF16)
    q1 = jnp.concatenate([qt[:, :HEAD_DIM], qaug], axis=1)
    q2 = jnp.concatenate([qt[:, HEAD_DIM:], qaug], axis=1)

    acc1_ref[...] = jnp.zeros_like(acc1_ref)
    acc2_ref[...] = jnp.zeros_like(acc2_ref)
    init = jnp.full((1, tq), NEG, F32)
    zero = jnp.zeros((1, tq), F32)

    def tile(n, carry, masked):
        m1, l1, m2, l2 = carry
        rows = pl.ds(pl.multiple_of(n * tk, tk), tk)
        vt = vt_ref[:, rows]
        shift_c = slope * ((n - qi) * tk).astype(F32)
        t1 = _dot_nt(k1_ref[rows, :], q1)
        t2 = _dot_nt(k2_ref[rows, :], q2)
        if masked:
            causal = (lax.broadcasted_iota(jnp.int32, (tk, tq), 0)
                      <= lax.broadcasted_iota(jnp.int32, (tk, tq), 1))
            t1 = jnp.where(causal, t1, NEG)
            t2 = jnp.where(causal, t2, NEG)
        m1, l1 = _flash_step(t1, vt, m1, l1, acc1_ref, shift_c)
        m2, l2 = _flash_step(t2, vt, m2, l2, acc2_ref, shift_c)
        return m1, l1, m2, l2

    carry = tile(qi, (init, zero, init, zero), True)
    m1, l1, m2, l2 = lax.fori_loop(0, qi, lambda n, c: tile(n, c, False), carry)

    d = acc1_ref[...] / l1 - lam_ref[0] * (acc2_ref[...] / l2)
    r = lax.rsqrt(jnp.mean(d * d, axis=0, keepdims=True) + RMS_EPS)
    o_ref[...] = (d * r).T * (subln_ref[...] * out_scale)


def _diff_attention(qkv, lam_f, subln, batch, seq, col_q, col_k, col_v, n_heads, out_scale, tq=256):
    dv = 2 * HEAD_DIM
    kern = functools.partial(_diff_kernel, tq=tq, n_heads=n_heads, out_scale=out_scale)
    grid_spec = pltpu.PrefetchScalarGridSpec(
        num_scalar_prefetch=1,
        grid=(batch, n_heads, seq // tq),
        in_specs=[
            pl.BlockSpec((None, tq, dv), lambda b, h, i, s: (b, i, col_q + h)),
            pl.BlockSpec((None, seq, dv), lambda b, h, i, s: (b, 0, col_k + h)),
            pl.BlockSpec((None, seq, dv), lambda b, h, i, s: (b, 0, col_v + h)),
            pl.BlockSpec((1, dv), lambda b, h, i, s: (0, 0)),
        ],
        out_specs=pl.BlockSpec((None, tq, dv), lambda b, h, i, s: (b, i, h)),
        scratch_shapes=[
            pltpu.VMEM((seq, LANES), BF16), pltpu.VMEM((seq, LANES), BF16),
            pltpu.VMEM((dv, seq), BF16),
            pltpu.VMEM((dv, tq), F32), pltpu.VMEM((dv, tq), F32),
        ],
    )
    return pl.pallas_call(
        kern,
        grid_spec=grid_spec,
        out_shape=jax.ShapeDtypeStruct((batch, seq, n_heads * dv), F32),
        compiler_params=pltpu.CompilerParams(
            dimension_semantics=("parallel", "parallel", "arbitrary"),
            vmem_limit_bytes=VMEM_LIMIT),
        name="diff_attn",
    )(lam_f.reshape(1).astype(F32), qkv, qkv, qkv, subln.reshape(1, dv).astype(F32))


BAND = 128
DILATIONS = (1, 4, 16)


def _dilated_kernel(q_ref, kp_ref, kc_ref, vp_ref, vc_ref, o_ref, kbuf_ref, vbuf_ref, st_ref,
                    *, chunk, n_heads):
    pair = pl.program_id(1)
    ci = pl.program_id(2)
    kbuf_ref[:chunk, :] = kp_ref[...]
    kbuf_ref[chunk:, :] = kc_ref[...]
    vbuf_ref[:chunk, :] = vp_ref[...]
    vbuf_ref[chunk:, :] = vc_ref[...]

    qi = lax.broadcasted_iota(jnp.int32, (BAND, 2 * BAND), 0)
    kj = lax.broadcasted_iota(jnp.int32, (BAND, 2 * BAND), 1)
    dist = qi + BAND - kj
    in_band = (dist >= 0) & (dist <= BAND)
    dist_f = dist.astype(F32)
    slopes = [jnp.exp2(-8.0 * (2 * pair + e + 1).astype(F32) / n_heads) for e in range(2)]

    for pi, dil in enumerate(DILATIONS):
        span = dil * BAND
        n_blk = chunk // span

        def block(idx, carry, dil=dil, span=span, n_blk=n_blk, pi=pi):
            r = idx // n_blk
            b = idx % n_blk
            q0 = r + b * span
            qb = q_ref[pl.ds(q0, BAND, stride=dil), :]
            kb = kbuf_ref[pl.ds(chunk + q0 - span, 2 * BAND, stride=dil), :]
            vb = vbuf_ref[pl.ds(chunk + q0 - span, 2 * BAND, stride=dil), :]
            first = jnp.logical_and(ci == 0, b == 0)
            ok = in_band & (kj >= jnp.where(first, BAND, 0))
            packed = []
            for e in range(2):
                cols = slice(e * HEAD_DIM, (e + 1) * HEAD_DIM)
                s = _dot_nt(qb[:, cols].astype(BF16), kb[:, cols].astype(BF16))
                s = jnp.where(ok, s - (slopes[e] * dil) * dist_f, NEG)
                m = jnp.max(s, axis=1, keepdims=True)
                p = jnp.exp(s - m)
                l = jnp.sum(p, axis=1, keepdims=True)
                o = _dot(p.astype(BF16), vb[:, cols].astype(BF16)) / l
                lse = jnp.broadcast_to(m + jnp.log(l), (BAND, HEAD_DIM))
                packed.append(jnp.concatenate([o, lse], axis=1))
            for e in range(2):
                st_ref[pi, e, pl.ds(q0, BAND, stride=dil), :] = packed[e]
            return carry

        lax.fori_loop(0, dil * n_blk, block, 0)

    rows_per = 256
    lane = lax.broadcasted_iota(jnp.int32, (rows_per, LANES), 1)

    def merge(c, carry):
        rows = pl.ds(pl.multiple_of(c * rows_per, rows_per), rows_per)
        outs = []
        for e in range(2):
            xs = [st_ref[pi, e, rows, :] for pi in range(len(DILATIONS))]
            top = functools.reduce(jnp.maximum, xs)
            num = jnp.zeros((rows_per, LANES), F32)
            den = jnp.zeros((rows_per, LANES), F32)
            for x in xs:
                w = pltpu.roll(jnp.exp(x - top), HEAD_DIM, axis=1)
                num = num + w * x
                den = den + w
            outs.append(num / den)
        o_ref[rows, :] = jnp.where(lane < HEAD_DIM, outs[0], pltpu.roll(outs[1], HEAD_DIM, axis=1))
        return carry

    lax.fori_loop(0, chunk // rows_per, merge, 0)


def _dilated_attention(act, batch, seq, col_q, col_k, col_v, n_heads, chunk=2048):
    assert chunk % (max(DILATIONS) * BAND) == 0 and seq % chunk == 0
    prev = lambda c: jnp.maximum(c - 1, 0)
    return pl.pallas_call(
        functools.partial(_dilated_kernel, chunk=chunk, n_heads=n_heads),
        grid=(batch, n_heads // 2, seq // chunk),
        in_specs=[
            pl.BlockSpec((None, chunk, LANES), lambda b, p, c: (b, c, col_q + p)),
            pl.BlockSpec((None, chunk, LANES), lambda b, p, c: (b, prev(c), col_k + p)),
            pl.BlockSpec((None, chunk, LANES), lambda b, p, c: (b, c, col_k + p)),
            pl.BlockSpec((None, chunk, LANES), lambda b, p, c: (b, prev(c), col_v + p)),
            pl.BlockSpec((None, chunk, LANES), lambda b, p, c: (b, c, col_v + p)),
        ],
        out_specs=pl.BlockSpec((None, chunk, LANES), lambda b, p, c: (b, c, p)),
        out_shape=jax.ShapeDtypeStruct((batch, seq, n_heads * HEAD_DIM), F32),
        scratch_shapes=[
            pltpu.VMEM((2 * chunk, LANES), F32),
            pltpu.VMEM((2 * chunk, LANES), F32),
            pltpu.VMEM((len(DILATIONS), 2, chunk, LANES), F32),
        ],
        compiler_params=pltpu.CompilerParams(
            dimension_semantics=("parallel", "parallel", "arbitrary"),
            vmem_limit_bytes=VMEM_LIMIT),
        name="dilated_attn",
    )(act, act, act, act, act)


TAKEN = -3e38


def _topk_member(cur, k):
    n = cur.shape[0]
    row = lax.broadcasted_iota(jnp.int32, cur.shape, 0)
    sel = jnp.zeros(cur.shape, jnp.bool_)
    for _ in range(k):
        mx = jnp.max(cur, axis=0, keepdims=True)
        first = jnp.min(jnp.where(cur == mx, row, n), axis=0, keepdims=True)
        pick = row == first
        sel = jnp.logical_or(sel, pick)
        cur = jnp.where(pick, TAKEN, cur)
    return sel


def _moba_kernel(q_ref, k_ref, v_ref, o_ref, kaug_ref, vt_ref, kmean_ref, selb_ref, acc_ref,
                 *, blk, topk, n_heads):
    pair = pl.program_id(1)
    qi = pl.program_id(2)
    S = k_ref.shape[0]
    nblk = S // blk
    tq = tk = blk

    @pl.when(qi == 0)
    def _():
        jj = lax.broadcasted_iota(jnp.int32, (tk, 1), 0).astype(F32)
        kaug = _aug_const(tk, [jj]).astype(BF16)

        def body(c, carry):
            rows = pl.ds(pl.multiple_of(c * tk, tk), tk)
            kc = k_ref[rows, :]
            vc = v_ref[rows, :].astype(F32).T.astype(BF16)
            for e in range(2):
                kaug_ref[e, rows, :] = jnp.concatenate(
                    [kc[:, e * HEAD_DIM:(e + 1) * HEAD_DIM], kaug], axis=1)
                vt_ref[e, :, rows] = vc[e * HEAD_DIM:(e + 1) * HEAD_DIM, :]
            kmean_ref[pl.ds(c, 1), :] = jnp.mean(kc.astype(F32), axis=0, keepdims=True)
            return carry

        lax.fori_loop(0, nblk, body, 0)

    qt = q_ref[...]
    blk_row = lax.broadcasted_iota(jnp.int32, (nblk, tq), 0)
    past = blk_row < qi
    qs, slopes = [], []
    for e in range(2):
        slope = jnp.exp2(-8.0 * (2 * pair + e + 1).astype(F32) / n_heads)
        qe = qt[:, e * HEAD_DIM:(e + 1) * HEAD_DIM]
        km_hi, km_lo = _split_bf16(kmean_ref[:, e * HEAD_DIM:(e + 1) * HEAD_DIM])
        gate = _dot_nt(km_hi, qe) + _dot_nt(km_lo, qe)
        sel = _topk_member(jnp.where(past, gate, -2e38), topk)
        selb_ref[e] = jnp.where(jnp.logical_and(sel, past), 0.0, NEG)
        qs.append(jnp.concatenate([qe, _aug_const(tq, [slope]).astype(BF16)], axis=1))
        slopes.append(slope)

    acc_ref[...] = jnp.zeros_like(acc_ref)
    init = jnp.full((1, tq), NEG, F32)
    zero = jnp.zeros((1, tq), F32)

    def tile(n, carry, diag):
        rows = pl.ds(pl.multiple_of(n * tk, tk), tk)
        out = []
        for e in range(2):
            m, l = carry[2 * e], carry[2 * e + 1]
            t = _dot_nt(kaug_ref[e, rows, :], qs[e])
            shift_c = slopes[e] * ((n - qi) * tk).astype(F32)
            if diag:
                causal = (lax.broadcasted_iota(jnp.int32, (tk, tq), 0)
                          <= lax.broadcasted_iota(jnp.int32, (tk, tq), 1))
                t = jnp.where(causal, t, NEG)
                bias = None
            else:
                bias = selb_ref[e, pl.ds(n, 1), :]
            m, l = _flash_step(t, vt_ref[e, :, rows], m, l, acc_ref.at[e], shift_c, bias)
            out += [m, l]
        return tuple(out)

    carry = tile(qi, (init, zero, init, zero), True)
    carry = lax.fori_loop(0, qi, lambda n, c: tile(n, c, False), carry)
    o_ref[...] = jnp.concatenate([acc_ref[e] / carry[2 * e + 1] for e in range(2)], axis=0).T


def _moba_attention(qkv, batch, seq, col_q, col_k, col_v, n_heads, blk, topk):
    kern = functools.partial(_moba_kernel, blk=blk, topk=topk, n_heads=n_heads)
    nblk = seq // blk
    return pl.pallas_call(
        kern,
        grid=(batch, n_heads // 2, nblk),
        in_specs=[
            pl.BlockSpec((None, blk, LANES), lambda b, p, i: (b, i, col_q + p)),
            pl.BlockSpec((None, seq, LANES), lambda b, p, i: (b, 0, col_k + p)),
            pl.BlockSpec((None, seq, LANES), lambda b, p, i: (b, 0, col_v + p)),
        ],
        out_specs=pl.BlockSpec((None, blk, LANES), lambda b, p, i: (b, i, p)),
        out_shape=jax.ShapeDtypeStruct((batch, seq, n_heads * HEAD_DIM), F32),
        scratch_shapes=[
            pltpu.VMEM((2, seq, LANES), BF16),
            pltpu.VMEM((2, HEAD_DIM, seq), BF16),
            pltpu.VMEM((nblk, LANES), F32),
            pltpu.VMEM((2, nblk, blk), F32),
            pltpu.VMEM((2, HEAD_DIM, blk), F32),
        ],
        compiler_params=pltpu.CompilerParams(
            dimension_semantics=("parallel", "parallel", "arbitrary"),
            vmem_limit_bytes=VMEM_LIMIT),
        name="moba_attn",
    )(qkv, qkv, qkv)


CMP_STRIDE = 16
CMP_LEN = 32
SLC_BLK = 64
SLC_TOPK = 16
NSA_WINDOW = 512


def _compress_kernel(c_ref, pe_ref, w1_ref, w2_ref, g_ref, o_ref, *, is_key):
    half = c_ref.shape[1]
    n = c_ref.shape[0]
    c = c_ref[...]
    w1 = w1_ref[...]
    a_lo = _dot(c, w1[:half])
    a_hi = _dot(c, w1[half:])
    pe_hi, pe_lo = _split_bf16(pe_ref[...])
    c0 = (_dot(pe_hi, w1) + _dot(pe_lo, w1))[0:1]
    y = a_lo + pltpu.roll(a_hi, n - 1, axis=0) + c0
    hid = (y * jax.nn.sigmoid(y)).astype(BF16)
    if is_key:
        z = _dot(hid, w2_ref[...])
    else:
        z = _dot(hid.astype(F32), w2_ref[...].astype(F32))
    if is_key:
        ms = jnp.sum(z * z, axis=-1, keepdims=True) * (1.0 / HEAD_DIM)
        kn = z * lax.rsqrt(ms + RMS_EPS) * g_ref[...]
        i = lax.broadcasted_iota(jnp.int32, (n, 1), 0)
        aug = _aug_const(n, [(i // 16 * 256).astype(F32), (i % 16 * 16).astype(F32), 1.0])
        o_ref[...] = jnp.concatenate([kn[:, :HEAD_DIM], aug], axis=1).astype(BF16)
    else:
        o_ref[...] = z.T[:HEAD_DIM, :].astype(BF16)


def _compress(chunks, pe, w1, w2, gain, is_key):
    B, Hkv, n, half = chunks.shape
    hidden = w1.shape[1]
    pe8 = jnp.broadcast_to(pe.reshape(1, 2 * half).astype(F32), (8, 2 * half))
    w2p = jnp.pad(w2, ((0, 0), (0, LANES - w2.shape[1]))).astype(BF16)
    g = jnp.pad(gain.astype(F32), (0, LANES - HEAD_DIM)).reshape(1, LANES)
    if is_key:
        out_shape = jax.ShapeDtypeStruct((B, Hkv, n, LANES), BF16)
        out_spec = pl.BlockSpec((None, None, n, LANES), lambda b, h: (b, h, 0, 0))
    else:
        out_shape = jax.ShapeDtypeStruct((B, Hkv, HEAD_DIM, n), BF16)
        out_spec = pl.BlockSpec((None, None, HEAD_DIM, n), lambda b, h: (b, h, 0, 0))
    return pl.pallas_call(
        functools.partial(_compress_kernel, is_key=is_key),
        grid=(B, Hkv),
        in_specs=[
            pl.BlockSpec((None, None, n, half), lambda b, h: (b, h, 0, 0)),
            pl.BlockSpec((8, 2 * half), lambda b, h: (0, 0)),
            pl.BlockSpec((2 * half, hidden), lambda b, h: (0, 0)),
            pl.BlockSpec((hidden, LANES), lambda b, h: (0, 0)),
            pl.BlockSpec((1, LANES), lambda b, h: (0, 0)),
        ],
        out_specs=out_spec,
        out_shape=out_shape,
        compiler_params=pltpu.CompilerParams(
            dimension_semantics=("parallel", "parallel"), vmem_limit_bytes=VMEM_LIMIT),
        name="nsa_compress_k" if is_key else "nsa_compress_v",
    )(chunks, pe8, w1.astype(BF16), w2p, g)


def _nsa_cmp_kernel(q_ref, kc_ref, vct_ref, ct_ref, o_ref, selb_ref, *, tq, group, n_heads):
    g = pl.program_id(1)
    qi = pl.program_id(2)
    n_cmp = kc_ref.shape[0]
    n_slc = ct_ref.shape[0]
    i0 = qi * tq
    qpos = i0 + lax.broadcasted_iota(jnp.int32, (1, tq), 1)
    cend = lax.broadcasted_iota(jnp.int32, (n_cmp, 1), 0) * CMP_STRIDE + (CMP_LEN - 1)
    valid = cend <= qpos
    qt = q_ref[...]
    kc = kc_ref[...]
    vct = vct_ref[...]
    psum = jnp.zeros((n_cmp, tq), F32)
    outs = []
    for e in range(group):
        slope = jnp.exp2(-8.0 * (group * g + e + 1).astype(F32) / n_heads)
        aug = _aug_const(tq, [slope, slope, -slope * i0.astype(F32)]).astype(BF16)
        qa = jnp.concatenate([qt[:, e * HEAD_DIM:(e + 1) * HEAD_DIM], aug], axis=1)
        t = jnp.where(valid, _dot_nt(kc, qa), NEG)
        m = jnp.max(t, axis=0, keepdims=True)
        p = jnp.where(valid, jnp.exp(t - m), 0.0)
        l = jnp.sum(p, axis=0, keepdims=True)
        p = p / jnp.where(l > 0, l, 1.0)
        outs.append(_dot(vct, p.astype(BF16)))
        psum = psum + p
    o_ref[...] = jnp.concatenate(outs, axis=0).T
    p_hi, p_lo = _split_bf16(psum)
    ct = ct_ref[...]
    imp = _dot(ct, p_hi) + _dot(ct, p_lo)
    jb = lax.broadcasted_iota(jnp.int32, (n_slc, tq), 0)
    qb = qpos // SLC_BLK
    forced = (jb == 0) | (jb == qb) | (jb == qb - 1)
    cur = jnp.where(forced, 3e38, jnp.where(jb > qb, -2e38, imp))
    sel = jnp.logical_and(_topk_member(cur, min(SLC_TOPK, n_slc)), jb <= qb)
    selb_ref[...] = jnp.where(sel, 0.0, NEG).T.astype(BF16)


def _nsa_cmp(act, kc, vct, batch, seq, col_q, n_heads, group, tq=256):
    Hkv = n_heads // group
    n_cmp = kc.shape[2]
    n_slc = seq // SLC_BLK
    cs = jnp.arange(n_cmp) * CMP_STRIDE
    ss = jnp.arange(n_slc) * SLC_BLK
    overlap = (cs[None, :] <= ss[:, None] + SLC_BLK - 1) & (cs[None, :] + CMP_LEN - 1 >= ss[:, None])
    overlap = overlap & (cs[None, :] + CMP_LEN <= seq)
    ct = overlap.astype(BF16)
    gw = group * HEAD_DIM
    return pl.pallas_call(
        functools.partial(_nsa_cmp_kernel, tq=tq, group=group, n_heads=n_heads),
        grid=(batch, Hkv, seq // tq),
        in_specs=[
            pl.BlockSpec((None, tq, gw), lambda b, g, i: (b, i, col_q + g)),
            pl.BlockSpec((None, None, n_cmp, LANES), lambda b, g, i: (b, g, 0, 0)),
            pl.BlockSpec((None, None, HEAD_DIM, n_cmp), lambda b, g, i: (b, g, 0, 0)),
            pl.BlockSpec((n_slc, n_cmp), lambda b, g, i: (0, 0)),
        ],
        out_specs=[
            pl.BlockSpec((None, tq, gw), lambda b, g, i: (b, i, g)),
            pl.BlockSpec((None, None, tq, n_slc), lambda b, g, i: (b, g, i, 0)),
        ],
        out_shape=[
            jax.ShapeDtypeStruct((batch, seq, n_heads * HEAD_DIM), F32),
            jax.ShapeDtypeStruct((batch, Hkv, seq, n_slc), BF16),
        ],
        compiler_params=pltpu.CompilerParams(
            dimension_semantics=("parallel", "parallel", "parallel"),
            vmem_limit_bytes=VMEM_LIMIT),
        name="nsa_cmp",
    )(act, kc, vct, ct)


def _nsa_main_kernel(q_ref, selb_ref, kvs_ref, kvw_ref, kconst_ref, ocmp_ref, gate_ref, o_ref,
                     ks_ref, vst_ref, kw_ref, vwt_ref, accs_ref, accw_ref, gt_ref,
                     *, tq, tk, group, n_heads):
    g = pl.program_id(1)
    qi = pl.program_id(2)
    S = kvs_ref.shape[0]
    R = group * tq
    i0 = qi * tq

    @pl.when(qi == 0)
    def _():
        def body(c, carry):
            rows = pl.ds(pl.multiple_of(c * tk, tk), tk)
            kvs = kvs_ref[rows, :]
            kvw = kvw_ref[rows, :]
            kconst = kconst_ref[...]
            ks_ref[rows, :HEAD_DIM] = kvs[:, :HEAD_DIM]
            ks_ref[rows, HEAD_DIM:LANES] = kconst[:, :HEAD_DIM]
            kw_ref[rows, :] = jnp.concatenate([kvw[:, :HEAD_DIM], kconst[:, :HEAD_DIM]], axis=1)
            vst_ref[:, rows] = kvs.astype(F32).T[HEAD_DIM:, :].astype(BF16)
            vwt_ref[:, rows] = kvw.astype(F32).T[HEAD_DIM:, :].astype(BF16)
            key_blk = (c * tk + lax.broadcasted_iota(jnp.int32, (tk, n_slc), 0)) // SLC_BLK
            onehot = key_blk == lax.broadcasted_iota(jnp.int32, (tk, n_slc), 1)
            ks_ref[rows, LANES:] = jnp.where(onehot, 1.0, 0.0).astype(BF16)
            return carry

        n_slc = ks_ref.shape[1] - LANES
        lax.fori_loop(0, S // tk, body, 0)

    qt = q_ref[...]
    selb = selb_ref[...]
    qs_parts, qw_parts, slope_parts = [], [], []
    for e in range(group):
        slope = jnp.exp2(-8.0 * (group * g + e + 1).astype(F32) / n_heads)
        aug = _aug_const(tq, [slope]).astype(BF16)
        qe = qt[:, e * HEAD_DIM:(e + 1) * HEAD_DIM]
        qw_parts.append(jnp.concatenate([qe, aug], axis=1))
        qs_parts.append(jnp.concatenate([qe, aug, selb], axis=1))
        slope_parts.append(jnp.full((1, tq), slope, F32))
    q_slc = jnp.concatenate(qs_parts, axis=0)
    q_win = jnp.concatenate(qw_parts, axis=0)
    slope_row = jnp.concatenate(slope_parts, axis=1)
    qpos = i0 + lax.broadcasted_iota(jnp.int32, (1, R), 1) % tq

    accs_ref[...] = jnp.zeros_like(accs_ref)
    accw_ref[...] = jnp.zeros_like(accw_ref)
    init = jnp.full((1, R), NEG, F32)
    zero = jnp.zeros((1, R), F32)
    nd = i0 // tk

    def shift(n):
        return slope_row * (n * tk - i0).astype(F32)

    def kpos(n):
        return n * tk + lax.broadcasted_iota(jnp.int32, (tk, 1), 0)

    def slc_tile(n, carry, diag):
        m, l = carry
        rows = pl.ds(pl.multiple_of(n * tk, tk), tk)
        t = _dot_nt(ks_ref[rows, :], q_slc)
        if diag:
            t = jnp.where(kpos(n) <= qpos, t, NEG)
        return _flash_step(t, vst_ref[:, rows], m, l, accs_ref, shift(n))

    def win_tile(n, carry):
        m, l = carry
        rows = pl.ds(pl.multiple_of(n * tk, tk), tk)
        t = _dot_nt(kw_ref[rows, :], q_win)
        dist = qpos - kpos(n)
        t = jnp.where((dist >= 0) & (dist < NSA_WINDOW), t, NEG)
        return _flash_step(t, vwt_ref[:, rows], m, l, accw_ref, shift(n))

    ms, ls = slc_tile(nd, (init, zero), True)
    ms, ls = lax.fori_loop(0, nd, lambda n, c: slc_tile(n, c, False), (ms, ls))
    mw, lw = win_tile(nd, (init, zero))
    lo = jnp.maximum(i0 - (NSA_WINDOW - 1), 0) // tk
    mw, lw = lax.fori_loop(lo, nd, win_tile, (mw, lw))

    gz = gate_ref[...]
    gt_ref[...] = jax.nn.sigmoid(gz).T
    oc_t = ocmp_ref[...].T
    outs = []
    for e in range(group):
        head = group * g + e
        cols = slice(e * tq, (e + 1) * tq)
        o_s = accs_ref[:, cols] / ls[:, cols]
        o_w = accw_ref[:, cols] / lw[:, cols]
        o_c = oc_t[e * HEAD_DIM:(e + 1) * HEAD_DIM, :]
        g_c = gt_ref[pl.ds(3 * head, 1), :]
        g_s = gt_ref[pl.ds(3 * head + 1, 1), :]
        g_w = gt_ref[pl.ds(3 * head + 2, 1), :]
        outs.append(g_c * o_c + g_s * o_s + g_w * o_w)
    o_ref[...] = jnp.concatenate(outs, axis=0).T


def _nsa_main(act, selb, ocmp, gates, batch, seq, col_q, col_kvs, col_kvw, col_gate, n_heads, group,
              tq=128, tk=256):
    Hkv = n_heads // group
    n_slc = seq // SLC_BLK
    gw = group * HEAD_DIM
    R = group * tq
    jj = jnp.arange(tk, dtype=F32)
    kconst = jnp.zeros((tk, HEAD_DIM), F32).at[:, 0].set(jj).astype(BF16)
    return pl.pallas_call(
        functools.partial(_nsa_main_kernel, tq=tq, tk=tk, group=group, n_heads=n_heads),
        grid=(batch, Hkv, seq // tq),
        in_specs=[
            pl.BlockSpec((None, tq, gw), lambda b, g, i: (b, i, col_q + g)),
            pl.BlockSpec((None, None, tq, n_slc), lambda b, g, i: (b, g, i, 0)),
            pl.BlockSpec((None, seq, LANES), lambda b, g, i: (b, 0, col_kvs + g)),
            pl.BlockSpec((None, seq, LANES), lambda b, g, i: (b, 0, col_kvw + g)),
            pl.BlockSpec((tk, HEAD_DIM), lambda b, g, i: (0, 0)),
            pl.BlockSpec((None, tq, gw), lambda b, g, i: (b, i, g)),
            pl.BlockSpec((None, tq, LANES), lambda b, g, i: (b, i, col_gate)),
        ],
        out_specs=pl.BlockSpec((None, tq, gw), lambda b, g, i: (b, i, g)),
        out_shape=jax.ShapeDtypeStruct((batch, seq, n_heads * HEAD_DIM), F32),
        scratch_shapes=[
            pltpu.VMEM((seq, LANES + n_slc), BF16),
            pltpu.VMEM((HEAD_DIM, seq), BF16),
            pltpu.VMEM((seq, LANES), BF16),
            pltpu.VMEM((HEAD_DIM, seq), BF16),
            pltpu.VMEM((HEAD_DIM, R), F32),
            pltpu.VMEM((HEAD_DIM, R), F32),
            pltpu.VMEM((LANES, tq), F32),
        ],
        compiler_params=pltpu.CompilerParams(
            dimension_semantics=("parallel", "parallel", "arbitrary"),
            vmem_limit_bytes=VMEM_LIMIT),
        name="nsa_main",
    )(act, selb, act, act, kconst, ocmp, gates)


def _tile_gain(g, width):
    return jnp.tile(g.astype(F32), width // HEAD_DIM)


def _even_layer(x, ln, w_in, qkn, lam, subln, w_out, layer):
    B, S, D = x.shape
    M = B * S
    x2d = x.reshape(M, D)
    W = 512
    aq, ak, av, az, bq, bk, bv, bz = [w_in[:, i * W:(i + 1) * W] for i in range(8)]
    scale = HEAD_DIM ** -0.5
    ones, zeros = jnp.ones((W,), F32), jnp.zeros((W,), F32)
    w_b = jnp.concatenate([bq, bk, bv], axis=1).astype(BF16)
    gain_b = jnp.concatenate([_tile_gain(qkn[2], W) * scale, _tile_gain(qkn[3], W), ones])
    mask_b = jnp.concatenate([ones, ones, zeros])
    act_b = _project(x2d, ln, w_b, gain_b, mask_b, BF16)
    w_a = jnp.concatenate([aq, ak, av, az, bz], axis=1).astype(BF16)
    gain_a = jnp.concatenate([_tile_gain(qkn[0], W) * scale, _tile_gain(qkn[1], W), ones, ones, ones])
    mask_a = jnp.concatenate([ones, ones, zeros, zeros, zeros])
    act_a = _project(x2d, ln, w_a, gain_a, mask_a, F32)

    lam_init = 0.8 - 0.6 * math.exp(-0.3 * layer)
    lf = lam.astype(F32)
    lam_f = jnp.exp(jnp.sum(lf[0] * lf[1])) - jnp.exp(jnp.sum(lf[2] * lf[3])) + lam_init
    ob = _diff_attention(act_b.reshape(B, S, 3 * W), lam_f, subln, B, S, 0, 4, 8, 4, 1.0 - lam_init)
    oa = _dilated_attention(act_a.reshape(B, S, 5 * W), B, S, 0, 4, 8, 8)
    out = _out_project(x2d, oa.reshape(M, W), ob.reshape(M, W), act_a, 3, 4, w_out)
    return out.reshape(B, S, D)


def _odd_layer(x, ln, w_in, qkn, phi_pe, phi_w1, phi_w2, w_out):
    B, S, D = x.shape
    M = B * S
    x2d = x.reshape(M, D)
    W, KW = 512, 128
    offs = [0]
    for s in (W, W, W, W, W, KW, KW, KW, KW, KW, KW, W, 24):
        offs.append(offs[-1] + s)
    cq, ck, cv, cz, dq, dkc, dvc, dks, dvs, dkw, dvw, dz, dg = [
        w_in[:, offs[i]:offs[i + 1]] for i in range(13)]
    scale = HEAD_DIM ** -0.5
    hd = HEAD_DIM
    ones = lambda n: jnp.ones((n,), F32)
    zeros = lambda n: jnp.zeros((n,), F32)
    kv_pair = lambda k, v: jnp.concatenate([k[:, :hd], v[:, :hd], k[:, hd:], v[:, hd:]], axis=1)
    w_b = jnp.concatenate([cq, ck, cv, dq, kv_pair(dks, dvs), kv_pair(dkw, dvw), dkc, dvc,
                           jnp.zeros((D, 2 * KW), w_in.dtype)], axis=1).astype(BF16)
    kv_gain = lambda g: jnp.concatenate([g, ones(hd), g, ones(hd)])
    kv_mask = jnp.concatenate([ones(hd), zeros(hd), ones(hd), zeros(hd)])
    gain_b = jnp.concatenate([_tile_gain(qkn[0], W) * scale, _tile_gain(qkn[1], W), ones(W),
                              _tile_gain(qkn[2], W) * scale, kv_gain(qkn[4].astype(F32)),
                              kv_gain(qkn[5].astype(F32)), ones(4 * KW)])
    mask_b = jnp.concatenate([ones(W), ones(W), zeros(W), ones(W), kv_mask, kv_mask, zeros(4 * KW)])
    act_b = _project(x2d, ln, w_b, gain_b, mask_b, BF16).reshape(B, S, 6 * W)
    w_g = jnp.concatenate([cz, dz, dg, jnp.zeros((D, W - 24), w_in.dtype)], axis=1).astype(BF16)
    act_g = _project(x2d, ln, w_g, ones(3 * W), zeros(3 * W), F32)

    oc = _moba_attention(act_b, B, S, 0, 4, 8, 8, 256, 3)

    def chunks(t):
        t = t.reshape(B, S // CMP_STRIDE, CMP_STRIDE, 2, hd)
        return jnp.transpose(t, (0, 3, 1, 2, 4)).reshape(B, 2, S // CMP_STRIDE, CMP_STRIDE * hd)

    kc = _compress(chunks(act_b[:, :, 20 * KW:21 * KW]), phi_pe[0], phi_w1[0], phi_w2[0], qkn[3], True)
    vct = _compress(chunks(act_b[:, :, 21 * KW:22 * KW]), phi_pe[1], phi_w1[1], phi_w2[1], qkn[3], False)
    ocmp, selb = _nsa_cmp(act_b, kc, vct, B, S, 6, 8, 4)
    od = _nsa_main(act_b, selb, ocmp, act_g.reshape(B, S, 3 * W), B, S, 6, 16, 18, 8, 8, 4)
    out = _out_project(x2d, oc.reshape(M, W), od.reshape(M, W), act_g, 0, 1, w_out)
    return out.reshape(B, S, D)


def kernel(x, ln_e, w_in_e, qkn_e, lam_e, subln_e, w_out_e, ln_o, w_in_o, qkn_o, phi_pe, phi_w1, phi_w2, w_out_o):
    n_layers = ln_e.shape[0] + ln_o.shape[0]
    for layer in range(n_layers):
        i = layer // 2
        if layer % 2 == 0:
            x = _even_layer(x, ln_e[i], w_in_e[i], qkn_e[i], lam_e[i], subln_e[i], w_out_e[i], layer)
        else:
            x = _odd_layer(x, ln_o[i], w_in_o[i], qkn_o[i], phi_pe[i], phi_w1[i], phi_w2[i], w_out_o[i])
    return x
```

```python
import functools
import math

import jax
import jax.numpy as jnp
from jax import lax
from jax.experimental import pallas as pl
from jax.experimental.pallas import tpu as pltpu

HEAD_DIM = 64
LANES = 128
RMS_EPS = 1e-6
NEG = -1e30
VMEM_LIMIT = 56 * 1024 * 1024

F32 = jnp.float32
BF16 = jnp.bfloat16

_NT = (((1,), (1,)), ((), ()))


def _dot(a, b):
    return jnp.dot(a, b, preferred_element_type=F32)


def _dot_nt(a, b):
    return lax.dot_general(a, b, _NT, preferred_element_type=F32)


def _alibi_slopes(n):
    return [2.0 ** (-8.0 * (i + 1) / n) for i in range(n)]


def _split_bf16(x):
    hi = x.astype(BF16)
    lo = (x - hi.astype(F32)).astype(BF16)
    return hi, lo


def _proj_kernel(flags_ref, x_ref, ln_ref, w_ref, gain_ref, nmask_ref, bd_ref, o_ref, h_ref):
    j = pl.program_id(1)

    @pl.when(j == 0)
    def _():
        x = x_ref[...]
        ms = jnp.mean(x * x, axis=-1, keepdims=True)
        h_ref[...] = (x * lax.rsqrt(ms + RMS_EPS) * ln_ref[...]).astype(BF16)

    y = _dot(h_ref[...], w_ref[...])

    @pl.when(flags_ref[j] == 0)
    def _():
        o_ref[...] = y.astype(o_ref.dtype)

    @pl.when(flags_ref[j] != 0)
    def _():
        tn = y.shape[1]
        bd = bd_ref[...]
        for c in range(tn // 256):
            yc = y[:, c * 256:(c + 1) * 256]
            hi, lo = _split_bf16(yc * yc)
            ms = _dot(hi, bd) + _dot(lo, bd)
            yn = yc * lax.rsqrt(ms + RMS_EPS) * gain_ref[:, c * 256:(c + 1) * 256]
            keep = nmask_ref[:, c * 256:(c + 1) * 256] > 0.5
            o_ref[:, c * 256:(c + 1) * 256] = jnp.where(keep, yn, yc).astype(o_ref.dtype)


def _project(x2d, ln, w, gain, nmask, out_dtype, tm=512, tn=512):
    M, D = x2d.shape
    N = w.shape[1]
    assert M % tm == 0 and N % tn == 0 and tn % 256 == 0
    flags = (jnp.max(nmask.reshape(N // tn, tn), axis=1) > 0.5).astype(jnp.int32)
    r = jnp.arange(256) // HEAD_DIM
    bd = jnp.where(r[:, None] == r[None, :], 1.0 / HEAD_DIM, 0.0).astype(BF16)
    grid_spec = pltpu.PrefetchScalarGridSpec(
        num_scalar_prefetch=1,
        grid=(M // tm, N // tn),
        in_specs=[
            pl.BlockSpec((tm, D), lambda i, j, f: (i, 0)),
            pl.BlockSpec((1, D), lambda i, j, f: (0, 0)),
            pl.BlockSpec((D, tn), lambda i, j, f: (0, j)),
            pl.BlockSpec((1, tn), lambda i, j, f: (0, j)),
            pl.BlockSpec((1, tn), lambda i, j, f: (0, j)),
            pl.BlockSpec((256, 256), lambda i, j, f: (0, 0)),
        ],
        out_specs=pl.BlockSpec((tm, tn), lambda i, j, f: (i, j)),
        scratch_shapes=[pltpu.VMEM((tm, D), BF16)],
    )
    return pl.pallas_call(
        _proj_kernel,
        grid_spec=grid_spec,
        out_shape=jax.ShapeDtypeStruct((M, N), out_dtype),
        compiler_params=pltpu.CompilerParams(
            dimension_semantics=("parallel", "arbitrary"), vmem_limit_bytes=VMEM_LIMIT),
        name="proj",
    )(flags, x2d, ln.reshape(1, D).astype(F32), w, gain.reshape(1, N).astype(F32),
      nmask.reshape(1, N).astype(F32), bd)


def _outproj_kernel(x_ref, oa_ref, ob_ref, za_ref, zb_ref, wa_ref, wb_ref, o_ref):
    za = za_ref[...]
    zb = zb_ref[...]
    ma = (oa_ref[...] * (za * jax.nn.sigmoid(za))).astype(BF16)
    mb = (ob_ref[...] * (zb * jax.nn.sigmoid(zb))).astype(BF16)
    o_ref[...] = x_ref[...] + _dot(ma, wa_ref[...]) + _dot(mb, wb_ref[...])


def _out_project(x2d, oa, ob, z, za_blk, zb_blk, w_out, tm=512):
    M, D = x2d.shape
    W = oa.shape[1]
    wa = w_out[:W].astype(BF16)
    wb = w_out[W:].astype(BF16)
    return pl.pallas_call(
        _outproj_kernel,
        grid=(M // tm,),
        in_specs=[
            pl.BlockSpec((tm, D), lambda i: (i, 0)),
            pl.BlockSpec((tm, W), lambda i: (i, 0)),
            pl.BlockSpec((tm, W), lambda i: (i, 0)),
            pl.BlockSpec((tm, W), lambda i: (i, za_blk)),
            pl.BlockSpec((tm, W), lambda i: (i, zb_blk)),
            pl.BlockSpec((W, D), lambda i: (0, 0)),
            pl.BlockSpec((W, D), lambda i: (0, 0)),
        ],
        out_specs=pl.BlockSpec((tm, D), lambda i: (i, 0)),
        out_shape=jax.ShapeDtypeStruct((M, D), F32),
        compiler_params=pltpu.CompilerParams(
            dimension_semantics=("parallel",), vmem_limit_bytes=VMEM_LIMIT),
        name="outproj",
    )(x2d, oa, ob, z, z, wa, wb)


class _Stream:
    def __init__(self, q, k_tile, vt_tile, acc_ref, t_ref, p_ref, shift, bias=None,
                 mask_first=None, mask_loop=None):
        self.q, self.k_tile, self.vt_tile = q, k_tile, vt_tile
        self.acc_ref, self.t_ref, self.p_ref = acc_ref, t_ref, p_ref
        self.shift, self.bias, self.mask_first, self.mask_loop = shift, bias, mask_first, mask_loop

    def scores(self, n):
        return _dot_nt(self.k_tile(n), self.q)

    def softmax(self, t, n, m, l, bias, mask, extra=None):
        if mask is not None:
            t = jnp.where(mask(n), t, NEG)
        off = self.shift(n)
        if bias is not None:
            off = off + bias(n)
        if extra is not None:
            off = off + extra
        m_new = jnp.maximum(m, jnp.max(t, axis=0, keepdims=True) + off)
        p = jnp.exp(t - (m_new - off))
        alpha = jnp.exp(m - m_new)
        return m_new, alpha * l + jnp.sum(p, axis=0, keepdims=True), alpha, p.astype(BF16)


FLASH_UNROLL = 2


def _flash_sweep(streams, first, lo, hi):
    U = FLASH_UNROLL
    R = streams[0].q.shape[0]
    init = jnp.full((1, R), NEG, F32)
    zero = jnp.zeros((1, R), F32)
    state = []
    for s in streams:
        s.acc_ref[...] = jnp.zeros_like(s.acc_ref)
        m, l, a, p = s.softmax(s.scores(first), first, init, zero, None, s.mask_first)
        s.p_ref[...] = p
        state += [m, l, a]
    last = jnp.maximum(hi - 1, lo)
    for s in streams:
        for k in range(U):
            s.t_ref[k] = s.scores(jnp.minimum(lo + k, last))

    def body(g, carry):
        n_prev, state = carry[0], list(carry[1:])
        base = lo + U * g
        pvs = [_dot(s.vt_tile(n_prev), s.p_ref[...]) for s in streams]
        nxt = [[s.scores(jnp.minimum(base + U + k, last)) for k in range(U)] for s in streams]
        for i, s in enumerate(streams):
            s.acc_ref[...] = state[3 * i + 2] * s.acc_ref[...] + pvs[i]
        for k in range(U):
            n = jnp.minimum(base + k, last)
            pad = jnp.where(base + k < hi, 0.0, NEG)
            for i, s in enumerate(streams):
                m, l, _ = state[3 * i:3 * i + 3]
                m, l, a, p = s.softmax(s.t_ref[k], n, m, l, s.bias, s.mask_loop, pad)
                state[3 * i:3 * i + 3] = [m, l, a]
                if k < U - 1:
                    s.acc_ref[...] = a * s.acc_ref[...] + _dot(s.vt_tile(n), p)
                else:
                    s.p_ref[...] = p
            n_prev = n
        for i, s in enumerate(streams):
            for k in range(U):
                s.t_ref[k] = nxt[i][k]
        return (n_prev,) + tuple(state)

    carry = lax.fori_loop(0, (hi - lo + U - 1) // U, body, (first,) + tuple(state))
    n_prev, state = carry[0], carry[1:]
    res = []
    for i, s in enumerate(streams):
        m, l, a_prev = state[3 * i:3 * i + 3]
        s.acc_ref[...] = a_prev * s.acc_ref[...] + _dot(s.vt_tile(n_prev), s.p_ref[...])
        res.append((m, l))
    return res


def _aug_const(rows, col_vals):
    lane = lax.broadcasted_iota(jnp.int32, (rows, HEAD_DIM), 1)
    out = jnp.zeros((rows, HEAD_DIM), F32)
    for c, v in enumerate(col_vals):
        out = jnp.where(lane == c, v, out)
    return out


def _diff_kernel(lam_ref, q_ref, k_ref, v_ref, subln_ref, o_ref,
                 k1_ref, k2_ref, vt_ref, acc1_ref, acc2_ref, t_ref, p_ref, *, tq, n_heads, out_scale):
    h = pl.program_id(1)
    qi = pl.program_id(2)
    S = k_ref.shape[0]
    tk = tq
    slope = jnp.exp2(-8.0 * (h + 1).astype(F32) / n_heads)

    @pl.when(qi == 0)
    def _():
        jj = lax.broadcasted_iota(jnp.int32, (tk, 1), 0).astype(F32)
        kaug = _aug_const(tk, [jj]).astype(BF16)

        def body(c, carry):
            rows = pl.ds(pl.multiple_of(c * tk, tk), tk)
            kc = k_ref[rows, :]
            k1_ref[rows, :] = jnp.concatenate([kc[:, :HEAD_DIM], kaug], axis=1)
            k2_ref[rows, :] = jnp.concatenate([kc[:, HEAD_DIM:], kaug], axis=1)
            vt_ref[:, rows] = v_ref[rows, :].astype(F32).T.astype(BF16)
            return carry

        lax.fori_loop(0, S // tk, body, 0)

    qt = q_ref[...]
    qaug = _aug_const(tq, [slope]).astype(BF16)
    q1 = jnp.concatenate([qt[:, :HEAD_DIM], qaug], axis=1)
    q2 = jnp.concatenate([qt[:, HEAD_DIM:], qaug], axis=1)

    rows = lambda n: pl.ds(pl.multiple_of(n * tk, tk), tk)
    causal = lambda n: (lax.broadcasted_iota(jnp.int32, (tk, tq), 0)
                        <= lax.broadcasted_iota(jnp.int32, (tk, tq), 1))
    shift = lambda n: slope * ((n - qi) * tk).astype(F32)
    vt_tile = lambda n: vt_ref[:, rows(n)]
    streams = [
        _Stream(q1, lambda n: k1_ref[rows(n), :], vt_tile, acc1_ref, t_ref.at[0], p_ref.at[0],
                shift, mask_first=causal),
        _Stream(q2, lambda n: k2_ref[rows(n), :], vt_tile, acc2_ref, t_ref.at[1], p_ref.at[1],
                shift, mask_first=causal),
    ]
    (m1, l1), (m2, l2) = _flash_sweep(streams, qi, 0, qi)

    d = acc1_ref[...] / l1 - lam_ref[0] * (acc2_ref[...] / l2)
    r = lax.rsqrt(jnp.mean(d * d, axis=0, keepdims=True) + RMS_EPS)
    o_ref[...] = (d * r).T * (subln_ref[...] * out_scale)


def _diff_attention(qkv, lam_f, subln, batch, seq, col_q, col_k, col_v, n_heads, out_scale, tq=256):
    dv = 2 * HEAD_DIM
    kern = functools.partial(_diff_kernel, tq=tq, n_heads=n_heads, out_scale=out_scale)
    grid_spec = pltpu.PrefetchScalarGridSpec(
        num_scalar_prefetch=1,
        grid=(batch, n_heads, seq // tq),
        in_specs=[
            pl.BlockSpec((None, tq, dv), lambda b, h, i, s: (b, i, col_q + h)),
            pl.BlockSpec((None, seq, dv), lambda b, h, i, s: (b, 0, col_k + h)),
            pl.BlockSpec((None, seq, dv), lambda b, h, i, s: (b, 0, col_v + h)),
            pl.BlockSpec((1, dv), lambda b, h, i, s: (0, 0)),
        ],
        out_specs=pl.BlockSpec((None, tq, dv), lambda b, h, i, s: (b, i, h)),
        scratch_shapes=[
            pltpu.VMEM((seq, LANES), BF16), pltpu.VMEM((seq, LANES), BF16),
            pltpu.VMEM((dv, seq), BF16),
            pltpu.VMEM((dv, tq), F32), pltpu.VMEM((dv, tq), F32),
            pltpu.VMEM((2, FLASH_UNROLL, tq, tq), F32), pltpu.VMEM((2, tq, tq), BF16),
        ],
    )
    return pl.pallas_call(
        kern,
        grid_spec=grid_spec,
        out_shape=jax.ShapeDtypeStruct((batch, seq, n_heads * dv), F32),
        compiler_params=pltpu.CompilerParams(
            dimension_semantics=("parallel", "parallel", "arbitrary"),
            vmem_limit_bytes=VMEM_LIMIT),
        name="diff_attn",
    )(lam_f.reshape(1).astype(F32), qkv, qkv, qkv, subln.reshape(1, dv).astype(F32))


BAND = 128
DILATIONS = (1, 4, 16)


def _dilated_kernel(q_ref, kp_ref, kc_ref, vp_ref, vc_ref, o_ref, kbuf_ref, vbuf_ref, st_ref,
                    *, chunk, n_heads):
    pair = pl.program_id(1)
    ci = pl.program_id(2)
    kbuf_ref[:chunk, :] = kp_ref[...]
    kbuf_ref[chunk:, :] = kc_ref[...]
    vbuf_ref[:chunk, :] = vp_ref[...]
    vbuf_ref[chunk:, :] = vc_ref[...]

    qi = lax.broadcasted_iota(jnp.int32, (BAND, 2 * BAND), 0)
    kj = lax.broadcasted_iota(jnp.int32, (BAND, 2 * BAND), 1)
    dist = qi + BAND - kj
    in_band = (dist >= 0) & (dist <= BAND)
    dist_f = dist.astype(F32)
    slopes = [jnp.exp2(-8.0 * (2 * pair + e + 1).astype(F32) / n_heads) for e in range(2)]

    for pi, dil in enumerate(DILATIONS):
        span = dil * BAND
        n_blk = chunk // span

        def block(idx, carry, dil=dil, span=span, n_blk=n_blk, pi=pi):
            r = idx // n_blk
            b = idx % n_blk
            q0 = r + b * span
            qb = q_ref[pl.ds(q0, BAND, stride=dil), :]
            kb = kbuf_ref[pl.ds(chunk + q0 - span, 2 * BAND, stride=dil), :]
            vb = vbuf_ref[pl.ds(chunk + q0 - span, 2 * BAND, stride=dil), :]
            first = jnp.logical_and(ci == 0, b == 0)
            ok = in_band & (kj >= jnp.where(first, BAND, 0))
            packed = []
            for e in range(2):
                cols = slice(e * HEAD_DIM, (e + 1) * HEAD_DIM)
                s = _dot_nt(qb[:, cols].astype(BF16), kb[:, cols].astype(BF16))
                s = jnp.where(ok, s - (slopes[e] * dil) * dist_f, NEG)
                m = jnp.max(s, axis=1, keepdims=True)
                p = jnp.exp(s - m)
                l = jnp.sum(p, axis=1, keepdims=True)
                o = _dot(p.astype(BF16), vb[:, cols].astype(BF16)) / l
                lse = jnp.broadcast_to(m + jnp.log(l), (BAND, HEAD_DIM))
                packed.append(jnp.concatenate([o, lse], axis=1))
            for e in range(2):
                st_ref[pi, e, pl.ds(q0, BAND, stride=dil), :] = packed[e]
            return carry

        lax.fori_loop(0, dil * n_blk, block, 0, unroll=4)

    rows_per = 256
    lane = lax.broadcasted_iota(jnp.int32, (rows_per, LANES), 1)

    def merge(c, carry):
        rows = pl.ds(pl.multiple_of(c * rows_per, rows_per), rows_per)
        outs = []
        for e in range(2):
            xs = [st_ref[pi, e, rows, :] for pi in range(len(DILATIONS))]
            top = functools.reduce(jnp.maximum, xs)
            num = jnp.zeros((rows_per, LANES), F32)
            den = jnp.zeros((rows_per, LANES), F32)
            for x in xs:
                w = pltpu.roll(jnp.exp(x - top), HEAD_DIM, axis=1)
                num = num + w * x
                den = den + w
            outs.append(num / den)
        o_ref[rows, :] = jnp.where(lane < HEAD_DIM, outs[0], pltpu.roll(outs[1], HEAD_DIM, axis=1))
        return carry

    lax.fori_loop(0, chunk // rows_per, merge, 0)


def _dilated_attention(act, batch, seq, col_q, col_k, col_v, n_heads, chunk=2048):
    assert chunk % (max(DILATIONS) * BAND) == 0 and seq % chunk == 0
    prev = lambda c: jnp.maximum(c - 1, 0)
    return pl.pallas_call(
        functools.partial(_dilated_kernel, chunk=chunk, n_heads=n_heads),
        grid=(batch, n_heads // 2, seq // chunk),
        in_specs=[
            pl.BlockSpec((None, chunk, LANES), lambda b, p, c: (b, c, col_q + p)),
            pl.BlockSpec((None, chunk, LANES), lambda b, p, c: (b, prev(c), col_k + p)),
            pl.BlockSpec((None, chunk, LANES), lambda b, p, c: (b, c, col_k + p)),
            pl.BlockSpec((None, chunk, LANES), lambda b, p, c: (b, prev(c), col_v + p)),
            pl.BlockSpec((None, chunk, LANES), lambda b, p, c: (b, c, col_v + p)),
        ],
        out_specs=pl.BlockSpec((None, chunk, LANES), lambda b, p, c: (b, c, p)),
        out_shape=jax.ShapeDtypeStruct((batch, seq, n_heads * HEAD_DIM), F32),
        scratch_shapes=[
            pltpu.VMEM((2 * chunk, LANES), F32),
            pltpu.VMEM((2 * chunk, LANES), F32),
            pltpu.VMEM((len(DILATIONS), 2, chunk, LANES), F32),
        ],
        compiler_params=pltpu.CompilerParams(
            dimension_semantics=("parallel", "parallel", "arbitrary"),
            vmem_limit_bytes=VMEM_LIMIT),
        name="dilated_attn",
    )(act, act, act, act, act)


TAKEN = -3e38


def _topk_member(cur, k):
    n = cur.shape[0]
    row = lax.broadcasted_iota(jnp.int32, cur.shape, 0)
    sel = jnp.zeros(cur.shape, jnp.bool_)
    for _ in range(k):
        mx = jnp.max(cur, axis=0, keepdims=True)
        first = jnp.min(jnp.where(cur == mx, row, n), axis=0, keepdims=True)
        pick = row == first
        sel = jnp.logical_or(sel, pick)
        cur = jnp.where(pick, TAKEN, cur)
    return sel


def _moba_kernel(q_ref, k_ref, v_ref, o_ref, kaug_ref, vt_ref, kmean_ref, selb_ref, acc_ref,
                 t_ref, p_ref, *, blk, topk, n_heads):
    pair = pl.program_id(1)
    qi = pl.program_id(2)
    S = k_ref.shape[0]
    nblk = S // blk
    tq = tk = blk

    @pl.when(qi == 0)
    def _():
        jj = lax.broadcasted_iota(jnp.int32, (tk, 1), 0).astype(F32)
        kaug = _aug_const(tk, [jj]).astype(BF16)

        def body(c, carry):
            rows = pl.ds(pl.multiple_of(c * tk, tk), tk)
            kc = k_ref[rows, :]
            vc = v_ref[rows, :].astype(F32).T.astype(BF16)
            for e in range(2):
                kaug_ref[e, rows, :] = jnp.concatenate(
                    [kc[:, e * HEAD_DIM:(e + 1) * HEAD_DIM], kaug], axis=1)
                vt_ref[e, :, rows] = vc[e * HEAD_DIM:(e + 1) * HEAD_DIM, :]
            kmean_ref[pl.ds(c, 1), :] = jnp.mean(kc.astype(F32), axis=0, keepdims=True)
            return carry

        lax.fori_loop(0, nblk, body, 0)

    qt = q_ref[...]
    blk_row = lax.broadcasted_iota(jnp.int32, (nblk, tq), 0)
    past = blk_row < qi
    qs, slopes = [], []
    for e in range(2):
        slope = jnp.exp2(-8.0 * (2 * pair + e + 1).astype(F32) / n_heads)
        qe = qt[:, e * HEAD_DIM:(e + 1) * HEAD_DIM]
        km_hi, km_lo = _split_bf16(kmean_ref[:, e * HEAD_DIM:(e + 1) * HEAD_DIM])
        gate = _dot_nt(km_hi, qe) + _dot_nt(km_lo, qe)
        sel = _topk_member(jnp.where(past, gate, -2e38), topk)
        selb_ref[e] = jnp.where(jnp.logical_and(sel, past), 0.0, NEG)
        qs.append(jnp.concatenate([qe, _aug_const(tq, [slope]).astype(BF16)], axis=1))
        slopes.append(slope)

    rows = lambda n: pl.ds(pl.multiple_of(n * tk, tk), tk)
    causal = lambda n: (lax.broadcasted_iota(jnp.int32, (tk, tq), 0)
                        <= lax.broadcasted_iota(jnp.int32, (tk, tq), 1))
    streams = [
        _Stream(qs[e],
                functools.partial(lambda n, e: kaug_ref[e, rows(n), :], e=e),
                functools.partial(lambda n, e: vt_ref[e, :, rows(n)], e=e),
                acc_ref.at[e], t_ref.at[e], p_ref.at[e],
                functools.partial(lambda n, e: slopes[e] * ((n - qi) * tk).astype(F32), e=e),
                bias=functools.partial(lambda n, e: selb_ref[e, pl.ds(n, 1), :], e=e),
                mask_first=causal)
        for e in range(2)]
    stats = _flash_sweep(streams, qi, 0, qi)
    o_ref[...] = jnp.concatenate([acc_ref[e] / stats[e][1] for e in range(2)], axis=0).T


def _moba_attention(qkv, batch, seq, col_q, col_k, col_v, n_heads, blk, topk):
    kern = functools.partial(_moba_kernel, blk=blk, topk=topk, n_heads=n_heads)
    nblk = seq // blk
    return pl.pallas_call(
        kern,
        grid=(batch, n_heads // 2, nblk),
        in_specs=[
            pl.BlockSpec((None, blk, LANES), lambda b, p, i: (b, i, col_q + p)),
            pl.BlockSpec((None, seq, LANES), lambda b, p, i: (b, 0, col_k + p)),
            pl.BlockSpec((None, seq, LANES), lambda b, p, i: (b, 0, col_v + p)),
        ],
        out_specs=pl.BlockSpec((None, blk, LANES), lambda b, p, i: (b, i, p)),
        out_shape=jax.ShapeDtypeStruct((batch, seq, n_heads * HEAD_DIM), F32),
        scratch_shapes=[
            pltpu.VMEM((2, seq, LANES), BF16),
            pltpu.VMEM((2, HEAD_DIM, seq), BF16),
            pltpu.VMEM((nblk, LANES), F32),
            pltpu.VMEM((2, nblk, blk), F32),
            pltpu.VMEM((2, HEAD_DIM, blk), F32),
            pltpu.VMEM((2, FLASH_UNROLL, blk, blk), F32), pltpu.VMEM((2, blk, blk), BF16),
        ],
        compiler_params=pltpu.CompilerParams(
            dimension_semantics=("parallel", "parallel", "arbitrary"),
            vmem_limit_bytes=VMEM_LIMIT),
        name="moba_attn",
    )(qkv, qkv, qkv)


CMP_STRIDE = 16
CMP_LEN = 32
SLC_BLK = 64
SLC_TOPK = 16
NSA_WINDOW = 512


def _compress_kernel(c_ref, pe_ref, w1_ref, w2_ref, g_ref, o_ref, *, is_key):
    half = c_ref.shape[1]
    n = c_ref.shape[0]
    c = c_ref[...]
    w1 = w1_ref[...]
    a_lo = _dot(c, w1[:half])
    a_hi = _dot(c, w1[half:])
    pe_hi, pe_lo = _split_bf16(pe_ref[...])
    c0 = (_dot(pe_hi, w1) + _dot(pe_lo, w1))[0:1]
    y = a_lo + pltpu.roll(a_hi, n - 1, axis=0) + c0
    hid = (y * jax.nn.sigmoid(y)).astype(BF16)
    if is_key:
        z = _dot(hid, w2_ref[...])
    else:
        z = _dot(hid.astype(F32), w2_ref[...].astype(F32))
    if is_key:
        ms = jnp.sum(z * z, axis=-1, keepdims=True) * (1.0 / HEAD_DIM)
        kn = z * lax.rsqrt(ms + RMS_EPS) * g_ref[...]
        i = lax.broadcasted_iota(jnp.int32, (n, 1), 0)
        aug = _aug_const(n, [(i // 16 * 256).astype(F32), (i % 16 * 16).astype(F32), 1.0])
        o_ref[...] = jnp.concatenate([kn[:, :HEAD_DIM], aug], axis=1).astype(BF16)
    else:
        o_ref[...] = z.T[:HEAD_DIM, :].astype(BF16)


def _compress(chunks, pe, w1, w2, gain, is_key):
    B, Hkv, n, half = chunks.shape
    hidden = w1.shape[1]
    pe8 = jnp.broadcast_to(pe.reshape(1, 2 * half).astype(F32), (8, 2 * half))
    w2p = jnp.pad(w2, ((0, 0), (0, LANES - w2.shape[1]))).astype(BF16)
    g = jnp.pad(gain.astype(F32), (0, LANES - HEAD_DIM)).reshape(1, LANES)
    if is_key:
        out_shape = jax.ShapeDtypeStruct((B, Hkv, n, LANES), BF16)
        out_spec = pl.BlockSpec((None, None, n, LANES), lambda b, h: (b, h, 0, 0))
    else:
        out_shape = jax.ShapeDtypeStruct((B, Hkv, HEAD_DIM, n), BF16)
        out_spec = pl.BlockSpec((None, None, HEAD_DIM, n), lambda b, h: (b, h, 0, 0))
    return pl.pallas_call(
        functools.partial(_compress_kernel, is_key=is_key),
        grid=(B, Hkv),
        in_specs=[
            pl.BlockSpec((None, None, n, half), lambda b, h: (b, h, 0, 0)),
            pl.BlockSpec((8, 2 * half), lambda b, h: (0, 0)),
            pl.BlockSpec((2 * half, hidden), lambda b, h: (0, 0)),
            pl.BlockSpec((hidden, LANES), lambda b, h: (0, 0)),
            pl.BlockSpec((1, LANES), lambda b, h: (0, 0)),
        ],
        out_specs=out_spec,
        out_shape=out_shape,
        compiler_params=pltpu.CompilerParams(
            dimension_semantics=("parallel", "parallel"), vmem_limit_bytes=VMEM_LIMIT),
        name="nsa_compress_k" if is_key else "nsa_compress_v",
    )(chunks, pe8, w1.astype(BF16), w2p, g)


def _nsa_cmp_kernel(q_ref, kc_ref, vct_ref, ct_ref, o_ref, selb_ref, *, tq, group, n_heads):
    g = pl.program_id(1)
    qi = pl.program_id(2)
    n_cmp = kc_ref.shape[0]
    n_slc = ct_ref.shape[0]
    i0 = qi * tq
    qpos = i0 + lax.broadcasted_iota(jnp.int32, (1, tq), 1)
    cend = lax.broadcasted_iota(jnp.int32, (n_cmp, 1), 0) * CMP_STRIDE + (CMP_LEN - 1)
    valid = cend <= qpos
    qt = q_ref[...]
    kc = kc_ref[...]
    vct = vct_ref[...]
    psum = jnp.zeros((n_cmp, tq), F32)
    outs = []
    for e in range(group):
        slope = jnp.exp2(-8.0 * (group * g + e + 1).astype(F32) / n_heads)
        aug = _aug_const(tq, [slope, slope, -slope * i0.astype(F32)]).astype(BF16)
        qa = jnp.concatenate([qt[:, e * HEAD_DIM:(e + 1) * HEAD_DIM], aug], axis=1)
        t = jnp.where(valid, _dot_nt(kc, qa), NEG)
        m = jnp.max(t, axis=0, keepdims=True)
        p = jnp.where(valid, jnp.exp(t - m), 0.0)
        l = jnp.sum(p, axis=0, keepdims=True)
        p = p / jnp.where(l > 0, l, 1.0)
        outs.append(_dot(vct, p.astype(BF16)))
        psum = psum + p
    o_ref[...] = jnp.concatenate(outs, axis=0).T
    p_hi, p_lo = _split_bf16(psum)
    ct = ct_ref[...]
    imp = _dot(ct, p_hi) + _dot(ct, p_lo)
    jb = lax.broadcasted_iota(jnp.int32, (n_slc, tq), 0)
    qb = qpos // SLC_BLK
    forced = (jb == 0) | (jb == qb) | (jb == qb - 1)
    cur = jnp.where(forced, 3e38, jnp.where(jb > qb, -2e38, imp))
    sel = jnp.logical_and(_topk_member(cur, min(SLC_TOPK, n_slc)), jb <= qb)
    selb_ref[...] = jnp.where(sel, 0.0, NEG).T.astype(BF16)


def _nsa_cmp(act, kc, vct, batch, seq, col_q, n_heads, group, tq=256):
    Hkv = n_heads // group
    n_cmp = kc.shape[2]
    n_slc = seq // SLC_BLK
    cs = jnp.arange(n_cmp) * CMP_STRIDE
    ss = jnp.arange(n_slc) * SLC_BLK
    overlap = (cs[None, :] <= ss[:, None] + SLC_BLK - 1) & (cs[None, :] + CMP_LEN - 1 >= ss[:, None])
    overlap = overlap & (cs[None, :] + CMP_LEN <= seq)
    ct = overlap.astype(BF16)
    gw = group * HEAD_DIM
    return pl.pallas_call(
        functools.partial(_nsa_cmp_kernel, tq=tq, group=group, n_heads=n_heads),
        grid=(batch, Hkv, seq // tq),
        in_specs=[
            pl.BlockSpec((None, tq, gw), lambda b, g, i: (b, i, col_q + g)),
            pl.BlockSpec((None, None, n_cmp, LANES), lambda b, g, i: (b, g, 0, 0)),
            pl.BlockSpec((None, None, HEAD_DIM, n_cmp), lambda b, g, i: (b, g, 0, 0)),
            pl.BlockSpec((n_slc, n_cmp), lambda b, g, i: (0, 0)),
        ],
        out_specs=[
            pl.BlockSpec((None, tq, gw), lambda b, g, i: (b, i, g)),
            pl.BlockSpec((None, None, tq, n_slc), lambda b, g, i: (b, g, i, 0)),
        ],
        out_shape=[
            jax.ShapeDtypeStruct((batch, seq, n_heads * HEAD_DIM), F32),
            jax.ShapeDtypeStruct((batch, Hkv, seq, n_slc), BF16),
        ],
        compiler_params=pltpu.CompilerParams(
            dimension_semantics=("parallel", "parallel", "parallel"),
            vmem_limit_bytes=VMEM_LIMIT),
        name="nsa_cmp",
    )(act, kc, vct, ct)


def _nsa_main_kernel(q_ref, selb_ref, kvs_ref, kvw_ref, kconst_ref, ocmp_ref, gate_ref, o_ref,
                     ks_ref, vst_ref, kw_ref, vwt_ref, accs_ref, accw_ref, gt_ref, t_ref, p_ref,
                     *, tq, tk, group, n_heads):
    g = pl.program_id(1)
    qi = pl.program_id(2)
    S = kvs_ref.shape[0]
    R = group * tq
    i0 = qi * tq

    @pl.when(qi == 0)
    def _():
        def body(c, carry):
            rows = pl.ds(pl.multiple_of(c * tk, tk), tk)
            kvs = kvs_ref[rows, :]
            kvw = kvw_ref[rows, :]
            kconst = kconst_ref[...]
            ks_ref[rows, :HEAD_DIM] = kvs[:, :HEAD_DIM]
            ks_ref[rows, HEAD_DIM:LANES] = kconst[:, :HEAD_DIM]
            kw_ref[rows, :] = jnp.concatenate([kvw[:, :HEAD_DIM], kconst[:, :HEAD_DIM]], axis=1)
            vst_ref[:, rows] = kvs.astype(F32).T[HEAD_DIM:, :].astype(BF16)
            vwt_ref[:, rows] = kvw.astype(F32).T[HEAD_DIM:, :].astype(BF16)
            key_blk = (c * tk + lax.broadcasted_iota(jnp.int32, (tk, n_slc), 0)) // SLC_BLK
            onehot = key_blk == lax.broadcasted_iota(jnp.int32, (tk, n_slc), 1)
            ks_ref[rows, LANES:] = jnp.where(onehot, 1.0, 0.0).astype(BF16)
            return carry

        n_slc = ks_ref.shape[1] - LANES
        lax.fori_loop(0, S // tk, body, 0)

    qt = q_ref[...]
    selb = selb_ref[...]
    qs_parts, qw_parts, slope_parts = [], [], []
    for e in range(group):
        slope = jnp.exp2(-8.0 * (group * g + e + 1).astype(F32) / n_heads)
        aug = _aug_const(tq, [slope]).astype(BF16)
        qe = qt[:, e * HEAD_DIM:(e + 1) * HEAD_DIM]
        qw_parts.append(jnp.concatenate([qe, aug], axis=1))
        qs_parts.append(jnp.concatenate([qe, aug, selb], axis=1))
        slope_parts.append(jnp.full((1, tq), slope, F32))
    q_slc = jnp.concatenate(qs_parts, axis=0)
    q_win = jnp.concatenate(qw_parts, axis=0)
    slope_row = jnp.concatenate(slope_parts, axis=1)
    qpos = i0 + lax.broadcasted_iota(jnp.int32, (1, R), 1) % tq

    nd = i0 // tk
    rows = lambda n: pl.ds(pl.multiple_of(n * tk, tk), tk)
    shift = lambda n: slope_row * (n * tk - i0).astype(F32)
    kpos = lambda n: n * tk + lax.broadcasted_iota(jnp.int32, (tk, 1), 0)
    causal = lambda n: kpos(n) <= qpos

    def in_window(n):
        dist = qpos - kpos(n)
        return (dist >= 0) & (dist < NSA_WINDOW)

    slc = _Stream(q_slc, lambda n: ks_ref[rows(n), :], lambda n: vst_ref[:, rows(n)],
                  accs_ref, t_ref, p_ref, shift, mask_first=causal)
    ((_, ls),) = _flash_sweep([slc], nd, 0, nd)
    win = _Stream(q_win, lambda n: kw_ref[rows(n), :], lambda n: vwt_ref[:, rows(n)],
                  accw_ref, t_ref, p_ref, shift, mask_first=in_window, mask_loop=in_window)
    lo = jnp.maximum(i0 - (NSA_WINDOW - 1), 0) // tk
    ((_, lw),) = _flash_sweep([win], nd, lo, nd)

    gz = gate_ref[...]
    gt_ref[...] = jax.nn.sigmoid(gz).T
    oc_t = ocmp_ref[...].T
    outs = []
    for e in range(group):
        head = group * g + e
        cols = slice(e * tq, (e + 1) * tq)
        o_s = accs_ref[:, cols] / ls[:, cols]
        o_w = accw_ref[:, cols] / lw[:, cols]
        o_c = oc_t[e * HEAD_DIM:(e + 1) * HEAD_DIM, :]
        g_c = gt_ref[pl.ds(3 * head, 1), :]
        g_s = gt_ref[pl.ds(3 * head + 1, 1), :]
        g_w = gt_ref[pl.ds(3 * head + 2, 1), :]
        outs.append(g_c * o_c + g_s * o_s + g_w * o_w)
    o_ref[...] = jnp.concatenate(outs, axis=0).T


def _nsa_main(act, selb, ocmp, gates, batch, seq, col_q, col_kvs, col_kvw, col_gate, n_heads, group,
              tq=128, tk=256):
    Hkv = n_heads // group
    n_slc = seq // SLC_BLK
    gw = group * HEAD_DIM
    R = group * tq
    jj = jnp.arange(tk, dtype=F32)
    kconst = jnp.zeros((tk, HEAD_DIM), F32).at[:, 0].set(jj).astype(BF16)
    return pl.pallas_call(
        functools.partial(_nsa_main_kernel, tq=tq, tk=tk, group=group, n_heads=n_heads),
        grid=(batch, Hkv, seq // tq),
        in_specs=[
            pl.BlockSpec((None, tq, gw), lambda b, g, i: (b, i, col_q + g)),
            pl.BlockSpec((None, None, tq, n_slc), lambda b, g, i: (b, g, i, 0)),
            pl.BlockSpec((None, seq, LANES), lambda b, g, i: (b, 0, col_kvs + g)),
            pl.BlockSpec((None, seq, LANES), lambda b, g, i: (b, 0, col_kvw + g)),
            pl.BlockSpec((tk, HEAD_DIM), lambda b, g, i: (0, 0)),
            pl.BlockSpec((None, tq, gw), lambda b, g, i: (b, i, g)),
            pl.BlockSpec((None, tq, LANES), lambda b, g, i: (b, i, col_gate)),
        ],
        out_specs=pl.BlockSpec((None, tq, gw), lambda b, g, i: (b, i, g)),
        out_shape=jax.ShapeDtypeStruct((batch, seq, n_heads * HEAD_DIM), F32),
        scratch_shapes=[
            pltpu.VMEM((seq, LANES + n_slc), BF16),
            pltpu.VMEM((HEAD_DIM, seq), BF16),
            pltpu.VMEM((seq, LANES), BF16),
            pltpu.VMEM((HEAD_DIM, seq), BF16),
            pltpu.VMEM((HEAD_DIM, R), F32),
            pltpu.VMEM((HEAD_DIM, R), F32),
            pltpu.VMEM((LANES, tq), F32),
            pltpu.VMEM((FLASH_UNROLL, tk, R), F32), pltpu.VMEM((tk, R), BF16),
        ],
        compiler_params=pltpu.CompilerParams(
            dimension_semantics=("parallel", "parallel", "arbitrary"),
            vmem_limit_bytes=VMEM_LIMIT),
        name="nsa_main",
    )(act, selb, act, act, kconst, ocmp, gates)


def _tile_gain(g, width):
    return jnp.tile(g.astype(F32), width // HEAD_DIM)


def _even_layer(x, ln, w_in, qkn, lam, subln, w_out, layer):
    B, S, D = x.shape
    M = B * S
    x2d = x.reshape(M, D)
    W = 512
    aq, ak, av, az, bq, bk, bv, bz = [w_in[:, i * W:(i + 1) * W] for i in range(8)]
    scale = HEAD_DIM ** -0.5
    ones, zeros = jnp.ones((W,), F32), jnp.zeros((W,), F32)
    w_b = jnp.concatenate([bq, bk, bv], axis=1).astype(BF16)
    gain_b = jnp.concatenate([_tile_gain(qkn[2], W) * scale, _tile_gain(qkn[3], W), ones])
    mask_b = jnp.concatenate([ones, ones, zeros])
    act_b = _project(x2d, ln, w_b, gain_b, mask_b, BF16)
    w_a = jnp.concatenate([aq, ak, av, az, bz], axis=1).astype(BF16)
    gain_a = jnp.concatenate([_tile_gain(qkn[0], W) * scale, _tile_gain(qkn[1], W), ones, ones, ones])
    mask_a = jnp.concatenate([ones, ones, zeros, zeros, zeros])
    act_a = _project(x2d, ln, w_a, gain_a, mask_a, F32)

    lam_init = 0.8 - 0.6 * math.exp(-0.3 * layer)
    lf = lam.astype(F32)
    lam_f = jnp.exp(jnp.sum(lf[0] * lf[1])) - jnp.exp(jnp.sum(lf[2] * lf[3])) + lam_init
    ob = _diff_attention(act_b.reshape(B, S, 3 * W), lam_f, subln, B, S, 0, 4, 8, 4, 1.0 - lam_init)
    oa = _dilated_attention(act_a.reshape(B, S, 5 * W), B, S, 0, 4, 8, 8)
    out = _out_project(x2d, oa.reshape(M, W), ob.reshape(M, W), act_a, 3, 4, w_out)
    return out.reshape(B, S, D)


def _odd_layer(x, ln, w_in, qkn, phi_pe, phi_w1, phi_w2, w_out):
    B, S, D = x.shape
    M = B * S
    x2d = x.reshape(M, D)
    W, KW = 512, 128
    offs = [0]
    for s in (W, W, W, W, W, KW, KW, KW, KW, KW, KW, W, 24):
        offs.append(offs[-1] + s)
    cq, ck, cv, cz, dq, dkc, dvc, dks, dvs, dkw, dvw, dz, dg = [
        w_in[:, offs[i]:offs[i + 1]] for i in range(13)]
    scale = HEAD_DIM ** -0.5
    hd = HEAD_DIM
    ones = lambda n: jnp.ones((n,), F32)
    zeros = lambda n: jnp.zeros((n,), F32)
    kv_pair = lambda k, v: jnp.concatenate([k[:, :hd], v[:, :hd], k[:, hd:], v[:, hd:]], axis=1)
    w_b = jnp.concatenate([cq, ck, cv, dq, kv_pair(dks, dvs), kv_pair(dkw, dvw), dkc, dvc,
                           jnp.zeros((D, 2 * KW), w_in.dtype)], axis=1).astype(BF16)
    kv_gain = lambda g: jnp.concatenate([g, ones(hd), g, ones(hd)])
    kv_mask = jnp.concatenate([ones(hd), zeros(hd), ones(hd), zeros(hd)])
    gain_b = jnp.concatenate([_tile_gain(qkn[0], W) * scale, _tile_gain(qkn[1], W), ones(W),
                              _tile_gain(qkn[2], W) * scale, kv_gain(qkn[4].astype(F32)),
                              kv_gain(qkn[5].astype(F32)), ones(4 * KW)])
    mask_b = jnp.concatenate([ones(W), ones(W), zeros(W), ones(W), kv_mask, kv_mask, zeros(4 * KW)])
    act_b = _project(x2d, ln, w_b, gain_b, mask_b, BF16).reshape(B, S, 6 * W)
    w_g = jnp.concatenate([cz, dz, dg, jnp.zeros((D, W - 24), w_in.dtype)], axis=1).astype(BF16)
    act_g = _project(x2d, ln, w_g, ones(3 * W), zeros(3 * W), F32)

    oc = _moba_attention(act_b, B, S, 0, 4, 8, 8, 256, 3)

    def chunks(t):
        t = t.reshape(B, S // CMP_STRIDE, CMP_STRIDE, 2, hd)
        return jnp.transpose(t, (0, 3, 1, 2, 4)).reshape(B, 2, S // CMP_STRIDE, CMP_STRIDE * hd)

    kc = _compress(chunks(act_b[:, :, 20 * KW:21 * KW]), phi_pe[0], phi_w1[0], phi_w2[0], qkn[3], True)
    vct = _compress(chunks(act_b[:, :, 21 * KW:22 * KW]), phi_pe[1], phi_w1[1], phi_w2[1], qkn[3], False)
    ocmp, selb = _nsa_cmp(act_b, kc, vct, B, S, 6, 8, 4)
    od = _nsa_main(act_b, selb, ocmp, act_g.reshape(B, S, 3 * W), B, S, 6, 16, 18, 8, 8, 4)
    out = _out_project(x2d, oc.reshape(M, W), od.reshape(M, W), act_g, 0, 1, w_out)
    return out.reshape(B, S, D)


def kernel(x, ln_e, w_in_e, qkn_e, lam_e, subln_e, w_out_e, ln_o, w_in_o, qkn_o, phi_pe, phi_w1, phi_w2, w_out_o):
    n_layers = ln_e.shape[0] + ln_o.shape[0]
    for layer in range(n_layers):
        i = layer // 2
        if layer % 2 == 0:
            x = _even_layer(x, ln_e[i], w_in_e[i], qkn_e[i], lam_e[i], subln_e[i], w_out_e[i], layer)
        else:
            x = _odd_layer(x, ln_o[i], w_in_o[i], qkn_o[i], phi_pe[i], phi_w1[i], phi_w2[i], w_out_o[i])
    return x
```

```python
import functools
import math

import jax
import jax.numpy as jnp
import numpy as np
from jax import lax
from jax.experimental import pallas as pl
from jax.experimental.pallas import tpu as pltpu

HEAD_DIM = 64
LANES = 128
RMS_EPS = 1e-6
NEG = -2e30
M_INIT = -1e30
VMEM_LIMIT = 56 * 1024 * 1024

F32 = jnp.float32
BF16 = jnp.bfloat16

_NT = (((1,), (1,)), ((), ()))


def _dot(a, b):
    return jnp.dot(a, b, preferred_element_type=F32)


def _dot_nt(a, b):
    return lax.dot_general(a, b, _NT, preferred_element_type=F32)


def _alibi_slopes(n):
    return [2.0 ** (-8.0 * (i + 1) / n) for i in range(n)]


def _split_bf16(x):
    hi = x.astype(BF16)
    lo = (x - hi.astype(F32)).astype(BF16)
    return hi, lo


NORM_CHUNK = 256


def _proj_kernel(x_ref, ln_ref, w_ref, gain_ref, nmask_ref, bd_ref, o_ref, *, norm_chunks):
    x = x_ref[...]
    ms = jnp.mean(x * x, axis=-1, keepdims=True)
    h = (x * lax.rsqrt(ms + RMS_EPS) * ln_ref[...]).astype(BF16)
    bd = bd_ref[...]
    for c, normed in enumerate(norm_chunks):
        cols = slice(c * NORM_CHUNK, (c + 1) * NORM_CHUNK)
        y = _dot(h, w_ref[:, cols])
        if normed:
            hi, lo = _split_bf16(y * y)
            ms = _dot(hi, bd) + _dot(lo, bd)
            yn = y * lax.rsqrt(ms + RMS_EPS) * gain_ref[:, cols]
            y = jnp.where(nmask_ref[:, cols] > 0.5, yn, y)
        o_ref[:, cols] = y.astype(o_ref.dtype)


def _project(x2d, ln, w, gain, nmask, out_dtype, tm=512):
    M, D = x2d.shape
    N = w.shape[1]
    assert M % tm == 0 and N % NORM_CHUNK == 0
    norm_chunks = [bool(c) for c in np.asarray(nmask).reshape(-1, NORM_CHUNK).max(axis=1) > 0.5]
    nmask = jnp.asarray(nmask, F32)
    r = jnp.arange(NORM_CHUNK) // HEAD_DIM
    bd = jnp.where(r[:, None] == r[None, :], 1.0 / HEAD_DIM, 0.0).astype(BF16)
    return pl.pallas_call(
        functools.partial(_proj_kernel, norm_chunks=tuple(norm_chunks)),
        grid=(M // tm,),
        in_specs=[
            pl.BlockSpec((tm, D), lambda i: (i, 0)),
            pl.BlockSpec((1, D), lambda i: (0, 0)),
            pl.BlockSpec((D, N), lambda i: (0, 0)),
            pl.BlockSpec((1, N), lambda i: (0, 0)),
            pl.BlockSpec((1, N), lambda i: (0, 0)),
            pl.BlockSpec((NORM_CHUNK, NORM_CHUNK), lambda i: (0, 0)),
        ],
        out_specs=pl.BlockSpec((tm, N), lambda i: (i, 0)),
        out_shape=jax.ShapeDtypeStruct((M, N), out_dtype),
        compiler_params=pltpu.CompilerParams(
            dimension_semantics=("parallel",), vmem_limit_bytes=VMEM_LIMIT),
        name="proj",
    )(x2d, ln.reshape(1, D).astype(F32), w, gain.reshape(1, N).astype(F32),
      nmask.reshape(1, N).astype(F32), bd)


def _outproj_kernel(x_ref, oa_ref, ob_ref, za_ref, zb_ref, wa_ref, wb_ref, o_ref):
    za = za_ref[...]
    zb = zb_ref[...]
    ma = (oa_ref[...] * (za * jax.nn.sigmoid(za))).astype(BF16)
    mb = (ob_ref[...] * (zb * jax.nn.sigmoid(zb))).astype(BF16)
    o_ref[...] = x_ref[...] + _dot(ma, wa_ref[...]) + _dot(mb, wb_ref[...])


def _out_project(x2d, oa, ob, z, za_blk, zb_blk, w_out, tm=512):
    M, D = x2d.shape
    W = oa.shape[1]
    wa = w_out[:W].astype(BF16)
    wb = w_out[W:].astype(BF16)
    return pl.pallas_call(
        _outproj_kernel,
        grid=(M // tm,),
        in_specs=[
            pl.BlockSpec((tm, D), lambda i: (i, 0)),
            pl.BlockSpec((tm, W), lambda i: (i, 0)),
            pl.BlockSpec((tm, W), lambda i: (i, 0)),
            pl.BlockSpec((tm, W), lambda i: (i, za_blk)),
            pl.BlockSpec((tm, W), lambda i: (i, zb_blk)),
            pl.BlockSpec((W, D), lambda i: (0, 0)),
            pl.BlockSpec((W, D), lambda i: (0, 0)),
        ],
        out_specs=pl.BlockSpec((tm, D), lambda i: (i, 0)),
        out_shape=jax.ShapeDtypeStruct((M, D), F32),
        compiler_params=pltpu.CompilerParams(
            dimension_semantics=("parallel",), vmem_limit_bytes=VMEM_LIMIT),
        name="outproj",
    )(x2d, oa, ob, z, z, wa, wb)


class _Stream:
    def __init__(self, q, k_tile, vt_tile, acc_ref, t_ref, p_ref, shift, bias=None, mask=None,
                 mask_every_tile=False):
        self.q, self.k_tile, self.vt_tile = q, k_tile, vt_tile
        self.acc_ref, self.t_ref, self.p_ref = acc_ref, t_ref, p_ref
        self.shift, self.bias, self.mask = shift, bias, mask
        self.mask_every_tile = mask_every_tile

    def scores(self, n):
        return _dot_nt(self.k_tile(n), self.q)

    def softmax(self, t, n, m, l, pad, masked):
        if masked:
            t = jnp.where(self.mask(n), t, NEG)
        off = self.shift(n) + pad
        if self.bias is not None:
            off = off + self.bias(n)
        m_new = jnp.maximum(m, jnp.max(t, axis=0, keepdims=True) + off)
        p = jnp.exp(t - (m_new - off))
        alpha = jnp.exp(m - m_new)
        return m_new, alpha * l + jnp.sum(p, axis=0, keepdims=True), alpha, p.astype(BF16)


FLASH_UNROLL = 2


def _flash_sweep(streams, count, tile_of):
    U = FLASH_UNROLL
    R = streams[0].q.shape[0]
    last = count - 1
    state = []
    for s in streams:
        s.acc_ref[...] = jnp.zeros_like(s.acc_ref)
        s.p_ref[...] = jnp.zeros_like(s.p_ref)
        for k in range(U):
            s.t_ref[k] = s.scores(tile_of(jnp.minimum(k, last)))
        state += [jnp.full((1, R), M_INIT, F32), jnp.zeros((1, R), F32), jnp.ones((1, R), F32)]

    def body(g, carry, first_body):
        n_prev, state = carry[0], list(carry[1:])
        base = U * g
        pvs = [_dot(s.vt_tile(n_prev), s.p_ref[...]) for s in streams]
        nxt = [[s.scores(tile_of(jnp.minimum(base + U + k, last))) for k in range(U)]
               for s in streams]
        for i, s in enumerate(streams):
            s.acc_ref[...] = state[3 * i + 2] * s.acc_ref[...] + pvs[i]
        for k in range(U):
            n = tile_of(jnp.minimum(base + k, last))
            pad = jnp.where(base + k < count, 0.0, NEG)
            for i, s in enumerate(streams):
                m, l, _ = state[3 * i:3 * i + 3]
                masked = s.mask is not None and (s.mask_every_tile or (first_body and k == 0))
                m, l, a, p = s.softmax(s.t_ref[k], n, m, l, pad, masked)
                state[3 * i:3 * i + 3] = [m, l, a]
                if k < U - 1:
                    s.acc_ref[...] = a * s.acc_ref[...] + _dot(s.vt_tile(n), p)
                else:
                    s.p_ref[...] = p
            n_prev = n
        for i, s in enumerate(streams):
            for k in range(U):
                s.t_ref[k] = nxt[i][k]
        return (n_prev,) + tuple(state)

    carry = body(0, (tile_of(0),) + tuple(state), True)
    carry = lax.fori_loop(1, (count + U - 1) // U, lambda g, c: body(g, c, False), carry)
    n_prev, state = carry[0], carry[1:]
    res = []
    for i, s in enumerate(streams):
        m, l, a_prev = state[3 * i:3 * i + 3]
        s.acc_ref[...] = a_prev * s.acc_ref[...] + _dot(s.vt_tile(n_prev), s.p_ref[...])
        res.append((m, l))
    return res


def _aug_const(rows, col_vals):
    lane = lax.broadcasted_iota(jnp.int32, (rows, HEAD_DIM), 1)
    out = jnp.zeros((rows, HEAD_DIM), F32)
    for c, v in enumerate(col_vals):
        out = jnp.where(lane == c, v, out)
    return out


def _diff_kernel(lam_ref, q_ref, k_ref, v_ref, subln_ref, o_ref,
                 k1_ref, k2_ref, vt_ref, acc1_ref, acc2_ref, t_ref, p_ref, *, tq, n_heads, out_scale):
    h = pl.program_id(1)
    qi = pl.program_id(2)
    S = k_ref.shape[0]
    tk = tq
    slope = jnp.exp2(-8.0 * (h + 1).astype(F32) / n_heads)

    @pl.when(qi == 0)
    def _():
        jj = lax.broadcasted_iota(jnp.int32, (tk, 1), 0).astype(F32)
        kaug = _aug_const(tk, [jj]).astype(BF16)

        def body(c, carry):
            rows = pl.ds(pl.multiple_of(c * tk, tk), tk)
            kc = k_ref[rows, :]
            k1_ref[rows, :] = jnp.concatenate([kc[:, :HEAD_DIM], kaug], axis=1)
            k2_ref[rows, :] = jnp.concatenate([kc[:, HEAD_DIM:], kaug], axis=1)
            vt_ref[:, rows] = v_ref[rows, :].astype(F32).T.astype(BF16)
            return carry

        lax.fori_loop(0, S // tk, body, 0)

    qt = q_ref[...]
    qaug = _aug_const(tq, [slope]).astype(BF16)
    q1 = jnp.concatenate([qt[:, :HEAD_DIM], qaug], axis=1)
    q2 = jnp.concatenate([qt[:, HEAD_DIM:], qaug], axis=1)

    rows = lambda n: pl.ds(pl.multiple_of(n * tk, tk), tk)
    qpos = qi * tq + lax.broadcasted_iota(jnp.int32, (1, tq), 1)
    causal = lambda n: n * tk + lax.broadcasted_iota(jnp.int32, (tk, 1), 0) <= qpos
    shift = lambda n: slope * ((n - qi) * tk).astype(F32)
    vt_tile = lambda n: vt_ref[:, rows(n)]
    streams = [
        _Stream(q1, lambda n: k1_ref[rows(n), :], vt_tile, acc1_ref, t_ref.at[0], p_ref.at[0],
                shift, mask=causal),
        _Stream(q2, lambda n: k2_ref[rows(n), :], vt_tile, acc2_ref, t_ref.at[1], p_ref.at[1],
                shift, mask=causal),
    ]
    (m1, l1), (m2, l2) = _flash_sweep(streams, qi + 1, lambda pos: jnp.where(pos == 0, qi, pos - 1))

    d = acc1_ref[...] / l1 - lam_ref[0] * (acc2_ref[...] / l2)
    r = lax.rsqrt(jnp.mean(d * d, axis=0, keepdims=True) + RMS_EPS)
    o_ref[...] = (d * r).T * (subln_ref[...] * out_scale)


def _diff_attention(qkv, lam_f, subln, batch, seq, col_q, col_k, col_v, n_heads, out_scale, tq=256):
    dv = 2 * HEAD_DIM
    kern = functools.partial(_diff_kernel, tq=tq, n_heads=n_heads, out_scale=out_scale)
    grid_spec = pltpu.PrefetchScalarGridSpec(
        num_scalar_prefetch=1,
        grid=(batch, n_heads, seq // tq),
        in_specs=[
            pl.BlockSpec((None, tq, dv), lambda b, h, i, s: (b, i, col_q + h)),
            pl.BlockSpec((None, seq, dv), lambda b, h, i, s: (b, 0, col_k + h)),
            pl.BlockSpec((None, seq, dv), lambda b, h, i, s: (b, 0, col_v + h)),
            pl.BlockSpec((1, dv), lambda b, h, i, s: (0, 0)),
        ],
        out_specs=pl.BlockSpec((None, tq, dv), lambda b, h, i, s: (b, i, h)),
        scratch_shapes=[
            pltpu.VMEM((seq, LANES), BF16), pltpu.VMEM((seq, LANES), BF16),
            pltpu.VMEM((dv, seq), BF16),
            pltpu.VMEM((dv, tq), F32), pltpu.VMEM((dv, tq), F32),
            pltpu.VMEM((2, FLASH_UNROLL, tq, tq), F32), pltpu.VMEM((2, tq, tq), BF16),
        ],
    )
    return pl.pallas_call(
        kern,
        grid_spec=grid_spec,
        out_shape=jax.ShapeDtypeStruct((batch, seq, n_heads * dv), F32),
        compiler_params=pltpu.CompilerParams(
            dimension_semantics=("parallel", "parallel", "arbitrary"),
            vmem_limit_bytes=VMEM_LIMIT),
        name="diff_attn",
    )(lam_f.reshape(1).astype(F32), qkv, qkv, qkv, subln.reshape(1, dv).astype(F32))


BAND = 128
DILATIONS = (1, 4, 16)


def _dilated_kernel(q_ref, kp_ref, kc_ref, vp_ref, vc_ref, o_ref, kbuf_ref, vbuf_ref, st_ref,
                    *, chunk, n_heads):
    pair = pl.program_id(1)
    ci = pl.program_id(2)
    kbuf_ref[:chunk, :] = kp_ref[...]
    kbuf_ref[chunk:, :] = kc_ref[...]
    vbuf_ref[:chunk, :] = vp_ref[...]
    vbuf_ref[chunk:, :] = vc_ref[...]

    qi = lax.broadcasted_iota(jnp.int32, (BAND, 2 * BAND), 0)
    kj = lax.broadcasted_iota(jnp.int32, (BAND, 2 * BAND), 1)
    dist = qi + BAND - kj
    in_band = (dist >= 0) & (dist <= BAND)
    dist_f = dist.astype(F32)
    slopes = [jnp.exp2(-8.0 * (2 * pair + e + 1).astype(F32) / n_heads) for e in range(2)]

    for pi, dil in enumerate(DILATIONS):
        span = dil * BAND
        n_blk = chunk // span

        def block(idx, carry, dil=dil, span=span, n_blk=n_blk, pi=pi):
            r = idx // n_blk
            b = idx % n_blk
            q0 = r + b * span
            qb = q_ref[pl.ds(q0, BAND, stride=dil), :]
            kb = kbuf_ref[pl.ds(chunk + q0 - span, 2 * BAND, stride=dil), :]
            vb = vbuf_ref[pl.ds(chunk + q0 - span, 2 * BAND, stride=dil), :]
            first = jnp.logical_and(ci == 0, b == 0)
            ok = in_band & (kj >= jnp.where(first, BAND, 0))
            packed = []
            for e in range(2):
                cols = slice(e * HEAD_DIM, (e + 1) * HEAD_DIM)
                s = _dot_nt(qb[:, cols].astype(BF16), kb[:, cols].astype(BF16))
                s = jnp.where(ok, s - (slopes[e] * dil) * dist_f, NEG)
                m = jnp.max(s, axis=1, keepdims=True)
                p = jnp.exp(s - m)
                l = jnp.sum(p, axis=1, keepdims=True)
                o = _dot(p.astype(BF16), vb[:, cols].astype(BF16)) / l
                lse = jnp.broadcast_to(m + jnp.log(l), (BAND, HEAD_DIM))
                packed.append(jnp.concatenate([o, lse], axis=1))
            for e in range(2):
                st_ref[pi, e, pl.ds(q0, BAND, stride=dil), :] = packed[e]
            return carry

        lax.fori_loop(0, dil * n_blk, block, 0, unroll=4)

    rows_per = 256
    lane = lax.broadcasted_iota(jnp.int32, (rows_per, LANES), 1)

    def merge(c, carry):
        rows = pl.ds(pl.multiple_of(c * rows_per, rows_per), rows_per)
        outs = []
        for e in range(2):
            xs = [st_ref[pi, e, rows, :] for pi in range(len(DILATIONS))]
            top = functools.reduce(jnp.maximum, xs)
            num = jnp.zeros((rows_per, LANES), F32)
            den = jnp.zeros((rows_per, LANES), F32)
            for x in xs:
                w = pltpu.roll(jnp.exp(x - top), HEAD_DIM, axis=1)
                num = num + w * x
                den = den + w
            outs.append(num / den)
        o_ref[rows, :] = jnp.where(lane < HEAD_DIM, outs[0], pltpu.roll(outs[1], HEAD_DIM, axis=1))
        return carry

    lax.fori_loop(0, chunk // rows_per, merge, 0)


def _dilated_attention(act, batch, seq, col_q, col_k, col_v, n_heads, chunk=2048):
    assert chunk % (max(DILATIONS) * BAND) == 0 and seq % chunk == 0
    prev = lambda c: jnp.maximum(c - 1, 0)
    return pl.pallas_call(
        functools.partial(_dilated_kernel, chunk=chunk, n_heads=n_heads),
        grid=(batch, n_heads // 2, seq // chunk),
        in_specs=[
            pl.BlockSpec((None, chunk, LANES), lambda b, p, c: (b, c, col_q + p)),
            pl.BlockSpec((None, chunk, LANES), lambda b, p, c: (b, prev(c), col_k + p)),
            pl.BlockSpec((None, chunk, LANES), lambda b, p, c: (b, c, col_k + p)),
            pl.BlockSpec((None, chunk, LANES), lambda b, p, c: (b, prev(c), col_v + p)),
            pl.BlockSpec((None, chunk, LANES), lambda b, p, c: (b, c, col_v + p)),
        ],
        out_specs=pl.BlockSpec((None, chunk, LANES), lambda b, p, c: (b, c, p)),
        out_shape=jax.ShapeDtypeStruct((batch, seq, n_heads * HEAD_DIM), F32),
        scratch_shapes=[
            pltpu.VMEM((2 * chunk, LANES), F32),
            pltpu.VMEM((2 * chunk, LANES), F32),
            pltpu.VMEM((len(DILATIONS), 2, chunk, LANES), F32),
        ],
        compiler_params=pltpu.CompilerParams(
            dimension_semantics=("parallel", "parallel", "arbitrary"),
            vmem_limit_bytes=VMEM_LIMIT),
        name="dilated_attn",
    )(act, act, act, act, act)


TAKEN = -3e38


def _topk_member(cur, k):
    n = cur.shape[0]
    row = lax.broadcasted_iota(jnp.int32, cur.shape, 0)
    sel = jnp.zeros(cur.shape, jnp.bool_)
    for _ in range(k):
        mx = jnp.max(cur, axis=0, keepdims=True)
        first = jnp.min(jnp.where(cur == mx, row, n), axis=0, keepdims=True)
        pick = row == first
        sel = jnp.logical_or(sel, pick)
        cur = jnp.where(pick, TAKEN, cur)
    return sel


def _moba_kernel(q_ref, k_ref, v_ref, o_ref, kaug_ref, vt_ref, kmean_ref, selb_ref, acc_ref,
                 t_ref, p_ref, *, blk, topk, n_heads):
    pair = pl.program_id(1)
    qi = pl.program_id(2)
    S = k_ref.shape[0]
    nblk = S // blk
    tq = tk = blk

    @pl.when(qi == 0)
    def _():
        jj = lax.broadcasted_iota(jnp.int32, (tk, 1), 0).astype(F32)
        kaug = _aug_const(tk, [jj]).astype(BF16)

        def body(c, carry):
            rows = pl.ds(pl.multiple_of(c * tk, tk), tk)
            kc = k_ref[rows, :]
            vc = v_ref[rows, :].astype(F32).T.astype(BF16)
            for e in range(2):
                kaug_ref[e, rows, :] = jnp.concatenate(
                    [kc[:, e * HEAD_DIM:(e + 1) * HEAD_DIM], kaug], axis=1)
                vt_ref[e, :, rows] = vc[e * HEAD_DIM:(e + 1) * HEAD_DIM, :]
            kmean_ref[pl.ds(c, 1), :] = jnp.mean(kc.astype(F32), axis=0, keepdims=True)
            return carry

        lax.fori_loop(0, nblk, body, 0)

    qt = q_ref[...]
    blk_row = lax.broadcasted_iota(jnp.int32, (nblk, tq), 0)
    past = blk_row < qi
    qs, slopes = [], []
    for e in range(2):
        slope = jnp.exp2(-8.0 * (2 * pair + e + 1).astype(F32) / n_heads)
        qe = qt[:, e * HEAD_DIM:(e + 1) * HEAD_DIM]
        km_hi, km_lo = _split_bf16(kmean_ref[:, e * HEAD_DIM:(e + 1) * HEAD_DIM])
        gate = _dot_nt(km_hi, qe) + _dot_nt(km_lo, qe)
        sel = _topk_member(jnp.where(past, gate, -2e38), topk)
        selb_ref[e] = jnp.where(jnp.logical_and(sel, past) | (blk_row == qi), 0.0, NEG)
        qs.append(jnp.concatenate([qe, _aug_const(tq, [slope]).astype(BF16)], axis=1))
        slopes.append(slope)

    rows = lambda n: pl.ds(pl.multiple_of(n * tk, tk), tk)
    qpos = qi * tq + lax.broadcasted_iota(jnp.int32, (1, tq), 1)
    causal = lambda n: n * tk + lax.broadcasted_iota(jnp.int32, (tk, 1), 0) <= qpos
    streams = [
        _Stream(qs[e],
                functools.partial(lambda n, e: kaug_ref[e, rows(n), :], e=e),
                functools.partial(lambda n, e: vt_ref[e, :, rows(n)], e=e),
                acc_ref.at[e], t_ref.at[e], p_ref.at[e],
                functools.partial(lambda n, e: slopes[e] * ((n - qi) * tk).astype(F32), e=e),
                bias=functools.partial(lambda n, e: selb_ref[e, pl.ds(n, 1), :], e=e),
                mask=causal)
        for e in range(2)]
    stats = _flash_sweep(streams, qi + 1, lambda pos: jnp.where(pos == 0, qi, pos - 1))
    o_ref[...] = jnp.concatenate([acc_ref[e] / stats[e][1] for e in range(2)], axis=0).T


def _moba_attention(qkv, batch, seq, col_q, col_k, col_v, n_heads, blk, topk):
    kern = functools.partial(_moba_kernel, blk=blk, topk=topk, n_heads=n_heads)
    nblk = seq // blk
    return pl.pallas_call(
        kern,
        grid=(batch, n_heads // 2, nblk),
        in_specs=[
            pl.BlockSpec((None, blk, LANES), lambda b, p, i: (b, i, col_q + p)),
            pl.BlockSpec((None, seq, LANES), lambda b, p, i: (b, 0, col_k + p)),
            pl.BlockSpec((None, seq, LANES), lambda b, p, i: (b, 0, col_v + p)),
        ],
        out_specs=pl.BlockSpec((None, blk, LANES), lambda b, p, i: (b, i, p)),
        out_shape=jax.ShapeDtypeStruct((batch, seq, n_heads * HEAD_DIM), F32),
        scratch_shapes=[
            pltpu.VMEM((2, seq, LANES), BF16),
            pltpu.VMEM((2, HEAD_DIM, seq), BF16),
            pltpu.VMEM((nblk, LANES), F32),
            pltpu.VMEM((2, nblk, blk), F32),
            pltpu.VMEM((2, HEAD_DIM, blk), F32),
            pltpu.VMEM((2, FLASH_UNROLL, blk, blk), F32), pltpu.VMEM((2, blk, blk), BF16),
        ],
        compiler_params=pltpu.CompilerParams(
            dimension_semantics=("parallel", "parallel", "arbitrary"),
            vmem_limit_bytes=VMEM_LIMIT),
        name="moba_attn",
    )(qkv, qkv, qkv)


CMP_STRIDE = 16
CMP_LEN = 32
SLC_BLK = 64
SLC_TOPK = 16
NSA_WINDOW = 512
NSA_TQ = 128
NSA_TK = 256


def _compress_kernel(c_ref, pe_ref, w1_ref, w2_ref, g_ref, o_ref, *, is_key):
    half = c_ref.shape[1]
    n = c_ref.shape[0]
    c = c_ref[...]
    w1 = w1_ref[...]
    a_lo = _dot(c, w1[:half])
    a_hi = _dot(c, w1[half:])
    pe_hi, pe_lo = _split_bf16(pe_ref[...])
    c0 = (_dot(pe_hi, w1) + _dot(pe_lo, w1))[0:1]
    y = a_lo + pltpu.roll(a_hi, n - 1, axis=0) + c0
    hid = (y * jax.nn.sigmoid(y)).astype(BF16)
    if is_key:
        z = _dot(hid, w2_ref[...])
    else:
        z = _dot(hid.astype(F32), w2_ref[...].astype(F32))
    if is_key:
        ms = jnp.sum(z * z, axis=-1, keepdims=True) * (1.0 / HEAD_DIM)
        kn = z * lax.rsqrt(ms + RMS_EPS) * g_ref[...]
        i = lax.broadcasted_iota(jnp.int32, (n, 1), 0)
        aug = _aug_const(n, [(i // 16 * 256).astype(F32), (i % 16 * 16).astype(F32), 1.0])
        o_ref[...] = jnp.concatenate([kn[:, :HEAD_DIM], aug], axis=1).astype(BF16)
    else:
        o_ref[...] = z.T[:HEAD_DIM, :].astype(BF16)


def _compress(chunks, pe, w1, w2, gain, is_key):
    B, Hkv, n, half = chunks.shape
    hidden = w1.shape[1]
    pe8 = jnp.broadcast_to(pe.reshape(1, 2 * half).astype(F32), (8, 2 * half))
    w2p = jnp.pad(w2, ((0, 0), (0, LANES - w2.shape[1]))).astype(BF16)
    g = jnp.pad(gain.astype(F32), (0, LANES - HEAD_DIM)).reshape(1, LANES)
    if is_key:
        out_shape = jax.ShapeDtypeStruct((B, Hkv, n, LANES), BF16)
        out_spec = pl.BlockSpec((None, None, n, LANES), lambda b, h: (b, h, 0, 0))
    else:
        out_shape = jax.ShapeDtypeStruct((B, Hkv, HEAD_DIM, n), BF16)
        out_spec = pl.BlockSpec((None, None, HEAD_DIM, n), lambda b, h: (b, h, 0, 0))
    return pl.pallas_call(
        functools.partial(_compress_kernel, is_key=is_key),
        grid=(B, Hkv),
        in_specs=[
            pl.BlockSpec((None, None, n, half), lambda b, h: (b, h, 0, 0)),
            pl.BlockSpec((8, 2 * half), lambda b, h: (0, 0)),
            pl.BlockSpec((2 * half, hidden), lambda b, h: (0, 0)),
            pl.BlockSpec((hidden, LANES), lambda b, h: (0, 0)),
            pl.BlockSpec((1, LANES), lambda b, h: (0, 0)),
        ],
        out_specs=out_spec,
        out_shape=out_shape,
        compiler_params=pltpu.CompilerParams(
            dimension_semantics=("parallel", "parallel"), vmem_limit_bytes=VMEM_LIMIT),
        name="nsa_compress_k" if is_key else "nsa_compress_v",
    )(chunks, pe8, w1.astype(BF16), w2p, g)


def _nsa_cmp_kernel(q_ref, kc_ref, vct_ref, ct_ref, o_ref, selb_ref, used_ref, *, tq, group, n_heads):
    g = pl.program_id(1)
    qi = pl.program_id(2)
    n_cmp = kc_ref.shape[0]
    n_slc = ct_ref.shape[0]
    i0 = qi * tq
    qpos = i0 + lax.broadcasted_iota(jnp.int32, (1, tq), 1)
    cend = lax.broadcasted_iota(jnp.int32, (n_cmp, 1), 0) * CMP_STRIDE + (CMP_LEN - 1)
    valid = cend <= qpos
    qt = q_ref[...]
    kc = kc_ref[...]
    vct = vct_ref[...]
    psum = jnp.zeros((n_cmp, tq), F32)
    outs = []
    for e in range(group):
        slope = jnp.exp2(-8.0 * (group * g + e + 1).astype(F32) / n_heads)
        aug = _aug_const(tq, [slope, slope, -slope * i0.astype(F32)]).astype(BF16)
        qa = jnp.concatenate([qt[:, e * HEAD_DIM:(e + 1) * HEAD_DIM], aug], axis=1)
        t = jnp.where(valid, _dot_nt(kc, qa), NEG)
        m = jnp.max(t, axis=0, keepdims=True)
        p = jnp.where(valid, jnp.exp(t - m), 0.0)
        l = jnp.sum(p, axis=0, keepdims=True)
        p = p / jnp.where(l > 0, l, 1.0)
        outs.append(_dot(vct, p.astype(BF16)))
        psum = psum + p
    o_ref[...] = jnp.concatenate(outs, axis=0).T
    p_hi, p_lo = _split_bf16(psum)
    ct = ct_ref[...]
    imp = _dot(ct, p_hi) + _dot(ct, p_lo)
    jb = lax.broadcasted_iota(jnp.int32, (n_slc, tq), 0)
    qb = qpos // SLC_BLK
    forced = (jb == 0) | (jb == qb) | (jb == qb - 1)
    cur = jnp.where(forced, 3e38, jnp.where(jb > qb, -2e38, imp))
    sel = jnp.logical_and(_topk_member(cur, min(SLC_TOPK, n_slc)), jb <= qb)
    selb = jnp.where(sel, 0.0, NEG).T
    selb_ref[...] = selb.astype(BF16)
    for c in range(tq // NSA_TQ):
        hit = jnp.max(selb[c * NSA_TQ:(c + 1) * NSA_TQ], axis=0, keepdims=True) == 0.0
        used_ref[c] = jnp.where(hit, 1, 0)


def _nsa_cmp(act, kc, vct, batch, seq, col_q, n_heads, group, tq=256):
    Hkv = n_heads // group
    n_cmp = kc.shape[2]
    n_slc = seq // SLC_BLK
    cs = jnp.arange(n_cmp) * CMP_STRIDE
    ss = jnp.arange(n_slc) * SLC_BLK
    overlap = (cs[None, :] <= ss[:, None] + SLC_BLK - 1) & (cs[None, :] + CMP_LEN - 1 >= ss[:, None])
    overlap = overlap & (cs[None, :] + CMP_LEN <= seq)
    ct = overlap.astype(BF16)
    gw = group * HEAD_DIM
    return pl.pallas_call(
        functools.partial(_nsa_cmp_kernel, tq=tq, group=group, n_heads=n_heads),
        grid=(batch, Hkv, seq // tq),
        in_specs=[
            pl.BlockSpec((None, tq, gw), lambda b, g, i: (b, i, col_q + g)),
            pl.BlockSpec((None, None, n_cmp, LANES), lambda b, g, i: (b, g, 0, 0)),
            pl.BlockSpec((None, None, HEAD_DIM, n_cmp), lambda b, g, i: (b, g, 0, 0)),
            pl.BlockSpec((n_slc, n_cmp), lambda b, g, i: (0, 0)),
        ],
        out_specs=[
            pl.BlockSpec((None, tq, gw), lambda b, g, i: (b, i, g)),
            pl.BlockSpec((None, None, tq, n_slc), lambda b, g, i: (b, g, i, 0)),
            pl.BlockSpec((None, None, tq // NSA_TQ, 1, n_slc), lambda b, g, i: (b, g, i, 0, 0)),
        ],
        out_shape=[
            jax.ShapeDtypeStruct((batch, seq, n_heads * HEAD_DIM), F32),
            jax.ShapeDtypeStruct((batch, Hkv, seq, n_slc), BF16),
            jax.ShapeDtypeStruct((batch, Hkv, seq // NSA_TQ, 1, n_slc), jnp.int32),
        ],
        compiler_params=pltpu.CompilerParams(
            dimension_semantics=("parallel", "parallel", "parallel"),
            vmem_limit_bytes=VMEM_LIMIT),
        name="nsa_cmp",
    )(act, kc, vct, ct)


def _nsa_main_kernel(used_ref, q_ref, selb_ref, kvs_ref, kvw_ref, kconst_ref, ocmp_ref, gate_ref,
                     o_ref, ks_ref, vst_ref, kw_ref, vwt_ref, accs_ref, accw_ref, gt_ref, t_ref,
                     p_ref, tiles_ref, *, tq, tk, group, n_heads):
    g = pl.program_id(1)
    qi = pl.program_id(2)
    S = kvs_ref.shape[0]
    R = group * tq
    i0 = qi * tq

    @pl.when(qi == 0)
    def _():
        def body(c, carry):
            rows = pl.ds(pl.multiple_of(c * tk, tk), tk)
            kvs = kvs_ref[rows, :]
            kvw = kvw_ref[rows, :]
            kconst = kconst_ref[...]
            ks_ref[rows, :HEAD_DIM] = kvs[:, :HEAD_DIM]
            ks_ref[rows, HEAD_DIM:LANES] = kconst[:, :HEAD_DIM]
            kw_ref[rows, :] = jnp.concatenate([kvw[:, :HEAD_DIM], kconst[:, :HEAD_DIM]], axis=1)
            vst_ref[:, rows] = kvs.astype(F32).T[HEAD_DIM:, :].astype(BF16)
            vwt_ref[:, rows] = kvw.astype(F32).T[HEAD_DIM:, :].astype(BF16)
            key_blk = (c * tk + lax.broadcasted_iota(jnp.int32, (tk, n_slc), 0)) // SLC_BLK
            onehot = key_blk == lax.broadcasted_iota(jnp.int32, (tk, n_slc), 1)
            ks_ref[rows, LANES:] = jnp.where(onehot, 1.0, 0.0).astype(BF16)
            return carry

        n_slc = ks_ref.shape[1] - LANES
        lax.fori_loop(0, S // tk, body, 0)

    qt = q_ref[...]
    selb = selb_ref[...]
    qs_parts, qw_parts, slope_parts = [], [], []
    for e in range(group):
        slope = jnp.exp2(-8.0 * (group * g + e + 1).astype(F32) / n_heads)
        aug = _aug_const(tq, [slope]).astype(BF16)
        qe = qt[:, e * HEAD_DIM:(e + 1) * HEAD_DIM]
        qw_parts.append(jnp.concatenate([qe, aug], axis=1))
        qs_parts.append(jnp.concatenate([qe, aug, selb], axis=1))
        slope_parts.append(jnp.full((1, tq), slope, F32))
    q_slc = jnp.concatenate(qs_parts, axis=0)
    q_win = jnp.concatenate(qw_parts, axis=0)
    slope_row = jnp.concatenate(slope_parts, axis=1)
    qpos = i0 + lax.broadcasted_iota(jnp.int32, (1, R), 1) % tq

    nd = i0 // tk
    rows = lambda n: pl.ds(pl.multiple_of(n * tk, tk), tk)
    shift = lambda n: slope_row * (n * tk - i0).astype(F32)
    kpos = lambda n: n * tk + lax.broadcasted_iota(jnp.int32, (tk, 1), 0)
    causal = lambda n: kpos(n) <= qpos

    def in_window(n):
        kp = kpos(n)
        dist = qpos - kp
        return (dist >= 0) & (dist < NSA_WINDOW) & (kp >= 0)

    n_tiles = S // tk
    base = ((pl.program_id(0) * pl.num_programs(1) + g) * pl.num_programs(2) + qi) * n_tiles

    def collect(n, cnt):
        tiles_ref[cnt] = n
        return cnt + jnp.where(used_ref[base + n] != 0, 1, 0)

    tiles_ref[0] = nd
    n_used = lax.fori_loop(0, nd, collect, 1)
    assert 2 * tk >= NSA_WINDOW - 1 and tk % tq == 0
    tw = []
    for c in range(3):
        n = nd - 2 + c
        t = _dot_nt(kw_ref[rows(jnp.maximum(n, 0)), :], q_win) + shift(n)
        tw.append(jnp.where(in_window(n), t, NEG))
    mw = functools.reduce(jnp.maximum, [jnp.max(t, axis=0, keepdims=True) for t in tw])
    lw = jnp.zeros((1, R), F32)
    o_w = jnp.zeros((HEAD_DIM, R), F32)
    for c in range(3):
        p = jnp.exp(tw[c] - mw)
        lw = lw + jnp.sum(p, axis=0, keepdims=True)
        o_w = o_w + _dot(vwt_ref[:, rows(jnp.maximum(nd - 2 + c, 0))], p.astype(BF16))
    accw_ref[...] = o_w

    slc = _Stream(q_slc, lambda n: ks_ref[rows(n), :], lambda n: vst_ref[:, rows(n)],
                  accs_ref, t_ref, p_ref, shift, mask=causal)
    ((_, ls),) = _flash_sweep([slc], n_used, lambda pos: tiles_ref[pos])

    gz = gate_ref[...]
    gt_ref[...] = jax.nn.sigmoid(gz).T
    oc_t = ocmp_ref[...].T
    outs = []
    for e in range(group):
        head = group * g + e
        cols = slice(e * tq, (e + 1) * tq)
        o_s = accs_ref[:, cols] / ls[:, cols]
        o_w = accw_ref[:, cols] / lw[:, cols]
        o_c = oc_t[e * HEAD_DIM:(e + 1) * HEAD_DIM, :]
        g_c = gt_ref[pl.ds(3 * head, 1), :]
        g_s = gt_ref[pl.ds(3 * head + 1, 1), :]
        g_w = gt_ref[pl.ds(3 * head + 2, 1), :]
        outs.append(g_c * o_c + g_s * o_s + g_w * o_w)
    o_ref[...] = jnp.concatenate(outs, axis=0).T


def _nsa_main(act, selb, used, ocmp, gates, batch, seq, col_q, col_kvs, col_kvw, col_gate, n_heads,
              group):
    tq, tk = NSA_TQ, NSA_TK
    Hkv = n_heads // group
    n_slc = seq // SLC_BLK
    gw = group * HEAD_DIM
    R = group * tq
    jj = jnp.arange(tk, dtype=F32)
    kconst = jnp.zeros((tk, HEAD_DIM), F32).at[:, 0].set(jj).astype(BF16)
    blk_per_tile = tk // SLC_BLK
    used_tiles = jnp.max(used.reshape(batch, Hkv, seq // tq, n_slc // blk_per_tile, blk_per_tile),
                         axis=-1).reshape(-1).astype(jnp.int32)
    grid_spec = pltpu.PrefetchScalarGridSpec(
        num_scalar_prefetch=1,
        grid=(batch, Hkv, seq // tq),
        in_specs=[
            pl.BlockSpec((None, tq, gw), lambda b, g, i, u: (b, i, col_q + g)),
            pl.BlockSpec((None, None, tq, n_slc), lambda b, g, i, u: (b, g, i, 0)),
            pl.BlockSpec((None, seq, LANES), lambda b, g, i, u: (b, 0, col_kvs + g)),
            pl.BlockSpec((None, seq, LANES), lambda b, g, i, u: (b, 0, col_kvw + g)),
            pl.BlockSpec((tk, HEAD_DIM), lambda b, g, i, u: (0, 0)),
            pl.BlockSpec((None, tq, gw), lambda b, g, i, u: (b, i, g)),
            pl.BlockSpec((None, tq, LANES), lambda b, g, i, u: (b, i, col_gate)),
        ],
        out_specs=pl.BlockSpec((None, tq, gw), lambda b, g, i, u: (b, i, g)),
        scratch_shapes=[
            pltpu.VMEM((seq, LANES + n_slc), BF16),
            pltpu.VMEM((HEAD_DIM, seq), BF16),
            pltpu.VMEM((seq, LANES), BF16),
            pltpu.VMEM((HEAD_DIM, seq), BF16),
            pltpu.VMEM((HEAD_DIM, R), F32),
            pltpu.VMEM((HEAD_DIM, R), F32),
            pltpu.VMEM((LANES, tq), F32),
            pltpu.VMEM((FLASH_UNROLL, tk, R), F32), pltpu.VMEM((tk, R), BF16),
            pltpu.SMEM((seq // tk,), jnp.int32),
        ],
    )
    return pl.pallas_call(
        functools.partial(_nsa_main_kernel, tq=tq, tk=tk, group=group, n_heads=n_heads),
        grid_spec=grid_spec,
        out_shape=jax.ShapeDtypeStruct((batch, seq, n_heads * HEAD_DIM), F32),
        compiler_params=pltpu.CompilerParams(
            dimension_semantics=("parallel", "parallel", "arbitrary"),
            vmem_limit_bytes=VMEM_LIMIT),
        name="nsa_main",
    )(used_tiles, act, selb, act, act, kconst, ocmp, gates)


def _tile_gain(g, width):
    return jnp.tile(g.astype(F32), width // HEAD_DIM)


def _even_layer(x, ln, w_in, qkn, lam, subln, w_out, layer):
    B, S, D = x.shape
    M = B * S
    x2d = x.reshape(M, D)
    W = 512
    aq, ak, av, az, bq, bk, bv, bz = [w_in[:, i * W:(i + 1) * W] for i in range(8)]
    scale = HEAD_DIM ** -0.5
    ones, zeros = jnp.ones((W,), F32), jnp.zeros((W,), F32)
    w_b = jnp.concatenate([bq, bk, bv], axis=1).astype(BF16)
    gain_b = jnp.concatenate([_tile_gain(qkn[2], W) * scale, _tile_gain(qkn[3], W), ones])
    mask_b = np.repeat([1.0, 1.0, 0.0], W)
    act_b = _project(x2d, ln, w_b, gain_b, mask_b, BF16)
    w_a = jnp.concatenate([aq, ak, av, az, bz], axis=1).astype(BF16)
    gain_a = jnp.concatenate([_tile_gain(qkn[0], W) * scale, _tile_gain(qkn[1], W), ones, ones, ones])
    mask_a = np.repeat([1.0, 1.0, 0.0, 0.0, 0.0], W)
    act_a = _project(x2d, ln, w_a, gain_a, mask_a, F32)

    lam_init = 0.8 - 0.6 * math.exp(-0.3 * layer)
    lf = lam.astype(F32)
    lam_f = jnp.exp(jnp.sum(lf[0] * lf[1])) - jnp.exp(jnp.sum(lf[2] * lf[3])) + lam_init
    ob = _diff_attention(act_b.reshape(B, S, 3 * W), lam_f, subln, B, S, 0, 4, 8, 4, 1.0 - lam_init)
    oa = _dilated_attention(act_a.reshape(B, S, 5 * W), B, S, 0, 4, 8, 8)
    out = _out_project(x2d, oa.reshape(M, W), ob.reshape(M, W), act_a, 3, 4, w_out)
    return out.reshape(B, S, D)


def _odd_layer(x, ln, w_in, qkn, phi_pe, phi_w1, phi_w2, w_out):
    B, S, D = x.shape
    M = B * S
    x2d = x.reshape(M, D)
    W, KW = 512, 128
    offs = [0]
    for s in (W, W, W, W, W, KW, KW, KW, KW, KW, KW, W, 24):
        offs.append(offs[-1] + s)
    cq, ck, cv, cz, dq, dkc, dvc, dks, dvs, dkw, dvw, dz, dg = [
        w_in[:, offs[i]:offs[i + 1]] for i in range(13)]
    scale = HEAD_DIM ** -0.5
    hd = HEAD_DIM
    ones = lambda n: jnp.ones((n,), F32)
    zeros = lambda n: jnp.zeros((n,), F32)
    kv_pair = lambda k, v: jnp.concatenate([k[:, :hd], v[:, :hd], k[:, hd:], v[:, hd:]], axis=1)
    w_b = jnp.concatenate([cq, ck, cv, dq, kv_pair(dks, dvs), kv_pair(dkw, dvw), dkc, dvc,
                           jnp.zeros((D, 2 * KW), w_in.dtype)], axis=1).astype(BF16)
    kv_gain = lambda g: jnp.concatenate([g, ones(hd), g, ones(hd)])
    kv_mask = np.repeat([1.0, 0.0, 1.0, 0.0], hd)
    gain_b = jnp.concatenate([_tile_gain(qkn[0], W) * scale, _tile_gain(qkn[1], W), ones(W),
                              _tile_gain(qkn[2], W) * scale, kv_gain(qkn[4].astype(F32)),
                              kv_gain(qkn[5].astype(F32)), ones(4 * KW)])
    mask_b = np.concatenate([np.repeat([1.0, 1.0, 0.0, 1.0], W), kv_mask, kv_mask, np.zeros(4 * KW)])
    act_b = _project(x2d, ln, w_b, gain_b, mask_b, BF16).reshape(B, S, 6 * W)
    w_g = jnp.concatenate([cz, dz, dg, jnp.zeros((D, W - 24), w_in.dtype)], axis=1).astype(BF16)
    act_g = _project(x2d, ln, w_g, ones(3 * W), np.zeros(3 * W), F32)

    oc = _moba_attention(act_b, B, S, 0, 4, 8, 8, 256, 3)

    def chunks(t):
        t = t.reshape(B, S // CMP_STRIDE, CMP_STRIDE, 2, hd)
        return jnp.transpose(t, (0, 3, 1, 2, 4)).reshape(B, 2, S // CMP_STRIDE, CMP_STRIDE * hd)

    kc = _compress(chunks(act_b[:, :, 20 * KW:21 * KW]), phi_pe[0], phi_w1[0], phi_w2[0], qkn[3], True)
    vct = _compress(chunks(act_b[:, :, 21 * KW:22 * KW]), phi_pe[1], phi_w1[1], phi_w2[1], qkn[3], False)
    ocmp, selb, used = _nsa_cmp(act_b, kc, vct, B, S, 6, 8, 4)
    od = _nsa_main(act_b, selb, used, ocmp, act_g.reshape(B, S, 3 * W), B, S, 6, 16, 18, 8, 8, 4)
    out = _out_project(x2d, oc.reshape(M, W), od.reshape(M, W), act_g, 0, 1, w_out)
    return out.reshape(B, S, D)


def kernel(x, ln_e, w_in_e, qkn_e, lam_e, subln_e, w_out_e, ln_o, w_in_o, qkn_o, phi_pe, phi_w1, phi_w2, w_out_o):
    n_layers = ln_e.shape[0] + ln_o.shape[0]
    for layer in range(n_layers):
        i = layer // 2
        if layer % 2 == 0:
            x = _even_layer(x, ln_e[i], w_in_e[i], qkn_e[i], lam_e[i], subln_e[i], w_out_e[i], layer)
        else:
            x = _odd_layer(x, ln_o[i], w_in_o[i], qkn_o[i], phi_pe[i], phi_w1[i], phi_w2[i], w_out_o[i])
    return x
```

```python
import functools
import math

import jax
import jax.numpy as jnp
import numpy as np
from jax import lax
from jax.experimental import pallas as pl
from jax.experimental.pallas import tpu as pltpu

HEAD_DIM = 64
LANES = 128
RMS_EPS = 1e-6
NEG = -2e30
M_INIT = -1e30
VMEM_LIMIT = 56 * 1024 * 1024

F32 = jnp.float32
BF16 = jnp.bfloat16

_NT = (((1,), (1,)), ((), ()))


def _dot(a, b):
    return jnp.dot(a, b, preferred_element_type=F32)


def _dot_nt(a, b):
    return lax.dot_general(a, b, _NT, preferred_element_type=F32)


def _alibi_slopes(n):
    return [2.0 ** (-8.0 * (i + 1) / n) for i in range(n)]


def _split_bf16(x):
    hi = x.astype(BF16)
    lo = (x - hi.astype(F32)).astype(BF16)
    return hi, lo


NORM_CHUNK = 256


def _proj_kernel(x_ref, ln_ref, w_ref, gain_ref, nmask_ref, bd_ref, o_ref, *, norm_chunks):
    x = x_ref[...]
    ms = jnp.mean(x * x, axis=-1, keepdims=True)
    h = (x * lax.rsqrt(ms + RMS_EPS) * ln_ref[...]).astype(BF16)
    bd = bd_ref[...]
    for c, normed in enumerate(norm_chunks):
        cols = slice(c * NORM_CHUNK, (c + 1) * NORM_CHUNK)
        y = _dot(h, w_ref[:, cols])
        if normed:
            hi, lo = _split_bf16(y * y)
            ms = _dot(hi, bd) + _dot(lo, bd)
            yn = y * lax.rsqrt(ms + RMS_EPS) * gain_ref[:, cols]
            y = jnp.where(nmask_ref[:, cols] > 0.5, yn, y)
        o_ref[:, cols] = y.astype(o_ref.dtype)


def _project(x2d, ln, w, gain, nmask, out_dtype, tm=1024):
    M, D = x2d.shape
    N = w.shape[1]
    assert M % tm == 0 and N % NORM_CHUNK == 0
    norm_chunks = [bool(c) for c in np.asarray(nmask).reshape(-1, NORM_CHUNK).max(axis=1) > 0.5]
    nmask = jnp.asarray(nmask, F32)
    r = jnp.arange(NORM_CHUNK) // HEAD_DIM
    bd = jnp.where(r[:, None] == r[None, :], 1.0 / HEAD_DIM, 0.0).astype(BF16)
    return pl.pallas_call(
        functools.partial(_proj_kernel, norm_chunks=tuple(norm_chunks)),
        grid=(M // tm,),
        in_specs=[
            pl.BlockSpec((tm, D), lambda i: (i, 0)),
            pl.BlockSpec((1, D), lambda i: (0, 0)),
            pl.BlockSpec((D, N), lambda i: (0, 0)),
            pl.BlockSpec((1, N), lambda i: (0, 0)),
            pl.BlockSpec((1, N), lambda i: (0, 0)),
            pl.BlockSpec((NORM_CHUNK, NORM_CHUNK), lambda i: (0, 0)),
        ],
        out_specs=pl.BlockSpec((tm, N), lambda i: (i, 0)),
        out_shape=jax.ShapeDtypeStruct((M, N), out_dtype),
        compiler_params=pltpu.CompilerParams(
            dimension_semantics=("parallel",), vmem_limit_bytes=VMEM_LIMIT),
        name="proj",
    )(x2d, ln.reshape(1, D).astype(F32), w, gain.reshape(1, N).astype(F32),
      nmask.reshape(1, N).astype(F32), bd)


def _outproj_kernel(x_ref, oa_ref, ob_ref, za_ref, zb_ref, wa_ref, wb_ref, o_ref):
    za = za_ref[...]
    zb = zb_ref[...]
    ma = (oa_ref[...] * (za * jax.nn.sigmoid(za))).astype(BF16)
    mb = (ob_ref[...] * (zb * jax.nn.sigmoid(zb))).astype(BF16)
    o_ref[...] = x_ref[...] + _dot(ma, wa_ref[...]) + _dot(mb, wb_ref[...])


def _out_project(x2d, oa, ob, z, za_blk, zb_blk, w_out, tm=512):
    M, D = x2d.shape
    W = oa.shape[1]
    wa = w_out[:W].astype(BF16)
    wb = w_out[W:].astype(BF16)
    return pl.pallas_call(
        _outproj_kernel,
        grid=(M // tm,),
        in_specs=[
            pl.BlockSpec((tm, D), lambda i: (i, 0)),
            pl.BlockSpec((tm, W), lambda i: (i, 0)),
            pl.BlockSpec((tm, W), lambda i: (i, 0)),
            pl.BlockSpec((tm, W), lambda i: (i, za_blk)),
            pl.BlockSpec((tm, W), lambda i: (i, zb_blk)),
            pl.BlockSpec((W, D), lambda i: (0, 0)),
            pl.BlockSpec((W, D), lambda i: (0, 0)),
        ],
        out_specs=pl.BlockSpec((tm, D), lambda i: (i, 0)),
        out_shape=jax.ShapeDtypeStruct((M, D), F32),
        compiler_params=pltpu.CompilerParams(
            dimension_semantics=("parallel",), vmem_limit_bytes=VMEM_LIMIT),
        name="outproj",
    )(x2d, oa, ob, z, z, wa, wb)


class _Stream:
    def __init__(self, q, k_tile, vt_tile, acc_ref, t_ref, p_ref, shift, bias=None, mask=None,
                 mask_every_tile=False):
        self.q, self.k_tile, self.vt_tile = q, k_tile, vt_tile
        self.acc_ref, self.t_ref, self.p_ref = acc_ref, t_ref, p_ref
        self.shift, self.bias, self.mask = shift, bias, mask
        self.mask_every_tile = mask_every_tile

    def scores(self, n):
        return _dot_nt(self.k_tile(n), self.q)

    def softmax(self, t, n, m, l, pad, masked):
        if masked:
            t = jnp.where(self.mask(n), t, NEG)
        off = self.shift(n) + pad
        if self.bias is not None:
            off = off + self.bias(n)
        m_new = jnp.maximum(m, jnp.max(t, axis=0, keepdims=True) + off)
        p = jnp.exp(t - (m_new - off))
        alpha = jnp.exp(m - m_new)
        return m_new, alpha * l + jnp.sum(p, axis=0, keepdims=True), alpha, p.astype(BF16)


FLASH_UNROLL = 2


def _flash_sweep(streams, count, tile_of):
    U = FLASH_UNROLL
    R = streams[0].q.shape[0]
    last = count - 1
    W = 2 + U
    state = []
    for s in streams:
        s.acc_ref[...] = jnp.zeros_like(s.acc_ref)
        s.p_ref[...] = jnp.zeros_like(s.p_ref)
        for k in range(U):
            s.t_ref[k] = s.scores(tile_of(jnp.minimum(k, last)))
        state += [jnp.full((1, R), M_INIT, F32), jnp.zeros((1, R), F32)]
        state += [jnp.ones((1, R), F32)] * U

    def flush(n_prev, state):
        pvs = [[_dot(s.vt_tile(n_prev[k]), s.p_ref[k]) for k in range(U)] for s in streams]
        for i, s in enumerate(streams):
            acc = s.acc_ref[...]
            for k in range(U):
                acc = state[W * i + 2 + k] * acc + pvs[i][k]
            s.acc_ref[...] = acc

    def body(g, carry, first_body):
        n_prev, state = list(carry[:U]), list(carry[U:])
        base = U * g
        nxt = [[s.scores(tile_of(jnp.minimum(base + U + k, last))) for k in range(U)]
               for s in streams]
        flush(n_prev, state)
        for k in range(U):
            n = tile_of(jnp.minimum(base + k, last))
            pad = jnp.where(base + k < count, 0.0, NEG)
            for i, s in enumerate(streams):
                m, l = state[W * i:W * i + 2]
                masked = s.mask is not None and (s.mask_every_tile or (first_body and k == 0))
                m, l, a, p = s.softmax(s.t_ref[k], n, m, l, pad, masked)
                state[W * i:W * i + 2] = [m, l]
                state[W * i + 2 + k] = a
                s.p_ref[k] = p
            n_prev[k] = n
        for i, s in enumerate(streams):
            for k in range(U):
                s.t_ref[k] = nxt[i][k]
        return tuple(n_prev) + tuple(state)

    carry = body(0, (tile_of(0),) * U + tuple(state), True)
    carry = lax.fori_loop(1, (count + U - 1) // U, lambda g, c: body(g, c, False), carry)
    flush(carry[:U], carry[U:])
    return [(carry[U + W * i], carry[U + W * i + 1]) for i in range(len(streams))]


def _aug_const(rows, col_vals):
    lane = lax.broadcasted_iota(jnp.int32, (rows, HEAD_DIM), 1)
    out = jnp.zeros((rows, HEAD_DIM), F32)
    for c, v in enumerate(col_vals):
        out = jnp.where(lane == c, v, out)
    return out


def _diff_kernel(lam_ref, q_ref, k_ref, v_ref, subln_ref, o_ref,
                 k1_ref, k2_ref, vt_ref, acc1_ref, acc2_ref, t_ref, p_ref, *, tq, n_heads, out_scale):
    h = pl.program_id(1)
    qi = pl.program_id(2)
    S = k_ref.shape[0]
    tk = tq
    slope = jnp.exp2(-8.0 * (h + 1).astype(F32) / n_heads)

    @pl.when(qi == 0)
    def _():
        jj = lax.broadcasted_iota(jnp.int32, (tk, 1), 0).astype(F32)
        kaug = _aug_const(tk, [jj]).astype(BF16)

        def body(c, carry):
            rows = pl.ds(pl.multiple_of(c * tk, tk), tk)
            kc = k_ref[rows, :]
            k1_ref[rows, :] = jnp.concatenate([kc[:, :HEAD_DIM], kaug], axis=1)
            k2_ref[rows, :] = jnp.concatenate([kc[:, HEAD_DIM:], kaug], axis=1)
            vt_ref[:, rows] = v_ref[rows, :].astype(F32).T.astype(BF16)
            return carry

        lax.fori_loop(0, S // tk, body, 0)

    qt = q_ref[...]
    qaug = _aug_const(tq, [slope]).astype(BF16)
    q1 = jnp.concatenate([qt[:, :HEAD_DIM], qaug], axis=1)
    q2 = jnp.concatenate([qt[:, HEAD_DIM:], qaug], axis=1)

    rows = lambda n: pl.ds(pl.multiple_of(n * tk, tk), tk)
    qpos = qi * tq + lax.broadcasted_iota(jnp.int32, (1, tq), 1)
    causal = lambda n: n * tk + lax.broadcasted_iota(jnp.int32, (tk, 1), 0) <= qpos
    shift = lambda n: slope * ((n - qi) * tk).astype(F32)
    vt_tile = lambda n: vt_ref[:, rows(n)]
    streams = [
        _Stream(q1, lambda n: k1_ref[rows(n), :], vt_tile, acc1_ref, t_ref.at[0], p_ref.at[0],
                shift, mask=causal),
        _Stream(q2, lambda n: k2_ref[rows(n), :], vt_tile, acc2_ref, t_ref.at[1], p_ref.at[1],
                shift, mask=causal),
    ]
    (m1, l1), (m2, l2) = _flash_sweep(streams, qi + 1, lambda pos: jnp.where(pos == 0, qi, pos - 1))

    d = acc1_ref[...] / l1 - lam_ref[0] * (acc2_ref[...] / l2)
    r = lax.rsqrt(jnp.mean(d * d, axis=0, keepdims=True) + RMS_EPS)
    o_ref[...] = (d * r).T * (subln_ref[...] * out_scale)


def _diff_attention(qkv, lam_f, subln, batch, seq, col_q, col_k, col_v, n_heads, out_scale, tq=256):
    dv = 2 * HEAD_DIM
    kern = functools.partial(_diff_kernel, tq=tq, n_heads=n_heads, out_scale=out_scale)
    grid_spec = pltpu.PrefetchScalarGridSpec(
        num_scalar_prefetch=1,
        grid=(batch, n_heads, seq // tq),
        in_specs=[
            pl.BlockSpec((None, tq, dv), lambda b, h, i, s: (b, i, col_q + h)),
            pl.BlockSpec((None, seq, dv), lambda b, h, i, s: (b, 0, col_k + h)),
            pl.BlockSpec((None, seq, dv), lambda b, h, i, s: (b, 0, col_v + h)),
            pl.BlockSpec((1, dv), lambda b, h, i, s: (0, 0)),
        ],
        out_specs=pl.BlockSpec((None, tq, dv), lambda b, h, i, s: (b, i, h)),
        scratch_shapes=[
            pltpu.VMEM((seq, LANES), BF16), pltpu.VMEM((seq, LANES), BF16),
            pltpu.VMEM((dv, seq), BF16),
            pltpu.VMEM((dv, tq), F32), pltpu.VMEM((dv, tq), F32),
            pltpu.VMEM((2, FLASH_UNROLL, tq, tq), F32), pltpu.VMEM((2, FLASH_UNROLL, tq, tq), BF16),
        ],
    )
    return pl.pallas_call(
        kern,
        grid_spec=grid_spec,
        out_shape=jax.ShapeDtypeStruct((batch, seq, n_heads * dv), F32),
        compiler_params=pltpu.CompilerParams(
            dimension_semantics=("parallel", "parallel", "arbitrary"),
            vmem_limit_bytes=VMEM_LIMIT),
        name="diff_attn",
    )(lam_f.reshape(1).astype(F32), qkv, qkv, qkv, subln.reshape(1, dv).astype(F32))


BAND = 128
DILATIONS = (1, 4, 16)


def _dilated_kernel(q_ref, kp_ref, kc_ref, vp_ref, vc_ref, o_ref, kbuf_ref, vbuf_ref, st_ref,
                    *, chunk, n_heads):
    pair = pl.program_id(1)
    ci = pl.program_id(2)
    kbuf_ref[:chunk, :] = kp_ref[...]
    kbuf_ref[chunk:, :] = kc_ref[...]
    vbuf_ref[:chunk, :] = vp_ref[...]
    vbuf_ref[chunk:, :] = vc_ref[...]

    qi = lax.broadcasted_iota(jnp.int32, (BAND, 2 * BAND), 0)
    kj = lax.broadcasted_iota(jnp.int32, (BAND, 2 * BAND), 1)
    dist = qi + BAND - kj
    in_band = (dist >= 0) & (dist <= BAND)
    dist_f = dist.astype(F32)
    slopes = [jnp.exp2(-8.0 * (2 * pair + e + 1).astype(F32) / n_heads) for e in range(2)]
    lane_q = lax.broadcasted_iota(jnp.int32, (BAND, LANES), 1)
    own_half = [lane_q < HEAD_DIM, lane_q >= HEAD_DIM]

    for pi, dil in enumerate(DILATIONS):
        span = dil * BAND
        n_blk = chunk // span

        def block(idx, carry, dil=dil, span=span, n_blk=n_blk, pi=pi):
            r = idx // n_blk
            b = idx % n_blk
            q0 = r + b * span
            qb = q_ref[pl.ds(q0, BAND, stride=dil), :]
            kb = kbuf_ref[pl.ds(chunk + q0 - span, 2 * BAND, stride=dil), :]
            vb = vbuf_ref[pl.ds(chunk + q0 - span, 2 * BAND, stride=dil), :]
            first = jnp.logical_and(ci == 0, b == 0)
            ok = in_band & (kj >= jnp.where(first, BAND, 0))
            kb16 = kb.astype(BF16)
            vb16 = vb.astype(BF16)
            packed = []
            for e in range(2):
                qe = jnp.where(own_half[e], qb, 0.0).astype(BF16)
                s = _dot_nt(qe, kb16)
                s = jnp.where(ok, s - (slopes[e] * dil) * dist_f, NEG)
                m = jnp.max(s, axis=1, keepdims=True)
                p = jnp.exp(s - m)
                l = jnp.sum(p, axis=1, keepdims=True)
                o = _dot(p.astype(BF16), vb16) / l
                packed.append(jnp.where(own_half[e], o, m + jnp.log(l)))
            for e in range(2):
                st_ref[pi, e, pl.ds(q0, BAND, stride=dil), :] = packed[e]
            return carry

        lax.fori_loop(0, dil * n_blk, block, 0, unroll=4)

    rows_per = 256
    lane = lax.broadcasted_iota(jnp.int32, (rows_per, LANES), 1)

    def merge(c, carry):
        rows = pl.ds(pl.multiple_of(c * rows_per, rows_per), rows_per)
        outs = []
        for e in range(2):
            xs = [st_ref[pi, e, rows, :] for pi in range(len(DILATIONS))]
            top = functools.reduce(jnp.maximum, xs)
            num = jnp.zeros((rows_per, LANES), F32)
            den = jnp.zeros((rows_per, LANES), F32)
            for x in xs:
                w = pltpu.roll(jnp.exp(x - top), HEAD_DIM, axis=1)
                num = num + w * x
                den = den + w
            outs.append(num / den)
        o_ref[rows, :] = jnp.where(lane < HEAD_DIM, outs[0], outs[1])
        return carry

    lax.fori_loop(0, chunk // rows_per, merge, 0)


def _dilated_attention(act, batch, seq, col_q, col_k, col_v, n_heads, chunk=2048):
    assert chunk % (max(DILATIONS) * BAND) == 0 and seq % chunk == 0
    prev = lambda c: jnp.maximum(c - 1, 0)
    return pl.pallas_call(
        functools.partial(_dilated_kernel, chunk=chunk, n_heads=n_heads),
        grid=(batch, n_heads // 2, seq // chunk),
        in_specs=[
            pl.BlockSpec((None, chunk, LANES), lambda b, p, c: (b, c, col_q + p)),
            pl.BlockSpec((None, chunk, LANES), lambda b, p, c: (b, prev(c), col_k + p)),
            pl.BlockSpec((None, chunk, LANES), lambda b, p, c: (b, c, col_k + p)),
            pl.BlockSpec((None, chunk, LANES), lambda b, p, c: (b, prev(c), col_v + p)),
            pl.BlockSpec((None, chunk, LANES), lambda b, p, c: (b, c, col_v + p)),
        ],
        out_specs=pl.BlockSpec((None, chunk, LANES), lambda b, p, c: (b, c, p)),
        out_shape=jax.ShapeDtypeStruct((batch, seq, n_heads * HEAD_DIM), F32),
        scratch_shapes=[
            pltpu.VMEM((2 * chunk, LANES), F32),
            pltpu.VMEM((2 * chunk, LANES), F32),
            pltpu.VMEM((len(DILATIONS), 2, chunk, LANES), F32),
        ],
        compiler_params=pltpu.CompilerParams(
            dimension_semantics=("parallel", "parallel", "arbitrary"),
            vmem_limit_bytes=VMEM_LIMIT),
        name="dilated_attn",
    )(act, act, act, act, act)


TAKEN = -3e38


def _topk_member(cur, k):
    n = cur.shape[0]
    row = lax.broadcasted_iota(jnp.int32, cur.shape, 0)
    sel = jnp.zeros(cur.shape, jnp.bool_)
    for _ in range(k):
        mx = jnp.max(cur, axis=0, keepdims=True)
        first = jnp.min(jnp.where(cur == mx, row, n), axis=0, keepdims=True)
        pick = row == first
        sel = jnp.logical_or(sel, pick)
        cur = jnp.where(pick, TAKEN, cur)
    return sel


def _moba_kernel(q_ref, k_ref, v_ref, o_ref, kaug_ref, vt_ref, kmean_ref, selb_ref, acc_ref,
                 t_ref, p_ref, *, blk, topk, n_heads):
    pair = pl.program_id(1)
    qi = pl.program_id(2)
    S = k_ref.shape[0]
    nblk = S // blk
    tq = tk = blk

    @pl.when(qi == 0)
    def _():
        jj = lax.broadcasted_iota(jnp.int32, (tk, 1), 0).astype(F32)
        kaug = _aug_const(tk, [jj]).astype(BF16)

        def body(c, carry):
            rows = pl.ds(pl.multiple_of(c * tk, tk), tk)
            kc = k_ref[rows, :]
            vc = v_ref[rows, :].astype(F32).T.astype(BF16)
            for e in range(2):
                kaug_ref[e, rows, :] = jnp.concatenate(
                    [kc[:, e * HEAD_DIM:(e + 1) * HEAD_DIM], kaug], axis=1)
                vt_ref[e, :, rows] = vc[e * HEAD_DIM:(e + 1) * HEAD_DIM, :]
            kmean_ref[pl.ds(c, 1), :] = jnp.mean(kc.astype(F32), axis=0, keepdims=True)
            return carry

        lax.fori_loop(0, nblk, body, 0)

    qt = q_ref[...]
    blk_row = lax.broadcasted_iota(jnp.int32, (nblk, tq), 0)
    past = blk_row < qi
    qs, slopes = [], []
    for e in range(2):
        slope = jnp.exp2(-8.0 * (2 * pair + e + 1).astype(F32) / n_heads)
        qe = qt[:, e * HEAD_DIM:(e + 1) * HEAD_DIM]
        km_hi, km_lo = _split_bf16(kmean_ref[:, e * HEAD_DIM:(e + 1) * HEAD_DIM])
        gate = _dot_nt(km_hi, qe) + _dot_nt(km_lo, qe)
        sel = _topk_member(jnp.where(past, gate, -2e38), topk)
        selb_ref[e] = jnp.where(jnp.logical_and(sel, past) | (blk_row == qi), 0.0, NEG)
        qs.append(jnp.concatenate([qe, _aug_const(tq, [slope]).astype(BF16)], axis=1))
        slopes.append(slope)

    rows = lambda n: pl.ds(pl.multiple_of(n * tk, tk), tk)
    qpos = qi * tq + lax.broadcasted_iota(jnp.int32, (1, tq), 1)
    causal = lambda n: n * tk + lax.broadcasted_iota(jnp.int32, (tk, 1), 0) <= qpos
    streams = [
        _Stream(qs[e],
                functools.partial(lambda n, e: kaug_ref[e, rows(n), :], e=e),
                functools.partial(lambda n, e: vt_ref[e, :, rows(n)], e=e),
                acc_ref.at[e], t_ref.at[e], p_ref.at[e],
                functools.partial(lambda n, e: slopes[e] * ((n - qi) * tk).astype(F32), e=e),
                bias=functools.partial(lambda n, e: selb_ref[e, pl.ds(n, 1), :], e=e),
                mask=causal)
        for e in range(2)]
    stats = _flash_sweep(streams, qi + 1, lambda pos: jnp.where(pos == 0, qi, pos - 1))
    o_ref[...] = jnp.concatenate([acc_ref[e] / stats[e][1] for e in range(2)], axis=0).T


def _moba_attention(qkv, batch, seq, col_q, col_k, col_v, n_heads, blk, topk):
    kern = functools.partial(_moba_kernel, blk=blk, topk=topk, n_heads=n_heads)
    nblk = seq // blk
    return pl.pallas_call(
        kern,
        grid=(batch, n_heads // 2, nblk),
        in_specs=[
            pl.BlockSpec((None, blk, LANES), lambda b, p, i: (b, i, col_q + p)),
            pl.BlockSpec((None, seq, LANES), lambda b, p, i: (b, 0, col_k + p)),
            pl.BlockSpec((None, seq, LANES), lambda b, p, i: (b, 0, col_v + p)),
        ],
        out_specs=pl.BlockSpec((None, blk, LANES), lambda b, p, i: (b, i, p)),
        out_shape=jax.ShapeDtypeStruct((batch, seq, n_heads * HEAD_DIM), F32),
        scratch_shapes=[
            pltpu.VMEM((2, seq, LANES), BF16),
            pltpu.VMEM((2, HEAD_DIM, seq), BF16),
            pltpu.VMEM((nblk, LANES), F32),
            pltpu.VMEM((2, nblk, blk), F32),
            pltpu.VMEM((2, HEAD_DIM, blk), F32),
            pltpu.VMEM((2, FLASH_UNROLL, blk, blk), F32),
            pltpu.VMEM((2, FLASH_UNROLL, blk, blk), BF16),
        ],
        compiler_params=pltpu.CompilerParams(
            dimension_semantics=("parallel", "parallel", "arbitrary"),
            vmem_limit_bytes=VMEM_LIMIT),
        name="moba_attn",
    )(qkv, qkv, qkv)


CMP_STRIDE = 16
CMP_LEN = 32
SLC_BLK = 64
SLC_TOPK = 16
NSA_WINDOW = 512
NSA_TQ = 128
NSA_TK = 256


def _compress_kernel(c_ref, pe_ref, w1_ref, w2_ref, g_ref, o_ref, *, is_key):
    half = c_ref.shape[1]
    n = c_ref.shape[0]
    c = c_ref[...]
    w1 = w1_ref[...]
    a_lo = _dot(c, w1[:half])
    a_hi = _dot(c, w1[half:])
    pe_hi, pe_lo = _split_bf16(pe_ref[...])
    c0 = (_dot(pe_hi, w1) + _dot(pe_lo, w1))[0:1]
    y = a_lo + pltpu.roll(a_hi, n - 1, axis=0) + c0
    hid = (y * jax.nn.sigmoid(y)).astype(BF16)
    if is_key:
        z = _dot(hid, w2_ref[...])
    else:
        z = _dot(hid.astype(F32), w2_ref[...].astype(F32))
    if is_key:
        ms = jnp.sum(z * z, axis=-1, keepdims=True) * (1.0 / HEAD_DIM)
        kn = z * lax.rsqrt(ms + RMS_EPS) * g_ref[...]
        i = lax.broadcasted_iota(jnp.int32, (n, 1), 0)
        aug = _aug_const(n, [(i // 16 * 256).astype(F32), (i % 16 * 16).astype(F32), 1.0])
        o_ref[...] = jnp.concatenate([kn[:, :HEAD_DIM], aug], axis=1).astype(BF16)
    else:
        o_ref[...] = z.T[:HEAD_DIM, :].astype(BF16)


def _compress(chunks, pe, w1, w2, gain, is_key):
    B, Hkv, n, half = chunks.shape
    hidden = w1.shape[1]
    pe8 = jnp.broadcast_to(pe.reshape(1, 2 * half).astype(F32), (8, 2 * half))
    w2p = jnp.pad(w2, ((0, 0), (0, LANES - w2.shape[1]))).astype(BF16)
    g = jnp.pad(gain.astype(F32), (0, LANES - HEAD_DIM)).reshape(1, LANES)
    if is_key:
        out_shape = jax.ShapeDtypeStruct((B, Hkv, n, LANES), BF16)
        out_spec = pl.BlockSpec((None, None, n, LANES), lambda b, h: (b, h, 0, 0))
    else:
        out_shape = jax.ShapeDtypeStruct((B, Hkv, HEAD_DIM, n), BF16)
        out_spec = pl.BlockSpec((None, None, HEAD_DIM, n), lambda b, h: (b, h, 0, 0))
    return pl.pallas_call(
        functools.partial(_compress_kernel, is_key=is_key),
        grid=(B, Hkv),
        in_specs=[
            pl.BlockSpec((None, None, n, half), lambda b, h: (b, h, 0, 0)),
            pl.BlockSpec((8, 2 * half), lambda b, h: (0, 0)),
            pl.BlockSpec((2 * half, hidden), lambda b, h: (0, 0)),
            pl.BlockSpec((hidden, LANES), lambda b, h: (0, 0)),
            pl.BlockSpec((1, LANES), lambda b, h: (0, 0)),
        ],
        out_specs=out_spec,
        out_shape=out_shape,
        compiler_params=pltpu.CompilerParams(
            dimension_semantics=("parallel", "parallel"), vmem_limit_bytes=VMEM_LIMIT),
        name="nsa_compress_k" if is_key else "nsa_compress_v",
    )(chunks, pe8, w1.astype(BF16), w2p, g)


CMP_ROWS = 128


def _nsa_cmp_kernel(q_ref, kc_ref, vct_ref, ct_ref, o_ref, selb_ref, used_ref, imp_ref,
                    *, tq, group, n_heads):
    g = pl.program_id(1)
    qi = pl.program_id(2)
    n_cmp = kc_ref.shape[0]
    n_slc = ct_ref.shape[0]
    i0 = qi * tq
    qpos = i0 + lax.broadcasted_iota(jnp.int32, (1, tq), 1)
    qt = q_ref[...]

    def attend(rows):
        cend = lax.broadcasted_iota(jnp.int32, (rows, 1), 0) * CMP_STRIDE + (CMP_LEN - 1)
        valid = cend <= qpos
        kc = kc_ref[:rows, :]
        vct = vct_ref[:, :rows]
        psum = jnp.zeros((rows, tq), F32)
        outs = []
        for e in range(group):
            slope = jnp.exp2(-8.0 * (group * g + e + 1).astype(F32) / n_heads)
            aug = _aug_const(tq, [slope, slope, -slope * i0.astype(F32)]).astype(BF16)
            qa = jnp.concatenate([qt[:, e * HEAD_DIM:(e + 1) * HEAD_DIM], aug], axis=1)
            t = jnp.where(valid, _dot_nt(kc, qa), NEG)
            m = jnp.max(t, axis=0, keepdims=True)
            p = jnp.where(valid, jnp.exp(t - m), 0.0)
            l = jnp.sum(p, axis=0, keepdims=True)
            p = p / jnp.where(l > 0, l, 1.0)
            outs.append(_dot(vct, p.astype(BF16)))
            psum = psum + p
        o_ref[...] = jnp.concatenate(outs, axis=0).T
        p_hi, p_lo = _split_bf16(psum)
        ct = ct_ref[:, :rows]
        imp_ref[...] = _dot(ct, p_hi) + _dot(ct, p_lo)

    needed = (i0 + tq - CMP_LEN) // CMP_STRIDE + 1
    n_var = max(n_cmp // CMP_ROWS, 1)
    for v in range(n_var):
        rows = n_cmp if v == n_var - 1 else (v + 1) * CMP_ROWS
        cond = needed > v * CMP_ROWS
        if v < n_var - 1:
            cond = jnp.logical_and(cond, needed <= (v + 1) * CMP_ROWS)
        pl.when(cond)(functools.partial(attend, rows))

    imp = imp_ref[...]
    jb = lax.broadcasted_iota(jnp.int32, (n_slc, tq), 0)
    qb = qpos // SLC_BLK
    forced = (jb == 0) | (jb == qb) | (jb == qb - 1)
    cur = jnp.where(forced, 3e38, jnp.where(jb > qb, -2e38, imp))
    sel = jnp.logical_and(_topk_member(cur, min(SLC_TOPK, n_slc)), jb <= qb)
    selb = jnp.where(sel, 0.0, NEG).T
    selb_ref[...] = selb.astype(BF16)
    for c in range(tq // NSA_TQ):
        hit = jnp.max(selb[c * NSA_TQ:(c + 1) * NSA_TQ], axis=0, keepdims=True) == 0.0
        used_ref[c] = jnp.where(hit, 1, 0)


def _nsa_cmp(act, kc, vct, batch, seq, col_q, n_heads, group, tq=256):
    Hkv = n_heads // group
    n_cmp = kc.shape[2]
    n_slc = seq // SLC_BLK
    cs = jnp.arange(n_cmp) * CMP_STRIDE
    ss = jnp.arange(n_slc) * SLC_BLK
    overlap = (cs[None, :] <= ss[:, None] + SLC_BLK - 1) & (cs[None, :] + CMP_LEN - 1 >= ss[:, None])
    overlap = overlap & (cs[None, :] + CMP_LEN <= seq)
    ct = overlap.astype(BF16)
    gw = group * HEAD_DIM
    return pl.pallas_call(
        functools.partial(_nsa_cmp_kernel, tq=tq, group=group, n_heads=n_heads),
        grid=(batch, Hkv, seq // tq),
        in_specs=[
            pl.BlockSpec((None, tq, gw), lambda b, g, i: (b, i, col_q + g)),
            pl.BlockSpec((None, None, n_cmp, LANES), lambda b, g, i: (b, g, 0, 0)),
            pl.BlockSpec((None, None, HEAD_DIM, n_cmp), lambda b, g, i: (b, g, 0, 0)),
            pl.BlockSpec((n_slc, n_cmp), lambda b, g, i: (0, 0)),
        ],
        out_specs=[
            pl.BlockSpec((None, tq, gw), lambda b, g, i: (b, i, g)),
            pl.BlockSpec((None, None, tq, n_slc), lambda b, g, i: (b, g, i, 0)),
            pl.BlockSpec((None, None, tq // NSA_TQ, 1, n_slc), lambda b, g, i: (b, g, i, 0, 0)),
        ],
        out_shape=[
            jax.ShapeDtypeStruct((batch, seq, n_heads * HEAD_DIM), F32),
            jax.ShapeDtypeStruct((batch, Hkv, seq, n_slc), BF16),
            jax.ShapeDtypeStruct((batch, Hkv, seq // NSA_TQ, 1, n_slc), jnp.int32),
        ],
        scratch_shapes=[pltpu.VMEM((n_slc, tq), F32)],
        compiler_params=pltpu.CompilerParams(
            dimension_semantics=("parallel", "parallel", "parallel"),
            vmem_limit_bytes=VMEM_LIMIT),
        name="nsa_cmp",
    )(act, kc, vct, ct)


def _nsa_main_kernel(used_ref, q_ref, selb_ref, kvs_ref, kvw_ref, kconst_ref, ocmp_ref, gate_ref,
                     o_ref, ks_ref, vst_ref, kw_ref, vwt_ref, accs_ref, accw_ref, gt_ref, t_ref,
                     p_ref, tiles_ref, *, tq, tk, group, n_heads):
    g = pl.program_id(1)
    qi = pl.program_id(2)
    S = kvs_ref.shape[0]
    R = group * tq
    i0 = qi * tq

    @pl.when(qi == 0)
    def _():
        def body(c, carry):
            rows = pl.ds(pl.multiple_of(c * tk, tk), tk)
            kvs = kvs_ref[rows, :]
            kvw = kvw_ref[rows, :]
            kconst = kconst_ref[...]
            ks_ref[rows, :HEAD_DIM] = kvs[:, :HEAD_DIM]
            ks_ref[rows, HEAD_DIM:LANES] = kconst[:, :HEAD_DIM]
            kw_ref[rows, :] = jnp.concatenate([kvw[:, :HEAD_DIM], kconst[:, :HEAD_DIM]], axis=1)
            vst_ref[:, rows] = kvs.astype(F32).T[HEAD_DIM:, :].astype(BF16)
            vwt_ref[:, rows] = kvw.astype(F32).T[HEAD_DIM:, :].astype(BF16)
            key_blk = (c * tk + lax.broadcasted_iota(jnp.int32, (tk, n_slc), 0)) // SLC_BLK
            onehot = key_blk == lax.broadcasted_iota(jnp.int32, (tk, n_slc), 1)
            ks_ref[rows, LANES:] = jnp.where(onehot, 1.0, 0.0).astype(BF16)
            return carry

        n_slc = ks_ref.shape[1] - LANES
        lax.fori_loop(0, S // tk, body, 0)

    qt = q_ref[...]
    selb = selb_ref[...]
    qs_parts, qw_parts, slope_parts = [], [], []
    for e in range(group):
        slope = jnp.exp2(-8.0 * (group * g + e + 1).astype(F32) / n_heads)
        aug = _aug_const(tq, [slope]).astype(BF16)
        qe = qt[:, e * HEAD_DIM:(e + 1) * HEAD_DIM]
        qw_parts.append(jnp.concatenate([qe, aug], axis=1))
        qs_parts.append(jnp.concatenate([qe, aug, selb], axis=1))
        slope_parts.append(jnp.full((1, tq), slope, F32))
    q_slc = jnp.concatenate(qs_parts, axis=0)
    q_win = jnp.concatenate(qw_parts, axis=0)
    slope_row = jnp.concatenate(slope_parts, axis=1)
    qpos = i0 + lax.broadcasted_iota(jnp.int32, (1, R), 1) % tq

    nd = i0 // tk
    rows = lambda n: pl.ds(pl.multiple_of(n * tk, tk), tk)
    shift = lambda n: slope_row * (n * tk - i0).astype(F32)
    kpos = lambda n: n * tk + lax.broadcasted_iota(jnp.int32, (tk, 1), 0)
    causal = lambda n: kpos(n) <= qpos

    def in_window(n):
        kp = kpos(n)
        dist = qpos - kp
        return (dist >= 0) & (dist < NSA_WINDOW) & (kp >= 0)

    n_tiles = S // tk
    base = ((pl.program_id(0) * pl.num_programs(1) + g) * pl.num_programs(2) + qi) * n_tiles

    def collect(n, cnt):
        tiles_ref[cnt] = n
        return cnt + jnp.where(used_ref[base + n] != 0, 1, 0)

    tiles_ref[0] = nd
    n_used = lax.fori_loop(0, nd, collect, 1)
    assert 2 * tk >= NSA_WINDOW - 1 and tk % tq == 0
    tw = []
    for c in range(3):
        n = nd - 2 + c
        t = _dot_nt(kw_ref[rows(jnp.maximum(n, 0)), :], q_win) + shift(n)
        tw.append(jnp.where(in_window(n), t, NEG))
    mw = functools.reduce(jnp.maximum, [jnp.max(t, axis=0, keepdims=True) for t in tw])
    lw = jnp.zeros((1, R), F32)
    o_w = jnp.zeros((HEAD_DIM, R), F32)
    for c in range(3):
        p = jnp.exp(tw[c] - mw)
        lw = lw + jnp.sum(p, axis=0, keepdims=True)
        o_w = o_w + _dot(vwt_ref[:, rows(jnp.maximum(nd - 2 + c, 0))], p.astype(BF16))
    accw_ref[...] = o_w

    slc = _Stream(q_slc, lambda n: ks_ref[rows(n), :], lambda n: vst_ref[:, rows(n)],
                  accs_ref, t_ref, p_ref, shift, mask=causal)
    ((_, ls),) = _flash_sweep([slc], n_used, lambda pos: tiles_ref[pos])

    gz = gate_ref[...]
    gt_ref[...] = jax.nn.sigmoid(gz).T
    oc_t = ocmp_ref[...].T
    outs = []
    for e in range(group):
        head = group * g + e
        cols = slice(e * tq, (e + 1) * tq)
        o_s = accs_ref[:, cols] / ls[:, cols]
        o_w = accw_ref[:, cols] / lw[:, cols]
        o_c = oc_t[e * HEAD_DIM:(e + 1) * HEAD_DIM, :]
        g_c = gt_ref[pl.ds(3 * head, 1), :]
        g_s = gt_ref[pl.ds(3 * head + 1, 1), :]
        g_w = gt_ref[pl.ds(3 * head + 2, 1), :]
        outs.append(g_c * o_c + g_s * o_s + g_w * o_w)
    o_ref[...] = jnp.concatenate(outs, axis=0).T


def _nsa_main(act, selb, used, ocmp, gates, batch, seq, col_q, col_kvs, col_kvw, col_gate, n_heads,
              group):
    tq, tk = NSA_TQ, NSA_TK
    Hkv = n_heads // group
    n_slc = seq // SLC_BLK
    gw = group * HEAD_DIM
    R = group * tq
    jj = jnp.arange(tk, dtype=F32)
    kconst = jnp.zeros((tk, HEAD_DIM), F32).at[:, 0].set(jj).astype(BF16)
    blk_per_tile = tk // SLC_BLK
    used_tiles = jnp.max(used.reshape(batch, Hkv, seq // tq, n_slc // blk_per_tile, blk_per_tile),
                         axis=-1).reshape(-1).astype(jnp.int32)
    grid_spec = pltpu.PrefetchScalarGridSpec(
        num_scalar_prefetch=1,
        grid=(batch, Hkv, seq // tq),
        in_specs=[
            pl.BlockSpec((None, tq, gw), lambda b, g, i, u: (b, i, col_q + g)),
            pl.BlockSpec((None, None, tq, n_slc), lambda b, g, i, u: (b, g, i, 0)),
            pl.BlockSpec((None, seq, LANES), lambda b, g, i, u: (b, 0, col_kvs + g)),
            pl.BlockSpec((None, seq, LANES), lambda b, g, i, u: (b, 0, col_kvw + g)),
            pl.BlockSpec((tk, HEAD_DIM), lambda b, g, i, u: (0, 0)),
            pl.BlockSpec((None, tq, gw), lambda b, g, i, u: (b, i, g)),
            pl.BlockSpec((None, tq, LANES), lambda b, g, i, u: (b, i, col_gate)),
        ],
        out_specs=pl.BlockSpec((None, tq, gw), lambda b, g, i, u: (b, i, g)),
        scratch_shapes=[
            pltpu.VMEM((seq, LANES + n_slc), BF16),
            pltpu.VMEM((HEAD_DIM, seq), BF16),
            pltpu.VMEM((seq, LANES), BF16),
            pltpu.VMEM((HEAD_DIM, seq), BF16),
            pltpu.VMEM((HEAD_DIM, R), F32),
            pltpu.VMEM((HEAD_DIM, R), F32),
            pltpu.VMEM((LANES, tq), F32),
            pltpu.VMEM((FLASH_UNROLL, tk, R), F32), pltpu.VMEM((FLASH_UNROLL, tk, R), BF16),
            pltpu.SMEM((seq // tk,), jnp.int32),
        ],
    )
    return pl.pallas_call(
        functools.partial(_nsa_main_kernel, tq=tq, tk=tk, group=group, n_heads=n_heads),
        grid_spec=grid_spec,
        out_shape=jax.ShapeDtypeStruct((batch, seq, n_heads * HEAD_DIM), F32),
        compiler_params=pltpu.CompilerParams(
            dimension_semantics=("parallel", "parallel", "arbitrary"),
            vmem_limit_bytes=VMEM_LIMIT),
        name="nsa_main",
    )(used_tiles, act, selb, act, act, kconst, ocmp, gates)


def _tile_gain(g, width):
    return jnp.tile(g.astype(F32), width // HEAD_DIM)


def _even_layer(x, ln, w_in, qkn, lam, subln, w_out, layer):
    B, S, D = x.shape
    M = B * S
    x2d = x.reshape(M, D)
    W = 512
    aq, ak, av, az, bq, bk, bv, bz = [w_in[:, i * W:(i + 1) * W] for i in range(8)]
    scale = HEAD_DIM ** -0.5
    ones, zeros = jnp.ones((W,), F32), jnp.zeros((W,), F32)
    w_b = jnp.concatenate([bq, bk, bv], axis=1).astype(BF16)
    gain_b = jnp.concatenate([_tile_gain(qkn[2], W) * scale, _tile_gain(qkn[3], W), ones])
    mask_b = np.repeat([1.0, 1.0, 0.0], W)
    act_b = _project(x2d, ln, w_b, gain_b, mask_b, BF16)
    w_a = jnp.concatenate([aq, ak, av, az, bz], axis=1).astype(BF16)
    gain_a = jnp.concatenate([_tile_gain(qkn[0], W) * scale, _tile_gain(qkn[1], W), ones, ones, ones])
    mask_a = np.repeat([1.0, 1.0, 0.0, 0.0, 0.0], W)
    act_a = _project(x2d, ln, w_a, gain_a, mask_a, F32)

    lam_init = 0.8 - 0.6 * math.exp(-0.3 * layer)
    lf = lam.astype(F32)
    lam_f = jnp.exp(jnp.sum(lf[0] * lf[1])) - jnp.exp(jnp.sum(lf[2] * lf[3])) + lam_init
    ob = _diff_attention(act_b.reshape(B, S, 3 * W), lam_f, subln, B, S, 0, 4, 8, 4, 1.0 - lam_init)
    oa = _dilated_attention(act_a.reshape(B, S, 5 * W), B, S, 0, 4, 8, 8)
    out = _out_project(x2d, oa.reshape(M, W), ob.reshape(M, W), act_a, 3, 4, w_out)
    return out.reshape(B, S, D)


def _odd_layer(x, ln, w_in, qkn, phi_pe, phi_w1, phi_w2, w_out):
    B, S, D = x.shape
    M = B * S
    x2d = x.reshape(M, D)
    W, KW = 512, 128
    offs = [0]
    for s in (W, W, W, W, W, KW, KW, KW, KW, KW, KW, W, 24):
        offs.append(offs[-1] + s)
    cq, ck, cv, cz, dq, dkc, dvc, dks, dvs, dkw, dvw, dz, dg = [
        w_in[:, offs[i]:offs[i + 1]] for i in range(13)]
    scale = HEAD_DIM ** -0.5
    hd = HEAD_DIM
    ones = lambda n: jnp.ones((n,), F32)
    zeros = lambda n: jnp.zeros((n,), F32)
    kv_pair = lambda k, v: jnp.concatenate([k[:, :hd], v[:, :hd], k[:, hd:], v[:, hd:]], axis=1)
    w_b = jnp.concatenate([cq, ck, cv, dq, kv_pair(dks, dvs), kv_pair(dkw, dvw), dkc, dvc,
                           jnp.zeros((D, 2 * KW), w_in.dtype)], axis=1).astype(BF16)
    kv_gain = lambda g: jnp.concatenate([g, ones(hd), g, ones(hd)])
    kv_mask = np.repeat([1.0, 0.0, 1.0, 0.0], hd)
    gain_b = jnp.concatenate([_tile_gain(qkn[0], W) * scale, _tile_gain(qkn[1], W), ones(W),
                              _tile_gain(qkn[2], W) * scale, kv_gain(qkn[4].astype(F32)),
                              kv_gain(qkn[5].astype(F32)), ones(4 * KW)])
    mask_b = np.concatenate([np.repeat([1.0, 1.0, 0.0, 1.0], W), kv_mask, kv_mask, np.zeros(4 * KW)])
    act_b = _project(x2d, ln, w_b, gain_b, mask_b, BF16).reshape(B, S, 6 * W)
    w_g = jnp.concatenate([cz, dz, dg, jnp.zeros((D, W - 24), w_in.dtype)], axis=1).astype(BF16)
    act_g = _project(x2d, ln, w_g, ones(3 * W), np.zeros(3 * W), F32)

    oc = _moba_attention(act_b, B, S, 0, 4, 8, 8, 256, 3)

    def chunks(t):
        t = t.reshape(B, S // CMP_STRIDE, CMP_STRIDE, 2, hd)
        return jnp.transpose(t, (0, 3, 1, 2, 4)).reshape(B, 2, S // CMP_STRIDE, CMP_STRIDE * hd)

    kc = _compress(chunks(act_b[:, :, 20 * KW:21 * KW]), phi_pe[0], phi_w1[0], phi_w2[0], qkn[3], True)
    vct = _compress(chunks(act_b[:, :, 21 * KW:22 * KW]), phi_pe[1], phi_w1[1], phi_w2[1], qkn[3], False)
    ocmp, selb, used = _nsa_cmp(act_b, kc, vct, B, S, 6, 8, 4)
    od = _nsa_main(act_b, selb, used, ocmp, act_g.reshape(B, S, 3 * W), B, S, 6, 16, 18, 8, 8, 4)
    out = _out_project(x2d, oc.reshape(M, W), od.reshape(M, W), act_g, 0, 1, w_out)
    return out.reshape(B, S, D)


def kernel(x, ln_e, w_in_e, qkn_e, lam_e, subln_e, w_out_e, ln_o, w_in_o, qkn_o, phi_pe, phi_w1, phi_w2, w_out_o):
    n_layers = ln_e.shape[0] + ln_o.shape[0]
    for layer in range(n_layers):
        i = layer // 2
        if layer % 2 == 0:
            x = _even_layer(x, ln_e[i], w_in_e[i], qkn_e[i], lam_e[i], subln_e[i], w_out_e[i], layer)
        else:
            x = _odd_layer(x, ln_o[i], w_in_o[i], qkn_o[i], phi_pe[i], phi_w1[i], phi_w2[i], w_out_o[i])
    return x
```

```python
import functools
import math

import jax
import jax.numpy as jnp
import numpy as np
from jax import lax
from jax.experimental import pallas as pl
from jax.experimental.pallas import tpu as pltpu

HEAD_DIM = 64
LANES = 128
RMS_EPS = 1e-6
NEG = -2e30
M_INIT = -1e30
VMEM_LIMIT = 56 * 1024 * 1024

F32 = jnp.float32
BF16 = jnp.bfloat16

_NT = (((1,), (1,)), ((), ()))


def _dot(a, b):
    return jnp.dot(a, b, preferred_element_type=F32)


def _dot_nt(a, b):
    return lax.dot_general(a, b, _NT, preferred_element_type=F32)


def _alibi_slopes(n):
    return [2.0 ** (-8.0 * (i + 1) / n) for i in range(n)]


def _split_bf16(x):
    hi = x.astype(BF16)
    lo = (x - hi.astype(F32)).astype(BF16)
    return hi, lo


NORM_CHUNK = 256


def _proj_kernel(x_ref, ln_ref, w_ref, gain_ref, nmask_ref, bd_ref, o_ref, *, norm_chunks):
    x = x_ref[...]
    ms = jnp.mean(x * x, axis=-1, keepdims=True)
    h = (x * lax.rsqrt(ms + RMS_EPS) * ln_ref[...]).astype(BF16)
    bd = bd_ref[...]
    for c, normed in enumerate(norm_chunks):
        cols = slice(c * NORM_CHUNK, (c + 1) * NORM_CHUNK)
        y = _dot(h, w_ref[:, cols])
        if normed:
            hi, lo = _split_bf16(y * y)
            ms = _dot(hi, bd) + _dot(lo, bd)
            yn = y * lax.rsqrt(ms + RMS_EPS) * gain_ref[:, cols]
            y = jnp.where(nmask_ref[:, cols] > 0.5, yn, y)
        o_ref[:, cols] = y.astype(o_ref.dtype)


def _project(x2d, ln, w, gain, nmask, out_dtype, tm=1024):
    M, D = x2d.shape
    N = w.shape[1]
    assert M % tm == 0 and N % NORM_CHUNK == 0
    norm_chunks = [bool(c) for c in np.asarray(nmask).reshape(-1, NORM_CHUNK).max(axis=1) > 0.5]
    nmask = jnp.asarray(nmask, F32)
    r = jnp.arange(NORM_CHUNK) // HEAD_DIM
    bd = jnp.where(r[:, None] == r[None, :], 1.0 / HEAD_DIM, 0.0).astype(BF16)
    return pl.pallas_call(
        functools.partial(_proj_kernel, norm_chunks=tuple(norm_chunks)),
        grid=(M // tm,),
        in_specs=[
            pl.BlockSpec((tm, D), lambda i: (i, 0)),
            pl.BlockSpec((1, D), lambda i: (0, 0)),
            pl.BlockSpec((D, N), lambda i: (0, 0)),
            pl.BlockSpec((1, N), lambda i: (0, 0)),
            pl.BlockSpec((1, N), lambda i: (0, 0)),
            pl.BlockSpec((NORM_CHUNK, NORM_CHUNK), lambda i: (0, 0)),
        ],
        out_specs=pl.BlockSpec((tm, N), lambda i: (i, 0)),
        out_shape=jax.ShapeDtypeStruct((M, N), out_dtype),
        compiler_params=pltpu.CompilerParams(
            dimension_semantics=("parallel",), vmem_limit_bytes=VMEM_LIMIT),
        name="proj",
    )(x2d, ln.reshape(1, D).astype(F32), w, gain.reshape(1, N).astype(F32),
      nmask.reshape(1, N).astype(F32), bd)


def _outproj_kernel(x_ref, oa_ref, ob_ref, za_ref, zb_ref, wa_ref, wb_ref, o_ref):
    za = za_ref[...]
    zb = zb_ref[...]
    ma = (oa_ref[...] * (za * jax.nn.sigmoid(za))).astype(BF16)
    mb = (ob_ref[...] * (zb * jax.nn.sigmoid(zb))).astype(BF16)
    o_ref[...] = x_ref[...] + _dot(ma, wa_ref[...]) + _dot(mb, wb_ref[...])


def _out_project(x2d, oa, ob, z, za_blk, zb_blk, w_out, tm=512):
    M, D = x2d.shape
    W = oa.shape[1]
    wa = w_out[:W].astype(BF16)
    wb = w_out[W:].astype(BF16)
    return pl.pallas_call(
        _outproj_kernel,
        grid=(M // tm,),
        in_specs=[
            pl.BlockSpec((tm, D), lambda i: (i, 0)),
            pl.BlockSpec((tm, W), lambda i: (i, 0)),
            pl.BlockSpec((tm, W), lambda i: (i, 0)),
            pl.BlockSpec((tm, W), lambda i: (i, za_blk)),
            pl.BlockSpec((tm, W), lambda i: (i, zb_blk)),
            pl.BlockSpec((W, D), lambda i: (0, 0)),
            pl.BlockSpec((W, D), lambda i: (0, 0)),
        ],
        out_specs=pl.BlockSpec((tm, D), lambda i: (i, 0)),
        out_shape=jax.ShapeDtypeStruct((M, D), F32),
        compiler_params=pltpu.CompilerParams(
            dimension_semantics=("parallel",), vmem_limit_bytes=VMEM_LIMIT),
        name="outproj",
    )(x2d, oa, ob, z, z, wa, wb)


class _Stream:
    def __init__(self, q, k_tile, vt_tile, acc_ref, t_ref, p_ref, shift, bias=None, mask=None,
                 mask_every_tile=False):
        self.q, self.k_tile, self.vt_tile = q, k_tile, vt_tile
        self.acc_ref, self.t_ref, self.p_ref = acc_ref, t_ref, p_ref
        self.shift, self.bias, self.mask = shift, bias, mask
        self.mask_every_tile = mask_every_tile

    def scores(self, n):
        return _dot_nt(self.k_tile(n), self.q)

    def softmax(self, t, n, m, l, pad, masked):
        if masked:
            t = jnp.where(self.mask(n), t, NEG)
        off = self.shift(n) + pad
        if self.bias is not None:
            off = off + self.bias(n)
        m_new = jnp.maximum(m, jnp.max(t, axis=0, keepdims=True) + off)
        p = jnp.exp(t - (m_new - off))
        alpha = jnp.exp(m - m_new)
        return m_new, alpha * l + jnp.sum(p, axis=0, keepdims=True), alpha, p.astype(BF16)


FLASH_UNROLL = 2
FLASH_STAGED = 1
FLASH_QK_FIRST = False


def _flash_sweep(streams, count, tile_of):
    U, D = FLASH_UNROLL, FLASH_STAGED
    R = streams[0].q.shape[0]
    last = count - 1
    W = 2 + D
    state = []
    for s in streams:
        s.acc_ref[...] = jnp.zeros_like(s.acc_ref)
        s.p_ref[...] = jnp.zeros_like(s.p_ref)
        for k in range(U):
            s.t_ref[k] = s.scores(tile_of(jnp.minimum(k, last)))
        state += [jnp.full((1, R), M_INIT, F32), jnp.zeros((1, R), F32)]
        state += [jnp.ones((1, R), F32)] * D

    def flush(n_prev, state):
        pvs = [[_dot(s.vt_tile(n_prev[d]), s.p_ref[d]) for d in range(D)] for s in streams]
        for i, s in enumerate(streams):
            acc = s.acc_ref[...]
            for d in range(D):
                acc = state[W * i + 2 + d] * acc + pvs[i][d]
            s.acc_ref[...] = acc

    def body(g, carry, first_body):
        n_prev, state = list(carry[:D]), list(carry[D:])
        base = U * g
        if FLASH_QK_FIRST:
            nxt = [[s.scores(tile_of(jnp.minimum(base + U + k, last))) for k in range(U)]
                   for s in streams]
            flush(n_prev, state)
        else:
            flush(n_prev, state)
            nxt = [[s.scores(tile_of(jnp.minimum(base + U + k, last))) for k in range(U)]
                   for s in streams]
        for k in range(U):
            n = tile_of(jnp.minimum(base + k, last))
            pad = jnp.where(base + k < count, 0.0, NEG)
            d = k - (U - D)
            for i, s in enumerate(streams):
                m, l = state[W * i:W * i + 2]
                masked = s.mask is not None and (s.mask_every_tile or (first_body and k == 0))
                m, l, a, p = s.softmax(s.t_ref[k], n, m, l, pad, masked)
                state[W * i:W * i + 2] = [m, l]
                if d < 0:
                    s.acc_ref[...] = a * s.acc_ref[...] + _dot(s.vt_tile(n), p)
                else:
                    state[W * i + 2 + d] = a
                    s.p_ref[d] = p
            if d >= 0:
                n_prev[d] = n
        for i, s in enumerate(streams):
            for k in range(U):
                s.t_ref[k] = nxt[i][k]
        return tuple(n_prev) + tuple(state)

    carry = body(0, (tile_of(0),) * D + tuple(state), True)
    carry = lax.fori_loop(1, (count + U - 1) // U, lambda g, c: body(g, c, False), carry)
    flush(carry[:D], carry[D:])
    return [(carry[D + W * i], carry[D + W * i + 1]) for i in range(len(streams))]


def _aug_const(rows, col_vals):
    lane = lax.broadcasted_iota(jnp.int32, (rows, HEAD_DIM), 1)
    out = jnp.zeros((rows, HEAD_DIM), F32)
    for c, v in enumerate(col_vals):
        out = jnp.where(lane == c, v, out)
    return out


def _diff_kernel(lam_ref, q_ref, k_ref, v_ref, subln_ref, o_ref,
                 k1_ref, k2_ref, vt_ref, acc1_ref, acc2_ref, t_ref, p_ref, *, tq, n_heads, out_scale):
    h = pl.program_id(1)
    qi = pl.program_id(2)
    S = k_ref.shape[0]
    tk = tq
    slope = jnp.exp2(-8.0 * (h + 1).astype(F32) / n_heads)

    @pl.when(qi == 0)
    def _():
        jj = lax.broadcasted_iota(jnp.int32, (tk, 1), 0).astype(F32)
        kaug = _aug_const(tk, [jj]).astype(BF16)

        def body(c, carry):
            rows = pl.ds(pl.multiple_of(c * tk, tk), tk)
            kc = k_ref[rows, :]
            k1_ref[rows, :] = jnp.concatenate([kc[:, :HEAD_DIM], kaug], axis=1)
            k2_ref[rows, :] = jnp.concatenate([kc[:, HEAD_DIM:], kaug], axis=1)
            vt_ref[:, rows] = v_ref[rows, :].astype(F32).T.astype(BF16)
            return carry

        lax.fori_loop(0, S // tk, body, 0)

    qt = q_ref[...]
    qaug = _aug_const(tq, [slope]).astype(BF16)
    q1 = jnp.concatenate([qt[:, :HEAD_DIM], qaug], axis=1)
    q2 = jnp.concatenate([qt[:, HEAD_DIM:], qaug], axis=1)

    rows = lambda n: pl.ds(pl.multiple_of(n * tk, tk), tk)
    qpos = qi * tq + lax.broadcasted_iota(jnp.int32, (1, tq), 1)
    causal = lambda n: n * tk + lax.broadcasted_iota(jnp.int32, (tk, 1), 0) <= qpos
    shift = lambda n: slope * ((n - qi) * tk).astype(F32)
    vt_tile = lambda n: vt_ref[:, rows(n)]
    streams = [
        _Stream(q1, lambda n: k1_ref[rows(n), :], vt_tile, acc1_ref, t_ref.at[0], p_ref.at[0],
                shift, mask=causal),
        _Stream(q2, lambda n: k2_ref[rows(n), :], vt_tile, acc2_ref, t_ref.at[1], p_ref.at[1],
                shift, mask=causal),
    ]
    (m1, l1), (m2, l2) = _flash_sweep(streams, qi + 1, lambda pos: jnp.where(pos == 0, qi, pos - 1))

    d = acc1_ref[...] / l1 - lam_ref[0] * (acc2_ref[...] / l2)
    r = lax.rsqrt(jnp.mean(d * d, axis=0, keepdims=True) + RMS_EPS)
    o_ref[...] = (d * r).T * (subln_ref[...] * out_scale)


def _diff_attention(qkv, lam_f, subln, batch, seq, col_q, col_k, col_v, n_heads, out_scale, tq=256):
    dv = 2 * HEAD_DIM
    kern = functools.partial(_diff_kernel, tq=tq, n_heads=n_heads, out_scale=out_scale)
    grid_spec = pltpu.PrefetchScalarGridSpec(
        num_scalar_prefetch=1,
        grid=(batch, n_heads, seq // tq),
        in_specs=[
            pl.BlockSpec((None, tq, dv), lambda b, h, i, s: (b, i, col_q + h)),
            pl.BlockSpec((None, seq, dv), lambda b, h, i, s: (b, 0, col_k + h)),
            pl.BlockSpec((None, seq, dv), lambda b, h, i, s: (b, 0, col_v + h)),
            pl.BlockSpec((1, dv), lambda b, h, i, s: (0, 0)),
        ],
        out_specs=pl.BlockSpec((None, tq, dv), lambda b, h, i, s: (b, i, h)),
        scratch_shapes=[
            pltpu.VMEM((seq, LANES), BF16), pltpu.VMEM((seq, LANES), BF16),
            pltpu.VMEM((dv, seq), BF16),
            pltpu.VMEM((dv, tq), F32), pltpu.VMEM((dv, tq), F32),
            pltpu.VMEM((2, FLASH_UNROLL, tq, tq), F32), pltpu.VMEM((2, FLASH_STAGED, tq, tq), BF16),
        ],
    )
    return pl.pallas_call(
        kern,
        grid_spec=grid_spec,
        out_shape=jax.ShapeDtypeStruct((batch, seq, n_heads * dv), F32),
        compiler_params=pltpu.CompilerParams(
            dimension_semantics=("parallel", "parallel", "arbitrary"),
            vmem_limit_bytes=VMEM_LIMIT),
        name="diff_attn",
    )(lam_f.reshape(1).astype(F32), qkv, qkv, qkv, subln.reshape(1, dv).astype(F32))


BAND = 128
DILATIONS = (1, 4, 16)


def _dilated_kernel(q_ref, kp_ref, kc_ref, vp_ref, vc_ref, o_ref, kbuf_ref, vbuf_ref, st_ref,
                    *, chunk, n_heads):
    pair = pl.program_id(1)
    ci = pl.program_id(2)
    kbuf_ref[:chunk, :] = kp_ref[...]
    kbuf_ref[chunk:, :] = kc_ref[...]
    vbuf_ref[:chunk, :] = vp_ref[...]
    vbuf_ref[chunk:, :] = vc_ref[...]

    qi = lax.broadcasted_iota(jnp.int32, (BAND, 2 * BAND), 0)
    kj = lax.broadcasted_iota(jnp.int32, (BAND, 2 * BAND), 1)
    dist = qi + BAND - kj
    in_band = (dist >= 0) & (dist <= BAND)
    dist_f = dist.astype(F32)
    slopes = [jnp.exp2(-8.0 * (2 * pair + e + 1).astype(F32) / n_heads) for e in range(2)]
    lane_q = lax.broadcasted_iota(jnp.int32, (BAND, LANES), 1)
    own_half = [lane_q < HEAD_DIM, lane_q >= HEAD_DIM]

    for pi, dil in enumerate(DILATIONS):
        span = dil * BAND
        n_blk = chunk // span

        def block(idx, carry, dil=dil, span=span, n_blk=n_blk, pi=pi):
            r = idx // n_blk
            b = idx % n_blk
            q0 = r + b * span
            qb = q_ref[pl.ds(q0, BAND, stride=dil), :]
            kb = kbuf_ref[pl.ds(chunk + q0 - span, 2 * BAND, stride=dil), :]
            vb = vbuf_ref[pl.ds(chunk + q0 - span, 2 * BAND, stride=dil), :]
            first = jnp.logical_and(ci == 0, b == 0)
            ok = in_band & (kj >= jnp.where(first, BAND, 0))
            kb16 = kb.astype(BF16)
            vb16 = vb.astype(BF16)
            packed = []
            for e in range(2):
                qe = jnp.where(own_half[e], qb, 0.0).astype(BF16)
                s = _dot_nt(qe, kb16)
                s = jnp.where(ok, s - (slopes[e] * dil) * dist_f, NEG)
                m = jnp.max(s, axis=1, keepdims=True)
                p = jnp.exp(s - m)
                l = jnp.sum(p, axis=1, keepdims=True)
                o = _dot(p.astype(BF16), vb16) / l
                packed.append(jnp.where(own_half[e], o, m + jnp.log(l)))
            for e in range(2):
                st_ref[pi, e, pl.ds(q0, BAND, stride=dil), :] = packed[e]
            return carry

        lax.fori_loop(0, dil * n_blk, block, 0, unroll=4)

    rows_per = 256
    lane = lax.broadcasted_iota(jnp.int32, (rows_per, LANES), 1)

    def merge(c, carry):
        rows = pl.ds(pl.multiple_of(c * rows_per, rows_per), rows_per)
        outs = []
        for e in range(2):
            xs = [st_ref[pi, e, rows, :] for pi in range(len(DILATIONS))]
            top = functools.reduce(jnp.maximum, xs)
            num = jnp.zeros((rows_per, LANES), F32)
            den = jnp.zeros((rows_per, LANES), F32)
            for x in xs:
                w = pltpu.roll(jnp.exp(x - top), HEAD_DIM, axis=1)
                num = num + w * x
                den = den + w
            outs.append(num / den)
        o_ref[rows, :] = jnp.where(lane < HEAD_DIM, outs[0], outs[1])
        return carry

    lax.fori_loop(0, chunk // rows_per, merge, 0)


def _dilated_attention(act, batch, seq, col_q, col_k, col_v, n_heads, chunk=2048):
    assert chunk % (max(DILATIONS) * BAND) == 0 and seq % chunk == 0
    prev = lambda c: jnp.maximum(c - 1, 0)
    return pl.pallas_call(
        functools.partial(_dilated_kernel, chunk=chunk, n_heads=n_heads),
        grid=(batch, n_heads // 2, seq // chunk),
        in_specs=[
            pl.BlockSpec((None, chunk, LANES), lambda b, p, c: (b, c, col_q + p)),
            pl.BlockSpec((None, chunk, LANES), lambda b, p, c: (b, prev(c), col_k + p)),
            pl.BlockSpec((None, chunk, LANES), lambda b, p, c: (b, c, col_k + p)),
            pl.BlockSpec((None, chunk, LANES), lambda b, p, c: (b, prev(c), col_v + p)),
            pl.BlockSpec((None, chunk, LANES), lambda b, p, c: (b, c, col_v + p)),
        ],
        out_specs=pl.BlockSpec((None, chunk, LANES), lambda b, p, c: (b, c, p)),
        out_shape=jax.ShapeDtypeStruct((batch, seq, n_heads * HEAD_DIM), F32),
        scratch_shapes=[
            pltpu.VMEM((2 * chunk, LANES), F32),
            pltpu.VMEM((2 * chunk, LANES), F32),
            pltpu.VMEM((len(DILATIONS), 2, chunk, LANES), F32),
        ],
        compiler_params=pltpu.CompilerParams(
            dimension_semantics=("parallel", "parallel", "arbitrary"),
            vmem_limit_bytes=VMEM_LIMIT),
        name="dilated_attn",
    )(act, act, act, act, act)


TAKEN = -3e38


def _topk_member(cur, k):
    n = cur.shape[0]
    row = lax.broadcasted_iota(jnp.int32, cur.shape, 0)
    sel = jnp.zeros(cur.shape, jnp.bool_)
    for _ in range(k):
        mx = jnp.max(cur, axis=0, keepdims=True)
        first = jnp.min(jnp.where(cur == mx, row, n), axis=0, keepdims=True)
        pick = row == first
        sel = jnp.logical_or(sel, pick)
        cur = jnp.where(pick, TAKEN, cur)
    return sel


def _moba_kernel(q_ref, k_ref, v_ref, o_ref, kaug_ref, vt_ref, kmean_ref, selb_ref, acc_ref,
                 t_ref, p_ref, *, blk, topk, n_heads):
    pair = pl.program_id(1)
    qi = pl.program_id(2)
    S = k_ref.shape[0]
    nblk = S // blk
    tq = tk = blk

    @pl.when(qi == 0)
    def _():
        jj = lax.broadcasted_iota(jnp.int32, (tk, 1), 0).astype(F32)
        kaug = _aug_const(tk, [jj]).astype(BF16)

        def body(c, carry):
            rows = pl.ds(pl.multiple_of(c * tk, tk), tk)
            kc = k_ref[rows, :]
            vc = v_ref[rows, :].astype(F32).T.astype(BF16)
            for e in range(2):
                kaug_ref[e, rows, :] = jnp.concatenate(
                    [kc[:, e * HEAD_DIM:(e + 1) * HEAD_DIM], kaug], axis=1)
                vt_ref[e, :, rows] = vc[e * HEAD_DIM:(e + 1) * HEAD_DIM, :]
            kmean_ref[pl.ds(c, 1), :] = jnp.mean(kc.astype(F32), axis=0, keepdims=True)
            return carry

        lax.fori_loop(0, nblk, body, 0)

    qt = q_ref[...]
    blk_row = lax.broadcasted_iota(jnp.int32, (nblk, tq), 0)
    past = blk_row < qi
    qs, slopes = [], []
    for e in range(2):
        slope = jnp.exp2(-8.0 * (2 * pair + e + 1).astype(F32) / n_heads)
        qe = qt[:, e * HEAD_DIM:(e + 1) * HEAD_DIM]
        km_hi, km_lo = _split_bf16(kmean_ref[:, e * HEAD_DIM:(e + 1) * HEAD_DIM])
        gate = _dot_nt(km_hi, qe) + _dot_nt(km_lo, qe)
        sel = _topk_member(jnp.where(past, gate, -2e38), topk)
        selb_ref[e] = jnp.where(jnp.logical_and(sel, past) | (blk_row == qi), 0.0, NEG)
        qs.append(jnp.concatenate([qe, _aug_const(tq, [slope]).astype(BF16)], axis=1))
        slopes.append(slope)

    rows = lambda n: pl.ds(pl.multiple_of(n * tk, tk), tk)
    qpos = qi * tq + lax.broadcasted_iota(jnp.int32, (1, tq), 1)
    causal = lambda n: n * tk + lax.broadcasted_iota(jnp.int32, (tk, 1), 0) <= qpos
    streams = [
        _Stream(qs[e],
                functools.partial(lambda n, e: kaug_ref[e, rows(n), :], e=e),
                functools.partial(lambda n, e: vt_ref[e, :, rows(n)], e=e),
                acc_ref.at[e], t_ref.at[e], p_ref.at[e],
                functools.partial(lambda n, e: slopes[e] * ((n - qi) * tk).astype(F32), e=e),
                bias=functools.partial(lambda n, e: selb_ref[e, pl.ds(n, 1), :], e=e),
                mask=causal)
        for e in range(2)]
    stats = _flash_sweep(streams, qi + 1, lambda pos: jnp.where(pos == 0, qi, pos - 1))
    o_ref[...] = jnp.concatenate([acc_ref[e] / stats[e][1] for e in range(2)], axis=0).T


def _moba_attention(qkv, batch, seq, col_q, col_k, col_v, n_heads, blk, topk):
    kern = functools.partial(_moba_kernel, blk=blk, topk=topk, n_heads=n_heads)
    nblk = seq // blk
    return pl.pallas_call(
        kern,
        grid=(batch, n_heads // 2, nblk),
        in_specs=[
            pl.BlockSpec((None, blk, LANES), lambda b, p, i: (b, i, col_q + p)),
            pl.BlockSpec((None, seq, LANES), lambda b, p, i: (b, 0, col_k + p)),
            pl.BlockSpec((None, seq, LANES), lambda b, p, i: (b, 0, col_v + p)),
        ],
        out_specs=pl.BlockSpec((None, blk, LANES), lambda b, p, i: (b, i, p)),
        out_shape=jax.ShapeDtypeStruct((batch, seq, n_heads * HEAD_DIM), F32),
        scratch_shapes=[
            pltpu.VMEM((2, seq, LANES), BF16),
            pltpu.VMEM((2, HEAD_DIM, seq), BF16),
            pltpu.VMEM((nblk, LANES), F32),
            pltpu.VMEM((2, nblk, blk), F32),
            pltpu.VMEM((2, HEAD_DIM, blk), F32),
            pltpu.VMEM((2, FLASH_UNROLL, blk, blk), F32),
            pltpu.VMEM((2, FLASH_STAGED, blk, blk), BF16),
        ],
        compiler_params=pltpu.CompilerParams(
            dimension_semantics=("parallel", "parallel", "arbitrary"),
            vmem_limit_bytes=VMEM_LIMIT),
        name="moba_attn",
    )(qkv, qkv, qkv)


CMP_STRIDE = 16
CMP_LEN = 32
SLC_BLK = 64
SLC_TOPK = 16
NSA_WINDOW = 512
NSA_TQ = 128
NSA_TK = 256


def _compress_kernel(c_ref, pe_ref, w1_ref, w2_ref, g_ref, o_ref, *, is_key):
    half = c_ref.shape[1]
    n = c_ref.shape[0]
    c = c_ref[...]
    w1 = w1_ref[...]
    a_lo = _dot(c, w1[:half])
    a_hi = _dot(c, w1[half:])
    pe_hi, pe_lo = _split_bf16(pe_ref[...])
    c0 = (_dot(pe_hi, w1) + _dot(pe_lo, w1))[0:1]
    y = a_lo + pltpu.roll(a_hi, n - 1, axis=0) + c0
    hid = (y * jax.nn.sigmoid(y)).astype(BF16)
    if is_key:
        z = _dot(hid, w2_ref[...])
    else:
        z = _dot(hid.astype(F32), w2_ref[...].astype(F32))
    if is_key:
        ms = jnp.sum(z * z, axis=-1, keepdims=True) * (1.0 / HEAD_DIM)
        kn = z * lax.rsqrt(ms + RMS_EPS) * g_ref[...]
        i = lax.broadcasted_iota(jnp.int32, (n, 1), 0)
        aug = _aug_const(n, [(i // 16 * 256).astype(F32), (i % 16 * 16).astype(F32), 1.0])
        o_ref[...] = jnp.concatenate([kn[:, :HEAD_DIM], aug], axis=1).astype(BF16)
    else:
        o_ref[...] = z.T[:HEAD_DIM, :].astype(BF16)


def _compress(chunks, pe, w1, w2, gain, is_key):
    B, Hkv, n, half = chunks.shape
    hidden = w1.shape[1]
    pe8 = jnp.broadcast_to(pe.reshape(1, 2 * half).astype(F32), (8, 2 * half))
    w2p = jnp.pad(w2, ((0, 0), (0, LANES - w2.shape[1]))).astype(BF16)
    g = jnp.pad(gain.astype(F32), (0, LANES - HEAD_DIM)).reshape(1, LANES)
    if is_key:
        out_shape = jax.ShapeDtypeStruct((B, Hkv, n, LANES), BF16)
        out_spec = pl.BlockSpec((None, None, n, LANES), lambda b, h: (b, h, 0, 0))
    else:
        out_shape = jax.ShapeDtypeStruct((B, Hkv, HEAD_DIM, n), BF16)
        out_spec = pl.BlockSpec((None, None, HEAD_DIM, n), lambda b, h: (b, h, 0, 0))
    return pl.pallas_call(
        functools.partial(_compress_kernel, is_key=is_key),
        grid=(B, Hkv),
        in_specs=[
            pl.BlockSpec((None, None, n, half), lambda b, h: (b, h, 0, 0)),
            pl.BlockSpec((8, 2 * half), lambda b, h: (0, 0)),
            pl.BlockSpec((2 * half, hidden), lambda b, h: (0, 0)),
            pl.BlockSpec((hidden, LANES), lambda b, h: (0, 0)),
            pl.BlockSpec((1, LANES), lambda b, h: (0, 0)),
        ],
        out_specs=out_spec,
        out_shape=out_shape,
        compiler_params=pltpu.CompilerParams(
            dimension_semantics=("parallel", "parallel"), vmem_limit_bytes=VMEM_LIMIT),
        name="nsa_compress_k" if is_key else "nsa_compress_v",
    )(chunks, pe8, w1.astype(BF16), w2p, g)


CMP_ROWS = 128


def _nsa_cmp_kernel(q_ref, kc_ref, vct_ref, ct_ref, o_ref, selb_ref, used_ref, imp_ref,
                    *, tq, group, n_heads):
    g = pl.program_id(1)
    qi = pl.program_id(2)
    n_cmp = kc_ref.shape[0]
    n_slc = ct_ref.shape[0]
    i0 = qi * tq
    qpos = i0 + lax.broadcasted_iota(jnp.int32, (1, tq), 1)
    qt = q_ref[...]

    def attend(rows):
        cend = lax.broadcasted_iota(jnp.int32, (rows, 1), 0) * CMP_STRIDE + (CMP_LEN - 1)
        valid = cend <= qpos
        kc = kc_ref[:rows, :]
        vct = vct_ref[:, :rows]
        psum = jnp.zeros((rows, tq), F32)
        outs = []
        for e in range(group):
            slope = jnp.exp2(-8.0 * (group * g + e + 1).astype(F32) / n_heads)
            aug = _aug_const(tq, [slope, slope, -slope * i0.astype(F32)]).astype(BF16)
            qa = jnp.concatenate([qt[:, e * HEAD_DIM:(e + 1) * HEAD_DIM], aug], axis=1)
            t = jnp.where(valid, _dot_nt(kc, qa), NEG)
            m = jnp.max(t, axis=0, keepdims=True)
            p = jnp.where(valid, jnp.exp(t - m), 0.0)
            l = jnp.sum(p, axis=0, keepdims=True)
            p = p / jnp.where(l > 0, l, 1.0)
            outs.append(_dot(vct, p.astype(BF16)))
            psum = psum + p
        o_ref[...] = jnp.concatenate(outs, axis=0).T
        p_hi, p_lo = _split_bf16(psum)
        ct = ct_ref[:, :rows]
        imp_ref[...] = _dot(ct, p_hi) + _dot(ct, p_lo)

    needed = (i0 + tq - CMP_LEN) // CMP_STRIDE + 1
    n_var = max(n_cmp // CMP_ROWS, 1)
    for v in range(n_var):
        rows = n_cmp if v == n_var - 1 else (v + 1) * CMP_ROWS
        cond = needed > v * CMP_ROWS
        if v < n_var - 1:
            cond = jnp.logical_and(cond, needed <= (v + 1) * CMP_ROWS)
        pl.when(cond)(functools.partial(attend, rows))

    imp = imp_ref[...]
    jb = lax.broadcasted_iota(jnp.int32, (n_slc, tq), 0)
    qb = qpos // SLC_BLK
    forced = (jb == 0) | (jb == qb) | (jb == qb - 1)
    cur = jnp.where(forced, 3e38, jnp.where(jb > qb, -2e38, imp))
    sel = jnp.logical_and(_topk_member(cur, min(SLC_TOPK, n_slc)), jb <= qb)
    selb = jnp.where(sel, 0.0, NEG).T
    selb_ref[...] = selb.astype(BF16)
    for c in range(tq // NSA_TQ):
        hit = jnp.max(selb[c * NSA_TQ:(c + 1) * NSA_TQ], axis=0, keepdims=True) == 0.0
        used_ref[c] = jnp.where(hit, 1, 0)


def _nsa_cmp(act, kc, vct, batch, seq, col_q, n_heads, group, tq=256):
    Hkv = n_heads // group
    n_cmp = kc.shape[2]
    n_slc = seq // SLC_BLK
    cs = jnp.arange(n_cmp) * CMP_STRIDE
    ss = jnp.arange(n_slc) * SLC_BLK
    overlap = (cs[None, :] <= ss[:, None] + SLC_BLK - 1) & (cs[None, :] + CMP_LEN - 1 >= ss[:, None])
    overlap = overlap & (cs[None, :] + CMP_LEN <= seq)
    ct = overlap.astype(BF16)
    gw = group * HEAD_DIM
    return pl.pallas_call(
        functools.partial(_nsa_cmp_kernel, tq=tq, group=group, n_heads=n_heads),
        grid=(batch, Hkv, seq // tq),
        in_specs=[
            pl.BlockSpec((None, tq, gw), lambda b, g, i: (b, i, col_q + g)),
            pl.BlockSpec((None, None, n_cmp, LANES), lambda b, g, i: (b, g, 0, 0)),
            pl.BlockSpec((None, None, HEAD_DIM, n_cmp), lambda b, g, i: (b, g, 0, 0)),
            pl.BlockSpec((n_slc, n_cmp), lambda b, g, i: (0, 0)),
        ],
        out_specs=[
            pl.BlockSpec((None, tq, gw), lambda b, g, i: (b, i, g)),
            pl.BlockSpec((None, None, tq, n_slc), lambda b, g, i: (b, g, i, 0)),
            pl.BlockSpec((None, None, tq // NSA_TQ, 1, n_slc), lambda b, g, i: (b, g, i, 0, 0)),
        ],
        out_shape=[
            jax.ShapeDtypeStruct((batch, seq, n_heads * HEAD_DIM), F32),
            jax.ShapeDtypeStruct((batch, Hkv, seq, n_slc), BF16),
            jax.ShapeDtypeStruct((batch, Hkv, seq // NSA_TQ, 1, n_slc), jnp.int32),
        ],
        scratch_shapes=[pltpu.VMEM((n_slc, tq), F32)],
        compiler_params=pltpu.CompilerParams(
            dimension_semantics=("parallel", "parallel", "parallel"),
            vmem_limit_bytes=VMEM_LIMIT),
        name="nsa_cmp",
    )(act, kc, vct, ct)


def _nsa_main_kernel(used_ref, q_ref, selb_ref, kvs_ref, kvw_ref, kconst_ref, ocmp_ref, gate_ref,
                     o_ref, ks_ref, vst_ref, kw_ref, vwt_ref, accs_ref, accw_ref, gt_ref, t_ref,
                     p_ref, tiles_ref, *, tq, tk, group, n_heads):
    g = pl.program_id(1)
    qi = pl.program_id(2)
    S = kvs_ref.shape[0]
    R = group * tq
    i0 = qi * tq

    @pl.when(qi == 0)
    def _():
        def body(c, carry):
            rows = pl.ds(pl.multiple_of(c * tk, tk), tk)
            kvs = kvs_ref[rows, :]
            kvw = kvw_ref[rows, :]
            kconst = kconst_ref[...]
            ks_ref[rows, :HEAD_DIM] = kvs[:, :HEAD_DIM]
            ks_ref[rows, HEAD_DIM:LANES] = kconst[:, :HEAD_DIM]
            kw_ref[rows, :] = jnp.concatenate([kvw[:, :HEAD_DIM], kconst[:, :HEAD_DIM]], axis=1)
            vst_ref[:, rows] = kvs.astype(F32).T[HEAD_DIM:, :].astype(BF16)
            vwt_ref[:, rows] = kvw.astype(F32).T[HEAD_DIM:, :].astype(BF16)
            key_blk = (c * tk + lax.broadcasted_iota(jnp.int32, (tk, n_slc), 0)) // SLC_BLK
            onehot = key_blk == lax.broadcasted_iota(jnp.int32, (tk, n_slc), 1)
            ks_ref[rows, LANES:] = jnp.where(onehot, 1.0, 0.0).astype(BF16)
            return carry

        n_slc = ks_ref.shape[1] - LANES
        lax.fori_loop(0, S // tk, body, 0)

    qt = q_ref[...]
    selb = selb_ref[...]
    qs_parts, qw_parts, slope_parts = [], [], []
    for e in range(group):
        slope = jnp.exp2(-8.0 * (group * g + e + 1).astype(F32) / n_heads)
        aug = _aug_const(tq, [slope]).astype(BF16)
        qe = qt[:, e * HEAD_DIM:(e + 1) * HEAD_DIM]
        qw_parts.append(jnp.concatenate([qe, aug], axis=1))
        qs_parts.append(jnp.concatenate([qe, aug, selb], axis=1))
        slope_parts.append(jnp.full((1, tq), slope, F32))
    q_slc = jnp.concatenate(qs_parts, axis=0)
    q_win = jnp.concatenate(qw_parts, axis=0)
    slope_row = jnp.concatenate(slope_parts, axis=1)
    qpos = i0 + lax.broadcasted_iota(jnp.int32, (1, R), 1) % tq

    nd = i0 // tk
    rows = lambda n: pl.ds(pl.multiple_of(n * tk, tk), tk)
    shift = lambda n: slope_row * (n * tk - i0).astype(F32)
    kpos = lambda n: n * tk + lax.broadcasted_iota(jnp.int32, (tk, 1), 0)
    causal = lambda n: kpos(n) <= qpos

    def in_window(n):
        kp = kpos(n)
        dist = qpos - kp
        return (dist >= 0) & (dist < NSA_WINDOW) & (kp >= 0)

    n_tiles = S // tk
    base = ((pl.program_id(0) * pl.num_programs(1) + g) * pl.num_programs(2) + qi) * n_tiles

    def collect(n, cnt):
        tiles_ref[cnt] = n
        return cnt + jnp.where(used_ref[base + n] != 0, 1, 0)

    tiles_ref[0] = nd
    n_used = lax.fori_loop(0, nd, collect, 1)
    assert 2 * tk >= NSA_WINDOW - 1 and tk % tq == 0
    tw = []
    for c in range(3):
        n = nd - 2 + c
        t = _dot_nt(kw_ref[rows(jnp.maximum(n, 0)), :], q_win) + shift(n)
        tw.append(jnp.where(in_window(n), t, NEG))
    mw = functools.reduce(jnp.maximum, [jnp.max(t, axis=0, keepdims=True) for t in tw])
    lw = jnp.zeros((1, R), F32)
    o_w = jnp.zeros((HEAD_DIM, R), F32)
    for c in range(3):
        p = jnp.exp(tw[c] - mw)
        lw = lw + jnp.sum(p, axis=0, keepdims=True)
        o_w = o_w + _dot(vwt_ref[:, rows(jnp.maximum(nd - 2 + c, 0))], p.astype(BF16))
    accw_ref[...] = o_w

    slc = _Stream(q_slc, lambda n: ks_ref[rows(n), :], lambda n: vst_ref[:, rows(n)],
                  accs_ref, t_ref, p_ref, shift, mask=causal)
    ((_, ls),) = _flash_sweep([slc], n_used, lambda pos: tiles_ref[pos])

    gz = gate_ref[...]
    gt_ref[...] = jax.nn.sigmoid(gz).T
    oc_t = ocmp_ref[...].T
    outs = []
    for e in range(group):
        head = group * g + e
        cols = slice(e * tq, (e + 1) * tq)
        o_s = accs_ref[:, cols] / ls[:, cols]
        o_w = accw_ref[:, cols] / lw[:, cols]
        o_c = oc_t[e * HEAD_DIM:(e + 1) * HEAD_DIM, :]
        g_c = gt_ref[pl.ds(3 * head, 1), :]
        g_s = gt_ref[pl.ds(3 * head + 1, 1), :]
        g_w = gt_ref[pl.ds(3 * head + 2, 1), :]
        outs.append(g_c * o_c + g_s * o_s + g_w * o_w)
    o_ref[...] = jnp.concatenate(outs, axis=0).T


def _nsa_main(act, selb, used, ocmp, gates, batch, seq, col_q, col_kvs, col_kvw, col_gate, n_heads,
              group):
    tq, tk = NSA_TQ, NSA_TK
    Hkv = n_heads // group
    n_slc = seq // SLC_BLK
    gw = group * HEAD_DIM
    R = group * tq
    jj = jnp.arange(tk, dtype=F32)
    kconst = jnp.zeros((tk, HEAD_DIM), F32).at[:, 0].set(jj).astype(BF16)
    blk_per_tile = tk // SLC_BLK
    used_tiles = jnp.max(used.reshape(batch, Hkv, seq // tq, n_slc // blk_per_tile, blk_per_tile),
                         axis=-1).reshape(-1).astype(jnp.int32)
    grid_spec = pltpu.PrefetchScalarGridSpec(
        num_scalar_prefetch=1,
        grid=(batch, Hkv, seq // tq),
        in_specs=[
            pl.BlockSpec((None, tq, gw), lambda b, g, i, u: (b, i, col_q + g)),
            pl.BlockSpec((None, None, tq, n_slc), lambda b, g, i, u: (b, g, i, 0)),
            pl.BlockSpec((None, seq, LANES), lambda b, g, i, u: (b, 0, col_kvs + g)),
            pl.BlockSpec((None, seq, LANES), lambda b, g, i, u: (b, 0, col_kvw + g)),
            pl.BlockSpec((tk, HEAD_DIM), lambda b, g, i, u: (0, 0)),
            pl.BlockSpec((None, tq, gw), lambda b, g, i, u: (b, i, g)),
            pl.BlockSpec((None, tq, LANES), lambda b, g, i, u: (b, i, col_gate)),
        ],
        out_specs=pl.BlockSpec((None, tq, gw), lambda b, g, i, u: (b, i, g)),
        scratch_shapes=[
            pltpu.VMEM((seq, LANES + n_slc), BF16),
            pltpu.VMEM((HEAD_DIM, seq), BF16),
            pltpu.VMEM((seq, LANES), BF16),
            pltpu.VMEM((HEAD_DIM, seq), BF16),
            pltpu.VMEM((HEAD_DIM, R), F32),
            pltpu.VMEM((HEAD_DIM, R), F32),
            pltpu.VMEM((LANES, tq), F32),
            pltpu.VMEM((FLASH_UNROLL, tk, R), F32), pltpu.VMEM((FLASH_STAGED, tk, R), BF16),
            pltpu.SMEM((seq // tk,), jnp.int32),
        ],
    )
    return pl.pallas_call(
        functools.partial(_nsa_main_kernel, tq=tq, tk=tk, group=group, n_heads=n_heads),
        grid_spec=grid_spec,
        out_shape=jax.ShapeDtypeStruct((batch, seq, n_heads * HEAD_DIM), F32),
        compiler_params=pltpu.CompilerParams(
            dimension_semantics=("parallel", "parallel", "arbitrary"),
            vmem_limit_bytes=VMEM_LIMIT),
        name="nsa_main",
    )(used_tiles, act, selb, act, act, kconst, ocmp, gates)


def _tile_gain(g, width):
    return jnp.tile(g.astype(F32), width // HEAD_DIM)


def _even_layer(x, ln, w_in, qkn, lam, subln, w_out, layer):
    B, S, D = x.shape
    M = B * S
    x2d = x.reshape(M, D)
    W = 512
    aq, ak, av, az, bq, bk, bv, bz = [w_in[:, i * W:(i + 1) * W] for i in range(8)]
    scale = HEAD_DIM ** -0.5
    ones, zeros = jnp.ones((W,), F32), jnp.zeros((W,), F32)
    w_b = jnp.concatenate([bq, bk, bv], axis=1).astype(BF16)
    gain_b = jnp.concatenate([_tile_gain(qkn[2], W) * scale, _tile_gain(qkn[3], W), ones])
    mask_b = np.repeat([1.0, 1.0, 0.0], W)
    act_b = _project(x2d, ln, w_b, gain_b, mask_b, BF16)
    w_a = jnp.concatenate([aq, ak, av, az, bz], axis=1).astype(BF16)
    gain_a = jnp.concatenate([_tile_gain(qkn[0], W) * scale, _tile_gain(qkn[1], W), ones, ones, ones])
    mask_a = np.repeat([1.0, 1.0, 0.0, 0.0, 0.0], W)
    act_a = _project(x2d, ln, w_a, gain_a, mask_a, F32)

    lam_init = 0.8 - 0.6 * math.exp(-0.3 * layer)
    lf = lam.astype(F32)
    lam_f = jnp.exp(jnp.sum(lf[0] * lf[1])) - jnp.exp(jnp.sum(lf[2] * lf[3])) + lam_init
    ob = _diff_attention(act_b.reshape(B, S, 3 * W), lam_f, subln, B, S, 0, 4, 8, 4, 1.0 - lam_init)
    oa = _dilated_attention(act_a.reshape(B, S, 5 * W), B, S, 0, 4, 8, 8)
    out = _out_project(x2d, oa.reshape(M, W), ob.reshape(M, W), act_a, 3, 4, w_out)
    return out.reshape(B, S, D)


def _odd_layer(x, ln, w_in, qkn, phi_pe, phi_w1, phi_w2, w_out):
    B, S, D = x.shape
    M = B * S
    x2d = x.reshape(M, D)
    W, KW = 512, 128
    offs = [0]
    for s in (W, W, W, W, W, KW, KW, KW, KW, KW, KW, W, 24):
        offs.append(offs[-1] + s)
    cq, ck, cv, cz, dq, dkc, dvc, dks, dvs, dkw, dvw, dz, dg = [
        w_in[:, offs[i]:offs[i + 1]] for i in range(13)]
    scale = HEAD_DIM ** -0.5
    hd = HEAD_DIM
    ones = lambda n: jnp.ones((n,), F32)
    zeros = lambda n: jnp.zeros((n,), F32)
    kv_pair = lambda k, v: jnp.concatenate([k[:, :hd], v[:, :hd], k[:, hd:], v[:, hd:]], axis=1)
    w_b = jnp.concatenate([cq, ck, cv, dq, kv_pair(dks, dvs), kv_pair(dkw, dvw), dkc, dvc,
                           jnp.zeros((D, 2 * KW), w_in.dtype)], axis=1).astype(BF16)
    kv_gain = lambda g: jnp.concatenate([g, ones(hd), g, ones(hd)])
    kv_mask = np.repeat([1.0, 0.0, 1.0, 0.0], hd)
    gain_b = jnp.concatenate([_tile_gain(qkn[0], W) * scale, _tile_gain(qkn[1], W), ones(W),
                              _tile_gain(qkn[2], W) * scale, kv_gain(qkn[4].astype(F32)),
                              kv_gain(qkn[5].astype(F32)), ones(4 * KW)])
    mask_b = np.concatenate([np.repeat([1.0, 1.0, 0.0, 1.0], W), kv_mask, kv_mask, np.zeros(4 * KW)])
    act_b = _project(x2d, ln, w_b, gain_b, mask_b, BF16).reshape(B, S, 6 * W)
    w_g = jnp.concatenate([cz, dz, dg, jnp.zeros((D, W - 24), w_in.dtype)], axis=1).astype(BF16)
    act_g = _project(x2d, ln, w_g, ones(3 * W), np.zeros(3 * W), F32)

    oc = _moba_attention(act_b, B, S, 0, 4, 8, 8, 256, 3)

    def chunks(t):
        t = t.reshape(B, S // CMP_STRIDE, CMP_STRIDE, 2, hd)
        return jnp.transpose(t, (0, 3, 1, 2, 4)).reshape(B, 2, S // CMP_STRIDE, CMP_STRIDE * hd)

    kc = _compress(chunks(act_b[:, :, 20 * KW:21 * KW]), phi_pe[0], phi_w1[0], phi_w2[0], qkn[3], True)
    vct = _compress(chunks(act_b[:, :, 21 * KW:22 * KW]), phi_pe[1], phi_w1[1], phi_w2[1], qkn[3], False)
    ocmp, selb, used = _nsa_cmp(act_b, kc, vct, B, S, 6, 8, 4)
    od = _nsa_main(act_b, selb, used, ocmp, act_g.reshape(B, S, 3 * W), B, S, 6, 16, 18, 8, 8, 4)
    out = _out_project(x2d, oc.reshape(M, W), od.reshape(M, W), act_g, 0, 1, w_out)
    return out.reshape(B, S, D)


def kernel(x, ln_e, w_in_e, qkn_e, lam_e, subln_e, w_out_e, ln_o, w_in_o, qkn_o, phi_pe, phi_w1, phi_w2, w_out_o):
    n_layers = ln_e.shape[0] + ln_o.shape[0]
    for layer in range(n_layers):
        i = layer // 2
        if layer % 2 == 0:
            x = _even_layer(x, ln_e[i], w_in_e[i], qkn_e[i], lam_e[i], subln_e[i], w_out_e[i], layer)
        else:
            x = _odd_layer(x, ln_o[i], w_in_o[i], qkn_o[i], phi_pe[i], phi_w1[i], phi_w2[i], w_out_o[i])
    return x
```

```python
import functools
import math

import jax
import jax.numpy as jnp
import numpy as np
from jax import lax
from jax.experimental import pallas as pl
from jax.experimental.pallas import tpu as pltpu

HEAD_DIM = 64
LANES = 128
RMS_EPS = 1e-6
NEG = -2e30
M_INIT = -1e30
VMEM_LIMIT = 56 * 1024 * 1024

F32 = jnp.float32
BF16 = jnp.bfloat16

_NT = (((1,), (1,)), ((), ()))


def _dot(a, b):
    return jnp.dot(a, b, preferred_element_type=F32)


def _dot_nt(a, b):
    return lax.dot_general(a, b, _NT, preferred_element_type=F32)


def _alibi_slopes(n):
    return [2.0 ** (-8.0 * (i + 1) / n) for i in range(n)]


def _split_bf16(x):
    hi = x.astype(BF16)
    lo = (x - hi.astype(F32)).astype(BF16)
    return hi, lo


NORM_CHUNK = 256


def _proj_kernel(x_ref, ln_ref, w_ref, gain_ref, nmask_ref, bd_ref, o_ref, *, norm_chunks):
    x = x_ref[...]
    ms = jnp.mean(x * x, axis=-1, keepdims=True)
    h = (x * lax.rsqrt(ms + RMS_EPS) * ln_ref[...]).astype(BF16)
    bd = bd_ref[...]
    col = lambda c: slice(c * NORM_CHUNK, (c + 1) * NORM_CHUNK)
    ys = {}
    for c, normed in enumerate(norm_chunks):
        y = _dot(h, w_ref[:, col(c)])
        if normed:
            ys[c] = y
        else:
            o_ref[:, col(c)] = y.astype(o_ref.dtype)
    for c, y in ys.items():
        hi, lo = _split_bf16(y * y)
        ms = _dot(hi, bd) + _dot(lo, bd)
        yn = y * lax.rsqrt(ms + RMS_EPS) * gain_ref[:, col(c)]
        o_ref[:, col(c)] = jnp.where(nmask_ref[:, col(c)] > 0.5, yn, y).astype(o_ref.dtype)


def _project(x2d, ln, w, gain, nmask, out_dtype, tm=1024):
    M, D = x2d.shape
    N = w.shape[1]
    assert M % tm == 0 and N % NORM_CHUNK == 0
    norm_chunks = [bool(c) for c in np.asarray(nmask).reshape(-1, NORM_CHUNK).max(axis=1) > 0.5]
    nmask = jnp.asarray(nmask, F32)
    r = jnp.arange(NORM_CHUNK) // HEAD_DIM
    bd = jnp.where(r[:, None] == r[None, :], 1.0 / HEAD_DIM, 0.0).astype(BF16)
    return pl.pallas_call(
        functools.partial(_proj_kernel, norm_chunks=tuple(norm_chunks)),
        grid=(M // tm,),
        in_specs=[
            pl.BlockSpec((tm, D), lambda i: (i, 0)),
            pl.BlockSpec((1, D), lambda i: (0, 0)),
            pl.BlockSpec((D, N), lambda i: (0, 0)),
            pl.BlockSpec((1, N), lambda i: (0, 0)),
            pl.BlockSpec((1, N), lambda i: (0, 0)),
            pl.BlockSpec((NORM_CHUNK, NORM_CHUNK), lambda i: (0, 0)),
        ],
        out_specs=pl.BlockSpec((tm, N), lambda i: (i, 0)),
        out_shape=jax.ShapeDtypeStruct((M, N), out_dtype),
        compiler_params=pltpu.CompilerParams(
            dimension_semantics=("parallel",), vmem_limit_bytes=VMEM_LIMIT),
        name="proj",
    )(x2d, ln.reshape(1, D).astype(F32), w, gain.reshape(1, N).astype(F32),
      nmask.reshape(1, N).astype(F32), bd)


def _outproj_kernel(x_ref, oa_ref, ob_ref, za_ref, zb_ref, wa_ref, wb_ref, o_ref):
    za = za_ref[...]
    zb = zb_ref[...]
    ma = (oa_ref[...] * (za * jax.nn.sigmoid(za))).astype(BF16)
    mb = (ob_ref[...] * (zb * jax.nn.sigmoid(zb))).astype(BF16)
    o_ref[...] = x_ref[...] + _dot(ma, wa_ref[...]) + _dot(mb, wb_ref[...])


def _out_project(x2d, oa, ob, z, za_blk, zb_blk, w_out, tm=512):
    M, D = x2d.shape
    W = oa.shape[1]
    wa = w_out[:W].astype(BF16)
    wb = w_out[W:].astype(BF16)
    return pl.pallas_call(
        _outproj_kernel,
        grid=(M // tm,),
        in_specs=[
            pl.BlockSpec((tm, D), lambda i: (i, 0)),
            pl.BlockSpec((tm, W), lambda i: (i, 0)),
            pl.BlockSpec((tm, W), lambda i: (i, 0)),
            pl.BlockSpec((tm, W), lambda i: (i, za_blk)),
            pl.BlockSpec((tm, W), lambda i: (i, zb_blk)),
            pl.BlockSpec((W, D), lambda i: (0, 0)),
            pl.BlockSpec((W, D), lambda i: (0, 0)),
        ],
        out_specs=pl.BlockSpec((tm, D), lambda i: (i, 0)),
        out_shape=jax.ShapeDtypeStruct((M, D), F32),
        compiler_params=pltpu.CompilerParams(
            dimension_semantics=("parallel",), vmem_limit_bytes=VMEM_LIMIT),
        name="outproj",
    )(x2d, oa, ob, z, z, wa, wb)


class _Stream:
    def __init__(self, q, k_tile, vt_tile, acc_ref, t_ref, p_ref, shift, bias=None, mask=None,
                 mask_every_tile=False):
        self.q, self.k_tile, self.vt_tile = q, k_tile, vt_tile
        self.acc_ref, self.t_ref, self.p_ref = acc_ref, t_ref, p_ref
        self.shift, self.bias, self.mask = shift, bias, mask
        self.mask_every_tile = mask_every_tile

    def scores(self, n):
        return _dot_nt(self.k_tile(n), self.q)

    def softmax(self, t, n, m, l, pad, masked):
        if masked:
            t = jnp.where(self.mask(n), t, NEG)
        off = self.shift(n) + pad
        if self.bias is not None:
            off = off + self.bias(n)
        m_new = jnp.maximum(m, jnp.max(t, axis=0, keepdims=True) + off)
        p = jnp.exp(t - (m_new - off))
        alpha = jnp.exp(m - m_new)
        return m_new, alpha * l + jnp.sum(p, axis=0, keepdims=True), alpha, p.astype(BF16)


FLASH_UNROLL = 2
FLASH_STAGED = 1
FLASH_QK_FIRST = False


def _flash_sweep(streams, count, tile_of):
    U, D = FLASH_UNROLL, FLASH_STAGED
    R = streams[0].q.shape[0]
    last = count - 1
    W = 2 + D
    state = []
    for s in streams:
        s.acc_ref[...] = jnp.zeros_like(s.acc_ref)
        s.p_ref[...] = jnp.zeros_like(s.p_ref)
        for k in range(U):
            s.t_ref[k] = s.scores(tile_of(jnp.minimum(k, last)))
        state += [jnp.full((1, R), M_INIT, F32), jnp.zeros((1, R), F32)]
        state += [jnp.ones((1, R), F32)] * D

    def flush(n_prev, state):
        pvs = [[_dot(s.vt_tile(n_prev[d]), s.p_ref[d]) for d in range(D)] for s in streams]
        for i, s in enumerate(streams):
            acc = s.acc_ref[...]
            for d in range(D):
                acc = state[W * i + 2 + d] * acc + pvs[i][d]
            s.acc_ref[...] = acc

    def body(g, carry, first_body):
        n_prev, state = list(carry[:D]), list(carry[D:])
        base = U * g
        if FLASH_QK_FIRST:
            nxt = [[s.scores(tile_of(jnp.minimum(base + U + k, last))) for k in range(U)]
                   for s in streams]
            flush(n_prev, state)
        else:
            flush(n_prev, state)
            nxt = [[s.scores(tile_of(jnp.minimum(base + U + k, last))) for k in range(U)]
                   for s in streams]
        for k in range(U):
            n = tile_of(jnp.minimum(base + k, last))
            pad = jnp.where(base + k < count, 0.0, NEG)
            d = k - (U - D)
            for i, s in enumerate(streams):
                m, l = state[W * i:W * i + 2]
                masked = s.mask is not None and (s.mask_every_tile or (first_body and k == 0))
                m, l, a, p = s.softmax(s.t_ref[k], n, m, l, pad, masked)
                state[W * i:W * i + 2] = [m, l]
                if d < 0:
                    s.acc_ref[...] = a * s.acc_ref[...] + _dot(s.vt_tile(n), p)
                else:
                    state[W * i + 2 + d] = a
                    s.p_ref[d] = p
            if d >= 0:
                n_prev[d] = n
        for i, s in enumerate(streams):
            for k in range(U):
                s.t_ref[k] = nxt[i][k]
        return tuple(n_prev) + tuple(state)

    carry = body(0, (tile_of(0),) * D + tuple(state), True)
    carry = lax.fori_loop(1, (count + U - 1) // U, lambda g, c: body(g, c, False), carry)
    flush(carry[:D], carry[D:])
    return [(carry[D + W * i], carry[D + W * i + 1]) for i in range(len(streams))]


def _aug_const(rows, col_vals):
    lane = lax.broadcasted_iota(jnp.int32, (rows, HEAD_DIM), 1)
    out = jnp.zeros((rows, HEAD_DIM), F32)
    for c, v in enumerate(col_vals):
        out = jnp.where(lane == c, v, out)
    return out


def _diff_kernel(lam_ref, q_ref, k_ref, v_ref, subln_ref, o_ref,
                 k1_ref, k2_ref, vt_ref, acc1_ref, acc2_ref, t_ref, p_ref, *, tq, n_heads, out_scale):
    h = pl.program_id(1)
    qi = pl.program_id(2)
    S = k_ref.shape[0]
    tk = tq
    slope = jnp.exp2(-8.0 * (h + 1).astype(F32) / n_heads)

    @pl.when(qi == 0)
    def _():
        jj = lax.broadcasted_iota(jnp.int32, (tk, 1), 0).astype(F32)
        kaug = _aug_const(tk, [jj]).astype(BF16)

        def body(c, carry):
            rows = pl.ds(pl.multiple_of(c * tk, tk), tk)
            kc = k_ref[rows, :]
            k1_ref[rows, :] = jnp.concatenate([kc[:, :HEAD_DIM], kaug], axis=1)
            k2_ref[rows, :] = jnp.concatenate([kc[:, HEAD_DIM:], kaug], axis=1)
            vt_ref[:, rows] = v_ref[rows, :].astype(F32).T.astype(BF16)
            return carry

        lax.fori_loop(0, S // tk, body, 0)

    qt = q_ref[...]
    qaug = _aug_const(tq, [slope]).astype(BF16)
    q1 = jnp.concatenate([qt[:, :HEAD_DIM], qaug], axis=1)
    q2 = jnp.concatenate([qt[:, HEAD_DIM:], qaug], axis=1)

    rows = lambda n: pl.ds(pl.multiple_of(n * tk, tk), tk)
    qpos = qi * tq + lax.broadcasted_iota(jnp.int32, (1, tq), 1)
    causal = lambda n: n * tk + lax.broadcasted_iota(jnp.int32, (tk, 1), 0) <= qpos
    shift = lambda n: slope * ((n - qi) * tk).astype(F32)
    vt_tile = lambda n: vt_ref[:, rows(n)]
    streams = [
        _Stream(q1, lambda n: k1_ref[rows(n), :], vt_tile, acc1_ref, t_ref.at[0], p_ref.at[0],
                shift, mask=causal),
        _Stream(q2, lambda n: k2_ref[rows(n), :], vt_tile, acc2_ref, t_ref.at[1], p_ref.at[1],
                shift, mask=causal),
    ]
    (m1, l1), (m2, l2) = _flash_sweep(streams, qi + 1, lambda pos: jnp.where(pos == 0, qi, pos - 1))

    d = acc1_ref[...] / l1 - lam_ref[0] * (acc2_ref[...] / l2)
    r = lax.rsqrt(jnp.mean(d * d, axis=0, keepdims=True) + RMS_EPS)
    o_ref[...] = (d * r).T * (subln_ref[...] * out_scale)


def _diff_attention(qkv, lam_f, subln, batch, seq, col_q, col_k, col_v, n_heads, out_scale, tq=256):
    dv = 2 * HEAD_DIM
    kern = functools.partial(_diff_kernel, tq=tq, n_heads=n_heads, out_scale=out_scale)
    grid_spec = pltpu.PrefetchScalarGridSpec(
        num_scalar_prefetch=1,
        grid=(batch, n_heads, seq // tq),
        in_specs=[
            pl.BlockSpec((None, tq, dv), lambda b, h, i, s: (b, i, col_q + h)),
            pl.BlockSpec((None, seq, dv), lambda b, h, i, s: (b, 0, col_k + h)),
            pl.BlockSpec((None, seq, dv), lambda b, h, i, s: (b, 0, col_v + h)),
            pl.BlockSpec((1, dv), lambda b, h, i, s: (0, 0)),
        ],
        out_specs=pl.BlockSpec((None, tq, dv), lambda b, h, i, s: (b, i, h)),
        scratch_shapes=[
            pltpu.VMEM((seq, LANES), BF16), pltpu.VMEM((seq, LANES), BF16),
            pltpu.VMEM((dv, seq), BF16),
            pltpu.VMEM((dv, tq), F32), pltpu.VMEM((dv, tq), F32),
            pltpu.VMEM((2, FLASH_UNROLL, tq, tq), F32), pltpu.VMEM((2, FLASH_STAGED, tq, tq), BF16),
        ],
    )
    return pl.pallas_call(
        kern,
        grid_spec=grid_spec,
        out_shape=jax.ShapeDtypeStruct((batch, seq, n_heads * dv), F32),
        compiler_params=pltpu.CompilerParams(
            dimension_semantics=("parallel", "parallel", "arbitrary"),
            vmem_limit_bytes=VMEM_LIMIT),
        name="diff_attn",
    )(lam_f.reshape(1).astype(F32), qkv, qkv, qkv, subln.reshape(1, dv).astype(F32))


BAND = 128
DILATIONS = (1, 4, 16)


def _dilated_kernel(q_ref, kp_ref, kc_ref, vp_ref, vc_ref, o_ref, kbuf_ref, vbuf_ref, st_ref,
                    *, chunk, n_heads):
    pair = pl.program_id(1)
    ci = pl.program_id(2)
    kbuf_ref[:chunk, :] = kp_ref[...]
    kbuf_ref[chunk:, :] = kc_ref[...]
    vbuf_ref[:chunk, :] = vp_ref[...]
    vbuf_ref[chunk:, :] = vc_ref[...]

    qi = lax.broadcasted_iota(jnp.int32, (BAND, 2 * BAND), 0)
    kj = lax.broadcasted_iota(jnp.int32, (BAND, 2 * BAND), 1)
    dist = qi + BAND - kj
    in_band = (dist >= 0) & (dist <= BAND)
    dist_f = dist.astype(F32)
    slopes = [jnp.exp2(-8.0 * (2 * pair + e + 1).astype(F32) / n_heads) for e in range(2)]
    lane_q = lax.broadcasted_iota(jnp.int32, (BAND, LANES), 1)
    own_half = [lane_q < HEAD_DIM, lane_q >= HEAD_DIM]

    for pi, dil in enumerate(DILATIONS):
        span = dil * BAND
        n_blk = chunk // span

        def block(idx, carry, dil=dil, span=span, n_blk=n_blk, pi=pi):
            r = idx // n_blk
            b = idx % n_blk
            q0 = r + b * span
            qb = q_ref[pl.ds(q0, BAND, stride=dil), :]
            kb = kbuf_ref[pl.ds(chunk + q0 - span, 2 * BAND, stride=dil), :]
            vb = vbuf_ref[pl.ds(chunk + q0 - span, 2 * BAND, stride=dil), :]
            first = jnp.logical_and(ci == 0, b == 0)
            ok = in_band & (kj >= jnp.where(first, BAND, 0))
            kb16 = kb.astype(BF16)
            vb16 = vb.astype(BF16)
            packed = []
            for e in range(2):
                qe = jnp.where(own_half[e], qb, 0.0).astype(BF16)
                s = _dot_nt(qe, kb16)
                s = jnp.where(ok, s - (slopes[e] * dil) * dist_f, NEG)
                m = jnp.max(s, axis=1, keepdims=True)
                p = jnp.exp(s - m)
                l = jnp.sum(p, axis=1, keepdims=True)
                o = _dot(p.astype(BF16), vb16) / l
                packed.append(jnp.where(own_half[e], o, m + jnp.log(l)))
            for e in range(2):
                st_ref[pi, e, pl.ds(q0, BAND, stride=dil), :] = packed[e]
            return carry

        lax.fori_loop(0, dil * n_blk, block, 0, unroll=8)

    rows_per = 256
    lane = lax.broadcasted_iota(jnp.int32, (rows_per, LANES), 1)

    def merge(c, carry):
        rows = pl.ds(pl.multiple_of(c * rows_per, rows_per), rows_per)
        outs = []
        for e in range(2):
            xs = [st_ref[pi, e, rows, :] for pi in range(len(DILATIONS))]
            top = functools.reduce(jnp.maximum, xs)
            num = jnp.zeros((rows_per, LANES), F32)
            den = jnp.zeros((rows_per, LANES), F32)
            for x in xs:
                w = pltpu.roll(jnp.exp(x - top), HEAD_DIM, axis=1)
                num = num + w * x
                den = den + w
            outs.append(num / den)
        o_ref[rows, :] = jnp.where(lane < HEAD_DIM, outs[0], outs[1])
        return carry

    lax.fori_loop(0, chunk // rows_per, merge, 0)


def _dilated_attention(act, batch, seq, col_q, col_k, col_v, n_heads, chunk=2048):
    assert chunk % (max(DILATIONS) * BAND) == 0 and seq % chunk == 0
    prev = lambda c: jnp.maximum(c - 1, 0)
    return pl.pallas_call(
        functools.partial(_dilated_kernel, chunk=chunk, n_heads=n_heads),
        grid=(batch, n_heads // 2, seq // chunk),
        in_specs=[
            pl.BlockSpec((None, chunk, LANES), lambda b, p, c: (b, c, col_q + p)),
            pl.BlockSpec((None, chunk, LANES), lambda b, p, c: (b, prev(c), col_k + p)),
            pl.BlockSpec((None, chunk, LANES), lambda b, p, c: (b, c, col_k + p)),
            pl.BlockSpec((None, chunk, LANES), lambda b, p, c: (b, prev(c), col_v + p)),
            pl.BlockSpec((None, chunk, LANES), lambda b, p, c: (b, c, col_v + p)),
        ],
        out_specs=pl.BlockSpec((None, chunk, LANES), lambda b, p, c: (b, c, p)),
        out_shape=jax.ShapeDtypeStruct((batch, seq, n_heads * HEAD_DIM), F32),
        scratch_shapes=[
            pltpu.VMEM((2 * chunk, LANES), F32),
            pltpu.VMEM((2 * chunk, LANES), F32),
            pltpu.VMEM((len(DILATIONS), 2, chunk, LANES), F32),
        ],
        compiler_params=pltpu.CompilerParams(
            dimension_semantics=("parallel", "parallel", "arbitrary"),
            vmem_limit_bytes=VMEM_LIMIT),
        name="dilated_attn",
    )(act, act, act, act, act)


TAKEN = -3e38


def _topk_member(cur, k):
    n = cur.shape[0]
    row = lax.broadcasted_iota(jnp.int32, cur.shape, 0)
    sel = jnp.zeros(cur.shape, jnp.bool_)
    for _ in range(k):
        mx = jnp.max(cur, axis=0, keepdims=True)
        first = jnp.min(jnp.where(cur == mx, row, n), axis=0, keepdims=True)
        pick = row == first
        sel = jnp.logical_or(sel, pick)
        cur = jnp.where(pick, TAKEN, cur)
    return sel


def _moba_kernel(q_ref, k_ref, v_ref, o_ref, kaug_ref, vt_ref, kmean_ref, selb_ref, acc_ref,
                 t_ref, p_ref, *, blk, topk, n_heads):
    pair = pl.program_id(1)
    qi = pl.program_id(2)
    S = k_ref.shape[0]
    nblk = S // blk
    tq = tk = blk

    @pl.when(qi == 0)
    def _():
        jj = lax.broadcasted_iota(jnp.int32, (tk, 1), 0).astype(F32)
        kaug = _aug_const(tk, [jj]).astype(BF16)

        def body(c, carry):
            rows = pl.ds(pl.multiple_of(c * tk, tk), tk)
            kc = k_ref[rows, :]
            vc = v_ref[rows, :].astype(F32).T.astype(BF16)
            for e in range(2):
                kaug_ref[e, rows, :] = jnp.concatenate(
                    [kc[:, e * HEAD_DIM:(e + 1) * HEAD_DIM], kaug], axis=1)
                vt_ref[e, :, rows] = vc[e * HEAD_DIM:(e + 1) * HEAD_DIM, :]
            kmean_ref[pl.ds(c, 1), :] = jnp.mean(kc.astype(F32), axis=0, keepdims=True)
            return carry

        lax.fori_loop(0, nblk, body, 0)

    qt = q_ref[...]
    blk_row = lax.broadcasted_iota(jnp.int32, (nblk, tq), 0)
    past = blk_row < qi
    qs, slopes = [], []
    for e in range(2):
        slope = jnp.exp2(-8.0 * (2 * pair + e + 1).astype(F32) / n_heads)
        qe = qt[:, e * HEAD_DIM:(e + 1) * HEAD_DIM]
        km_hi, km_lo = _split_bf16(kmean_ref[:, e * HEAD_DIM:(e + 1) * HEAD_DIM])
        gate = _dot_nt(km_hi, qe) + _dot_nt(km_lo, qe)
        sel = _topk_member(jnp.where(past, gate, -2e38), topk)
        selb_ref[e] = jnp.where(jnp.logical_and(sel, past) | (blk_row == qi), 0.0, NEG)
        qs.append(jnp.concatenate([qe, _aug_const(tq, [slope]).astype(BF16)], axis=1))
        slopes.append(slope)

    rows = lambda n: pl.ds(pl.multiple_of(n * tk, tk), tk)
    qpos = qi * tq + lax.broadcasted_iota(jnp.int32, (1, tq), 1)
    causal = lambda n: n * tk + lax.broadcasted_iota(jnp.int32, (tk, 1), 0) <= qpos
    streams = [
        _Stream(qs[e],
                functools.partial(lambda n, e: kaug_ref[e, rows(n), :], e=e),
                functools.partial(lambda n, e: vt_ref[e, :, rows(n)], e=e),
                acc_ref.at[e], t_ref.at[e], p_ref.at[e],
                functools.partial(lambda n, e: slopes[e] * ((n - qi) * tk).astype(F32), e=e),
                bias=functools.partial(lambda n, e: selb_ref[e, pl.ds(n, 1), :], e=e),
                mask=causal)
        for e in range(2)]
    stats = _flash_sweep(streams, qi + 1, lambda pos: jnp.where(pos == 0, qi, pos - 1))
    o_ref[...] = jnp.concatenate([acc_ref[e] / stats[e][1] for e in range(2)], axis=0).T


def _moba_attention(qkv, batch, seq, col_q, col_k, col_v, n_heads, blk, topk):
    kern = functools.partial(_moba_kernel, blk=blk, topk=topk, n_heads=n_heads)
    nblk = seq // blk
    return pl.pallas_call(
        kern,
        grid=(batch, n_heads // 2, nblk),
        in_specs=[
            pl.BlockSpec((None, blk, LANES), lambda b, p, i: (b, i, col_q + p)),
            pl.BlockSpec((None, seq, LANES), lambda b, p, i: (b, 0, col_k + p)),
            pl.BlockSpec((None, seq, LANES), lambda b, p, i: (b, 0, col_v + p)),
        ],
        out_specs=pl.BlockSpec((None, blk, LANES), lambda b, p, i: (b, i, p)),
        out_shape=jax.ShapeDtypeStruct((batch, seq, n_heads * HEAD_DIM), F32),
        scratch_shapes=[
            pltpu.VMEM((2, seq, LANES), BF16),
            pltpu.VMEM((2, HEAD_DIM, seq), BF16),
            pltpu.VMEM((nblk, LANES), F32),
            pltpu.VMEM((2, nblk, blk), F32),
            pltpu.VMEM((2, HEAD_DIM, blk), F32),
            pltpu.VMEM((2, FLASH_UNROLL, blk, blk), F32),
            pltpu.VMEM((2, FLASH_STAGED, blk, blk), BF16),
        ],
        compiler_params=pltpu.CompilerParams(
            dimension_semantics=("parallel", "parallel", "arbitrary"),
            vmem_limit_bytes=VMEM_LIMIT),
        name="moba_attn",
    )(qkv, qkv, qkv)


CMP_STRIDE = 16
CMP_LEN = 32
SLC_BLK = 64
SLC_TOPK = 16
NSA_WINDOW = 512
NSA_TQ = 256
NSA_TK = 256


def _compress_kernel(c_ref, pe_ref, w1_ref, w2_ref, g_ref, o_ref, *, is_key):
    half = c_ref.shape[1]
    n = c_ref.shape[0]
    c = c_ref[...]
    w1 = w1_ref[...]
    a_lo = _dot(c, w1[:half])
    a_hi = _dot(c, w1[half:])
    pe_hi, pe_lo = _split_bf16(pe_ref[...])
    c0 = (_dot(pe_hi, w1) + _dot(pe_lo, w1))[0:1]
    y = a_lo + pltpu.roll(a_hi, n - 1, axis=0) + c0
    hid = (y * jax.nn.sigmoid(y)).astype(BF16)
    if is_key:
        z = _dot(hid, w2_ref[...])
    else:
        z = _dot(hid.astype(F32), w2_ref[...].astype(F32))
    if is_key:
        ms = jnp.sum(z * z, axis=-1, keepdims=True) * (1.0 / HEAD_DIM)
        kn = z * lax.rsqrt(ms + RMS_EPS) * g_ref[...]
        i = lax.broadcasted_iota(jnp.int32, (n, 1), 0)
        aug = _aug_const(n, [(i // 16 * 256).astype(F32), (i % 16 * 16).astype(F32), 1.0])
        o_ref[...] = jnp.concatenate([kn[:, :HEAD_DIM], aug], axis=1).astype(BF16)
    else:
        o_ref[...] = z.T[:HEAD_DIM, :].astype(BF16)


def _compress(chunks, pe, w1, w2, gain, is_key):
    B, Hkv, n, half = chunks.shape
    hidden = w1.shape[1]
    pe8 = jnp.broadcast_to(pe.reshape(1, 2 * half).astype(F32), (8, 2 * half))
    w2p = jnp.pad(w2, ((0, 0), (0, LANES - w2.shape[1]))).astype(BF16)
    g = jnp.pad(gain.astype(F32), (0, LANES - HEAD_DIM)).reshape(1, LANES)
    if is_key:
        out_shape = jax.ShapeDtypeStruct((B, Hkv, n, LANES), BF16)
        out_spec = pl.BlockSpec((None, None, n, LANES), lambda b, h: (b, h, 0, 0))
    else:
        out_shape = jax.ShapeDtypeStruct((B, Hkv, HEAD_DIM, n), BF16)
        out_spec = pl.BlockSpec((None, None, HEAD_DIM, n), lambda b, h: (b, h, 0, 0))
    return pl.pallas_call(
        functools.partial(_compress_kernel, is_key=is_key),
        grid=(B, Hkv),
        in_specs=[
            pl.BlockSpec((None, None, n, half), lambda b, h: (b, h, 0, 0)),
            pl.BlockSpec((8, 2 * half), lambda b, h: (0, 0)),
            pl.BlockSpec((2 * half, hidden), lambda b, h: (0, 0)),
            pl.BlockSpec((hidden, LANES), lambda b, h: (0, 0)),
            pl.BlockSpec((1, LANES), lambda b, h: (0, 0)),
        ],
        out_specs=out_spec,
        out_shape=out_shape,
        compiler_params=pltpu.CompilerParams(
            dimension_semantics=("parallel", "parallel"), vmem_limit_bytes=VMEM_LIMIT),
        name="nsa_compress_k" if is_key else "nsa_compress_v",
    )(chunks, pe8, w1.astype(BF16), w2p, g)


CMP_ROWS = 128


def _nsa_cmp_kernel(q_ref, kc_ref, vct_ref, ct_ref, o_ref, selb_ref, used_ref, imp_ref,
                    *, tq, group, n_heads):
    g = pl.program_id(1)
    qi = pl.program_id(2)
    n_cmp = kc_ref.shape[0]
    n_slc = ct_ref.shape[0]
    i0 = qi * tq
    qpos = i0 + lax.broadcasted_iota(jnp.int32, (1, tq), 1)
    qt = q_ref[...]

    def attend(rows):
        cend = lax.broadcasted_iota(jnp.int32, (rows, 1), 0) * CMP_STRIDE + (CMP_LEN - 1)
        valid = cend <= qpos
        kc = kc_ref[:rows, :]
        vct = vct_ref[:, :rows]
        ts = []
        for e in range(group):
            slope = jnp.exp2(-8.0 * (group * g + e + 1).astype(F32) / n_heads)
            aug = _aug_const(tq, [slope, slope, -slope * i0.astype(F32)]).astype(BF16)
            qa = jnp.concatenate([qt[:, e * HEAD_DIM:(e + 1) * HEAD_DIM], aug], axis=1)
            ts.append(_dot_nt(kc, qa))
        psum = jnp.zeros((rows, tq), F32)
        ps = []
        for t in ts:
            t = jnp.where(valid, t, NEG)
            m = jnp.max(t, axis=0, keepdims=True)
            p = jnp.where(valid, jnp.exp(t - m), 0.0)
            l = jnp.sum(p, axis=0, keepdims=True)
            p = p / jnp.where(l > 0, l, 1.0)
            ps.append(p.astype(BF16))
            psum = psum + p
        o_ref[...] = jnp.concatenate([_dot(vct, p) for p in ps], axis=0).T
        p_hi, p_lo = _split_bf16(psum)
        ct = ct_ref[:, :rows]
        imp_ref[...] = _dot(ct, p_hi) + _dot(ct, p_lo)

    needed = (i0 + tq - CMP_LEN) // CMP_STRIDE + 1
    n_var = max(n_cmp // CMP_ROWS, 1)
    for v in range(n_var):
        rows = n_cmp if v == n_var - 1 else (v + 1) * CMP_ROWS
        cond = needed > v * CMP_ROWS
        if v < n_var - 1:
            cond = jnp.logical_and(cond, needed <= (v + 1) * CMP_ROWS)
        pl.when(cond)(functools.partial(attend, rows))

    imp = imp_ref[...]
    jb = lax.broadcasted_iota(jnp.int32, (n_slc, tq), 0)
    qb = qpos // SLC_BLK
    forced = (jb == 0) | (jb == qb) | (jb == qb - 1)
    cur = jnp.where(forced, 3e38, jnp.where(jb > qb, -2e38, imp))
    sel = jnp.logical_and(_topk_member(cur, min(SLC_TOPK, n_slc)), jb <= qb)
    selb = jnp.where(sel, 0.0, NEG).T
    selb_ref[...] = selb.astype(BF16)
    for c in range(tq // NSA_TQ):
        hit = jnp.max(selb[c * NSA_TQ:(c + 1) * NSA_TQ], axis=0, keepdims=True) == 0.0
        used_ref[c] = jnp.where(hit, 1, 0)


def _nsa_cmp(act, kc, vct, batch, seq, col_q, n_heads, group, tq=256):
    Hkv = n_heads // group
    n_cmp = kc.shape[2]
    n_slc = seq // SLC_BLK
    cs = jnp.arange(n_cmp) * CMP_STRIDE
    ss = jnp.arange(n_slc) * SLC_BLK
    overlap = (cs[None, :] <= ss[:, None] + SLC_BLK - 1) & (cs[None, :] + CMP_LEN - 1 >= ss[:, None])
    overlap = overlap & (cs[None, :] + CMP_LEN <= seq)
    ct = overlap.astype(BF16)
    gw = group * HEAD_DIM
    return pl.pallas_call(
        functools.partial(_nsa_cmp_kernel, tq=tq, group=group, n_heads=n_heads),
        grid=(batch, Hkv, seq // tq),
        in_specs=[
            pl.BlockSpec((None, tq, gw), lambda b, g, i: (b, i, col_q + g)),
            pl.BlockSpec((None, None, n_cmp, LANES), lambda b, g, i: (b, g, 0, 0)),
            pl.BlockSpec((None, None, HEAD_DIM, n_cmp), lambda b, g, i: (b, g, 0, 0)),
            pl.BlockSpec((n_slc, n_cmp), lambda b, g, i: (0, 0)),
        ],
        out_specs=[
            pl.BlockSpec((None, tq, gw), lambda b, g, i: (b, i, g)),
            pl.BlockSpec((None, None, tq, n_slc), lambda b, g, i: (b, g, i, 0)),
            pl.BlockSpec((None, None, tq // NSA_TQ, 1, n_slc), lambda b, g, i: (b, g, i, 0, 0)),
        ],
        out_shape=[
            jax.ShapeDtypeStruct((batch, seq, n_heads * HEAD_DIM), F32),
            jax.ShapeDtypeStruct((batch, Hkv, seq, n_slc), BF16),
            jax.ShapeDtypeStruct((batch, Hkv, seq // NSA_TQ, 1, n_slc), jnp.int32),
        ],
        scratch_shapes=[pltpu.VMEM((n_slc, tq), F32)],
        compiler_params=pltpu.CompilerParams(
            dimension_semantics=("parallel", "parallel", "parallel"),
            vmem_limit_bytes=VMEM_LIMIT),
        name="nsa_cmp",
    )(act, kc, vct, ct)


def _nsa_main_kernel(used_ref, q_ref, selb_ref, kvs_ref, kvw_ref, kconst_ref, ocmp_ref, gate_ref,
                     o_ref, ks_ref, vst_ref, kw_ref, vwt_ref, accs_ref, accw_ref, gt_ref, t_ref,
                     p_ref, tiles_ref, *, tq, tk, group, n_heads):
    g = pl.program_id(1)
    qi = pl.program_id(2)
    S = kvs_ref.shape[0]
    R = group * tq
    i0 = qi * tq

    @pl.when(qi == 0)
    def _():
        def body(c, carry):
            rows = pl.ds(pl.multiple_of(c * tk, tk), tk)
            kvs = kvs_ref[rows, :]
            kvw = kvw_ref[rows, :]
            kconst = kconst_ref[...]
            ks_ref[rows, :HEAD_DIM] = kvs[:, :HEAD_DIM]
            ks_ref[rows, HEAD_DIM:LANES] = kconst[:, :HEAD_DIM]
            kw_ref[rows, :] = jnp.concatenate([kvw[:, :HEAD_DIM], kconst[:, :HEAD_DIM]], axis=1)
            vst_ref[:, rows] = kvs.astype(F32).T[HEAD_DIM:, :].astype(BF16)
            vwt_ref[:, rows] = kvw.astype(F32).T[HEAD_DIM:, :].astype(BF16)
            key_blk = (c * tk + lax.broadcasted_iota(jnp.int32, (tk, n_slc), 0)) // SLC_BLK
            onehot = key_blk == lax.broadcasted_iota(jnp.int32, (tk, n_slc), 1)
            ks_ref[rows, LANES:] = jnp.where(onehot, 1.0, 0.0).astype(BF16)
            return carry

        n_slc = ks_ref.shape[1] - LANES
        lax.fori_loop(0, S // tk, body, 0)

    qt = q_ref[...]
    selb = selb_ref[...]
    qs_parts, qw_parts, slope_parts = [], [], []
    for e in range(group):
        slope = jnp.exp2(-8.0 * (group * g + e + 1).astype(F32) / n_heads)
        aug = _aug_const(tq, [slope]).astype(BF16)
        qe = qt[:, e * HEAD_DIM:(e + 1) * HEAD_DIM]
        qw_parts.append(jnp.concatenate([qe, aug], axis=1))
        qs_parts.append(jnp.concatenate([qe, aug, selb], axis=1))
        slope_parts.append(jnp.full((1, tq), slope, F32))
    q_slc = jnp.concatenate(qs_parts, axis=0)
    q_win = jnp.concatenate(qw_parts, axis=0)
    slope_row = jnp.concatenate(slope_parts, axis=1)
    qpos = i0 + lax.broadcasted_iota(jnp.int32, (1, R), 1) % tq

    nd = i0 // tk
    rows = lambda n: pl.ds(pl.multiple_of(n * tk, tk), tk)
    shift = lambda n: slope_row * (n * tk - i0).astype(F32)
    kpos = lambda n: n * tk + lax.broadcasted_iota(jnp.int32, (tk, 1), 0)
    causal = lambda n: kpos(n) <= qpos

    def in_window(n):
        kp = kpos(n)
        dist = qpos - kp
        return (dist >= 0) & (dist < NSA_WINDOW) & (kp >= 0)

    n_tiles = S // tk
    base = ((pl.program_id(0) * pl.num_programs(1) + g) * pl.num_programs(2) + qi) * n_tiles

    def collect(n, cnt):
        tiles_ref[cnt] = n
        return cnt + jnp.where(used_ref[base + n] != 0, 1, 0)

    tiles_ref[0] = nd
    n_used = lax.fori_loop(0, nd, collect, 1)
    assert 2 * tk >= NSA_WINDOW - 1 and tk % tq == 0
    tw = []
    for c in range(3):
        n = nd - 2 + c
        t = _dot_nt(kw_ref[rows(jnp.maximum(n, 0)), :], q_win) + shift(n)
        tw.append(jnp.where(in_window(n), t, NEG))
    mw = functools.reduce(jnp.maximum, [jnp.max(t, axis=0, keepdims=True) for t in tw])
    lw = jnp.zeros((1, R), F32)
    o_w = jnp.zeros((HEAD_DIM, R), F32)
    for c in range(3):
        p = jnp.exp(tw[c] - mw)
        lw = lw + jnp.sum(p, axis=0, keepdims=True)
        o_w = o_w + _dot(vwt_ref[:, rows(jnp.maximum(nd - 2 + c, 0))], p.astype(BF16))
    accw_ref[...] = o_w

    slc = _Stream(q_slc, lambda n: ks_ref[rows(n), :], lambda n: vst_ref[:, rows(n)],
                  accs_ref, t_ref, p_ref, shift, mask=causal)
    ((_, ls),) = _flash_sweep([slc], n_used, lambda pos: tiles_ref[pos])

    gz = gate_ref[...]
    gt_ref[...] = jax.nn.sigmoid(gz).T
    oc_t = ocmp_ref[...].T
    outs = []
    for e in range(group):
        head = group * g + e
        cols = slice(e * tq, (e + 1) * tq)
        o_s = accs_ref[:, cols] / ls[:, cols]
        o_w = accw_ref[:, cols] / lw[:, cols]
        o_c = oc_t[e * HEAD_DIM:(e + 1) * HEAD_DIM, :]
        g_c = gt_ref[pl.ds(3 * head, 1), :]
        g_s = gt_ref[pl.ds(3 * head + 1, 1), :]
        g_w = gt_ref[pl.ds(3 * head + 2, 1), :]
        outs.append(g_c * o_c + g_s * o_s + g_w * o_w)
    o_ref[...] = jnp.concatenate(outs, axis=0).T


def _nsa_main(act, selb, used, ocmp, gates, batch, seq, col_q, col_kvs, col_kvw, col_gate, n_heads,
              group):
    tq, tk = NSA_TQ, NSA_TK
    Hkv = n_heads // group
    n_slc = seq // SLC_BLK
    gw = group * HEAD_DIM
    R = group * tq
    jj = jnp.arange(tk, dtype=F32)
    kconst = jnp.zeros((tk, HEAD_DIM), F32).at[:, 0].set(jj).astype(BF16)
    blk_per_tile = tk // SLC_BLK
    used_tiles = jnp.max(used.reshape(batch, Hkv, seq // tq, n_slc // blk_per_tile, blk_per_tile),
                         axis=-1).reshape(-1).astype(jnp.int32)
    grid_spec = pltpu.PrefetchScalarGridSpec(
        num_scalar_prefetch=1,
        grid=(batch, Hkv, seq // tq),
        in_specs=[
            pl.BlockSpec((None, tq, gw), lambda b, g, i, u: (b, i, col_q + g)),
            pl.BlockSpec((None, None, tq, n_slc), lambda b, g, i, u: (b, g, i, 0)),
            pl.BlockSpec((None, seq, LANES), lambda b, g, i, u: (b, 0, col_kvs + g)),
            pl.BlockSpec((None, seq, LANES), lambda b, g, i, u: (b, 0, col_kvw + g)),
            pl.BlockSpec((tk, HEAD_DIM), lambda b, g, i, u: (0, 0)),
            pl.BlockSpec((None, tq, gw), lambda b, g, i, u: (b, i, g)),
            pl.BlockSpec((None, tq, LANES), lambda b, g, i, u: (b, i, col_gate)),
        ],
        out_specs=pl.BlockSpec((None, tq, gw), lambda b, g, i, u: (b, i, g)),
        scratch_shapes=[
            pltpu.VMEM((seq, LANES + n_slc), BF16),
            pltpu.VMEM((HEAD_DIM, seq), BF16),
            pltpu.VMEM((seq, LANES), BF16),
            pltpu.VMEM((HEAD_DIM, seq), BF16),
            pltpu.VMEM((HEAD_DIM, R), F32),
            pltpu.VMEM((HEAD_DIM, R), F32),
            pltpu.VMEM((LANES, tq), F32),
            pltpu.VMEM((FLASH_UNROLL, tk, R), F32), pltpu.VMEM((FLASH_STAGED, tk, R), BF16),
            pltpu.SMEM((seq // tk,), jnp.int32),
        ],
    )
    return pl.pallas_call(
        functools.partial(_nsa_main_kernel, tq=tq, tk=tk, group=group, n_heads=n_heads),
        grid_spec=grid_spec,
        out_shape=jax.ShapeDtypeStruct((batch, seq, n_heads * HEAD_DIM), F32),
        compiler_params=pltpu.CompilerParams(
            dimension_semantics=("parallel", "parallel", "arbitrary"),
            vmem_limit_bytes=VMEM_LIMIT),
        name="nsa_main",
    )(used_tiles, act, selb, act, act, kconst, ocmp, gates)


def _tile_gain(g, width):
    return jnp.tile(g.astype(F32), width // HEAD_DIM)


def _even_layer(x, ln, w_in, qkn, lam, subln, w_out, layer):
    B, S, D = x.shape
    M = B * S
    x2d = x.reshape(M, D)
    W = 512
    aq, ak, av, az, bq, bk, bv, bz = [w_in[:, i * W:(i + 1) * W] for i in range(8)]
    scale = HEAD_DIM ** -0.5
    ones, zeros = jnp.ones((W,), F32), jnp.zeros((W,), F32)
    w_b = jnp.concatenate([bq, bk, bv], axis=1).astype(BF16)
    gain_b = jnp.concatenate([_tile_gain(qkn[2], W) * scale, _tile_gain(qkn[3], W), ones])
    mask_b = np.repeat([1.0, 1.0, 0.0], W)
    act_b = _project(x2d, ln, w_b, gain_b, mask_b, BF16)
    w_a = jnp.concatenate([aq, ak, av, az, bz], axis=1).astype(BF16)
    gain_a = jnp.concatenate([_tile_gain(qkn[0], W) * scale, _tile_gain(qkn[1], W), ones, ones, ones])
    mask_a = np.repeat([1.0, 1.0, 0.0, 0.0, 0.0], W)
    act_a = _project(x2d, ln, w_a, gain_a, mask_a, F32)

    lam_init = 0.8 - 0.6 * math.exp(-0.3 * layer)
    lf = lam.astype(F32)
    lam_f = jnp.exp(jnp.sum(lf[0] * lf[1])) - jnp.exp(jnp.sum(lf[2] * lf[3])) + lam_init
    ob = _diff_attention(act_b.reshape(B, S, 3 * W), lam_f, subln, B, S, 0, 4, 8, 4, 1.0 - lam_init)
    oa = _dilated_attention(act_a.reshape(B, S, 5 * W), B, S, 0, 4, 8, 8)
    out = _out_project(x2d, oa.reshape(M, W), ob.reshape(M, W), act_a, 3, 4, w_out)
    return out.reshape(B, S, D)


def _odd_layer(x, ln, w_in, qkn, phi_pe, phi_w1, phi_w2, w_out):
    B, S, D = x.shape
    M = B * S
    x2d = x.reshape(M, D)
    W, KW = 512, 128
    offs = [0]
    for s in (W, W, W, W, W, KW, KW, KW, KW, KW, KW, W, 24):
        offs.append(offs[-1] + s)
    cq, ck, cv, cz, dq, dkc, dvc, dks, dvs, dkw, dvw, dz, dg = [
        w_in[:, offs[i]:offs[i + 1]] for i in range(13)]
    scale = HEAD_DIM ** -0.5
    hd = HEAD_DIM
    ones = lambda n: jnp.ones((n,), F32)
    zeros = lambda n: jnp.zeros((n,), F32)
    kv_pair = lambda k, v: jnp.concatenate([k[:, :hd], v[:, :hd], k[:, hd:], v[:, hd:]], axis=1)
    w_b = jnp.concatenate([cq, ck, cv, dq, kv_pair(dks, dvs), kv_pair(dkw, dvw), dkc, dvc,
                           jnp.zeros((D, 2 * KW), w_in.dtype)], axis=1).astype(BF16)
    kv_gain = lambda g: jnp.concatenate([g, ones(hd), g, ones(hd)])
    kv_mask = np.repeat([1.0, 0.0, 1.0, 0.0], hd)
    gain_b = jnp.concatenate([_tile_gain(qkn[0], W) * scale, _tile_gain(qkn[1], W), ones(W),
                              _tile_gain(qkn[2], W) * scale, kv_gain(qkn[4].astype(F32)),
                              kv_gain(qkn[5].astype(F32)), ones(4 * KW)])
    mask_b = np.concatenate([np.repeat([1.0, 1.0, 0.0, 1.0], W), kv_mask, kv_mask, np.zeros(4 * KW)])
    act_b = _project(x2d, ln, w_b, gain_b, mask_b, BF16).reshape(B, S, 6 * W)
    w_g = jnp.concatenate([cz, dz, dg, jnp.zeros((D, W - 24), w_in.dtype)], axis=1).astype(BF16)
    act_g = _project(x2d, ln, w_g, ones(3 * W), np.zeros(3 * W), F32)

    oc = _moba_attention(act_b, B, S, 0, 4, 8, 8, 256, 3)

    def chunks(t):
        t = t.reshape(B, S // CMP_STRIDE, CMP_STRIDE, 2, hd)
        return jnp.transpose(t, (0, 3, 1, 2, 4)).reshape(B, 2, S // CMP_STRIDE, CMP_STRIDE * hd)

    kc = _compress(chunks(act_b[:, :, 20 * KW:21 * KW]), phi_pe[0], phi_w1[0], phi_w2[0], qkn[3], True)
    vct = _compress(chunks(act_b[:, :, 21 * KW:22 * KW]), phi_pe[1], phi_w1[1], phi_w2[1], qkn[3], False)
    ocmp, selb, used = _nsa_cmp(act_b, kc, vct, B, S, 6, 8, 4)
    od = _nsa_main(act_b, selb, used, ocmp, act_g.reshape(B, S, 3 * W), B, S, 6, 16, 18, 8, 8, 4)
    out = _out_project(x2d, oc.reshape(M, W), od.reshape(M, W), act_g, 0, 1, w_out)
    return out.reshape(B, S, D)


def kernel(x, ln_e, w_in_e, qkn_e, lam_e, subln_e, w_out_e, ln_o, w_in_o, qkn_o, phi_pe, phi_w1, phi_w2, w_out_o):
    n_layers = ln_e.shape[0] + ln_o.shape[0]
    for layer in range(n_layers):
        i = layer // 2
        if layer % 2 == 0:
            x = _even_layer(x, ln_e[i], w_in_e[i], qkn_e[i], lam_e[i], subln_e[i], w_out_e[i], layer)
        else:
            x = _odd_layer(x, ln_o[i], w_in_o[i], qkn_o[i], phi_pe[i], phi_w1[i], phi_w2[i], w_out_o[i])
    return x
```

```python
import functools
import math

import jax
import jax.numpy as jnp
import numpy as np
from jax import lax
from jax.experimental import pallas as pl
from jax.experimental.pallas import tpu as pltpu

HEAD_DIM = 64
LANES = 128
RMS_EPS = 1e-6
NEG = -2e30
M_INIT = -1e30
VMEM_LIMIT = 56 * 1024 * 1024

F32 = jnp.float32
BF16 = jnp.bfloat16

_NT = (((1,), (1,)), ((), ()))


def _dot(a, b):
    return jnp.dot(a, b, preferred_element_type=F32)


def _dot_nt(a, b):
    return lax.dot_general(a, b, _NT, preferred_element_type=F32)


def _alibi_slopes(n):
    return [2.0 ** (-8.0 * (i + 1) / n) for i in range(n)]


def _split_bf16(x):
    hi = x.astype(BF16)
    lo = (x - hi.astype(F32)).astype(BF16)
    return hi, lo


NORM_CHUNK = 256


def _proj_kernel(x_ref, ln_ref, w_ref, gain_ref, nmask_ref, bd_ref, o_ref, *, norm_chunks):
    x = x_ref[...]
    ms = jnp.mean(x * x, axis=-1, keepdims=True)
    h = (x * lax.rsqrt(ms + RMS_EPS) * ln_ref[...]).astype(BF16)
    bd = bd_ref[...]
    col = lambda c: slice(c * NORM_CHUNK, (c + 1) * NORM_CHUNK)
    ys = {}
    for c, normed in enumerate(norm_chunks):
        y = _dot(h, w_ref[:, col(c)])
        if normed:
            ys[c] = y
        else:
            o_ref[:, col(c)] = y.astype(o_ref.dtype)
    for c, y in ys.items():
        hi, lo = _split_bf16(y * y)
        ms = _dot(hi, bd) + _dot(lo, bd)
        yn = y * lax.rsqrt(ms + RMS_EPS) * gain_ref[:, col(c)]
        o_ref[:, col(c)] = jnp.where(nmask_ref[:, col(c)] > 0.5, yn, y).astype(o_ref.dtype)


def _project(x2d, ln, w, gain, nmask, out_dtype, tm=1024):
    M, D = x2d.shape
    N = w.shape[1]
    assert M % tm == 0 and N % NORM_CHUNK == 0
    norm_chunks = [bool(c) for c in np.asarray(nmask).reshape(-1, NORM_CHUNK).max(axis=1) > 0.5]
    nmask = jnp.asarray(nmask, F32)
    r = jnp.arange(NORM_CHUNK) // HEAD_DIM
    bd = jnp.where(r[:, None] == r[None, :], 1.0 / HEAD_DIM, 0.0).astype(BF16)
    return pl.pallas_call(
        functools.partial(_proj_kernel, norm_chunks=tuple(norm_chunks)),
        grid=(M // tm,),
        in_specs=[
            pl.BlockSpec((tm, D), lambda i: (i, 0)),
            pl.BlockSpec((1, D), lambda i: (0, 0)),
            pl.BlockSpec((D, N), lambda i: (0, 0)),
            pl.BlockSpec((1, N), lambda i: (0, 0)),
            pl.BlockSpec((1, N), lambda i: (0, 0)),
            pl.BlockSpec((NORM_CHUNK, NORM_CHUNK), lambda i: (0, 0)),
        ],
        out_specs=pl.BlockSpec((tm, N), lambda i: (i, 0)),
        out_shape=jax.ShapeDtypeStruct((M, N), out_dtype),
        compiler_params=pltpu.CompilerParams(
            dimension_semantics=("parallel",), vmem_limit_bytes=VMEM_LIMIT),
        name="proj",
    )(x2d, ln.reshape(1, D).astype(F32), w, gain.reshape(1, N).astype(F32),
      nmask.reshape(1, N).astype(F32), bd)


def _outproj_kernel(x_ref, oa_ref, ob_ref, za_ref, zb_ref, wa_ref, wb_ref, o_ref):
    za = za_ref[...]
    zb = zb_ref[...]
    ma = (oa_ref[...] * (za * jax.nn.sigmoid(za))).astype(BF16)
    mb = (ob_ref[...] * (zb * jax.nn.sigmoid(zb))).astype(BF16)
    o_ref[...] = x_ref[...] + _dot(ma, wa_ref[...]) + _dot(mb, wb_ref[...])


def _out_project(x2d, oa, ob, z, za_blk, zb_blk, w_out, tm=512):
    M, D = x2d.shape
    W = oa.shape[1]
    wa = w_out[:W].astype(BF16)
    wb = w_out[W:].astype(BF16)
    return pl.pallas_call(
        _outproj_kernel,
        grid=(M // tm,),
        in_specs=[
            pl.BlockSpec((tm, D), lambda i: (i, 0)),
            pl.BlockSpec((tm, W), lambda i: (i, 0)),
            pl.BlockSpec((tm, W), lambda i: (i, 0)),
            pl.BlockSpec((tm, W), lambda i: (i, za_blk)),
            pl.BlockSpec((tm, W), lambda i: (i, zb_blk)),
            pl.BlockSpec((W, D), lambda i: (0, 0)),
            pl.BlockSpec((W, D), lambda i: (0, 0)),
        ],
        out_specs=pl.BlockSpec((tm, D), lambda i: (i, 0)),
        out_shape=jax.ShapeDtypeStruct((M, D), F32),
        compiler_params=pltpu.CompilerParams(
            dimension_semantics=("parallel",), vmem_limit_bytes=VMEM_LIMIT),
        name="outproj",
    )(x2d, oa, ob, z, z, wa, wb)


class _Stream:
    def __init__(self, q, k_tile, vt_tile, acc_ref, t_ref, p_ref, shift, bias=None, mask=None,
                 mask_every_tile=False):
        self.q, self.k_tile, self.vt_tile = q, k_tile, vt_tile
        self.acc_ref, self.t_ref, self.p_ref = acc_ref, t_ref, p_ref
        self.shift, self.bias, self.mask = shift, bias, mask
        self.mask_every_tile = mask_every_tile

    def scores(self, n):
        return _dot_nt(self.k_tile(n), self.q)

    def softmax(self, t, n, m, l, pad, masked):
        if masked:
            t = jnp.where(self.mask(n), t, NEG)
        off = self.shift(n) + pad
        if self.bias is not None:
            off = off + self.bias(n)
        m_new = jnp.maximum(m, jnp.max(t, axis=0, keepdims=True) + off)
        p = jnp.exp(t - (m_new - off))
        alpha = jnp.exp(m - m_new)
        return m_new, alpha * l + jnp.sum(p, axis=0, keepdims=True), alpha, p.astype(BF16)


FLASH_UNROLL = 2
FLASH_STAGED = 1
FLASH_QK_FIRST = False


def _flash_sweep(streams, count, tile_of):
    U, D = FLASH_UNROLL, FLASH_STAGED
    R = streams[0].q.shape[0]
    last = count - 1
    W = 2 + D
    state = []
    for s in streams:
        s.acc_ref[...] = jnp.zeros_like(s.acc_ref)
        s.p_ref[...] = jnp.zeros_like(s.p_ref)
        for k in range(U):
            s.t_ref[k] = s.scores(tile_of(jnp.minimum(k, last)))
        state += [jnp.full((1, R), M_INIT, F32), jnp.zeros((1, R), F32)]
        state += [jnp.ones((1, R), F32)] * D

    def flush(n_prev, state):
        pvs = [[_dot(s.vt_tile(n_prev[d]), s.p_ref[d]) for d in range(D)] for s in streams]
        for i, s in enumerate(streams):
            acc = s.acc_ref[...]
            for d in range(D):
                acc = state[W * i + 2 + d] * acc + pvs[i][d]
            s.acc_ref[...] = acc

    def body(g, carry, first_body):
        n_prev, state = list(carry[:D]), list(carry[D:])
        base = U * g
        if FLASH_QK_FIRST:
            nxt = [[s.scores(tile_of(jnp.minimum(base + U + k, last))) for k in range(U)]
                   for s in streams]
            flush(n_prev, state)
        else:
            flush(n_prev, state)
            nxt = [[s.scores(tile_of(jnp.minimum(base + U + k, last))) for k in range(U)]
                   for s in streams]
        for k in range(U):
            n = tile_of(jnp.minimum(base + k, last))
            pad = jnp.where(base + k < count, 0.0, NEG)
            d = k - (U - D)
            for i, s in enumerate(streams):
                m, l = state[W * i:W * i + 2]
                masked = s.mask is not None and (s.mask_every_tile or (first_body and k == 0))
                m, l, a, p = s.softmax(s.t_ref[k], n, m, l, pad, masked)
                state[W * i:W * i + 2] = [m, l]
                if d < 0:
                    s.acc_ref[...] = a * s.acc_ref[...] + _dot(s.vt_tile(n), p)
                else:
                    state[W * i + 2 + d] = a
                    s.p_ref[d] = p
            if d >= 0:
                n_prev[d] = n
        for i, s in enumerate(streams):
            for k in range(U):
                s.t_ref[k] = nxt[i][k]
        return tuple(n_prev) + tuple(state)

    carry = body(0, (tile_of(0),) * D + tuple(state), True)
    carry = lax.fori_loop(1, (count + U - 1) // U, lambda g, c: body(g, c, False), carry)
    flush(carry[:D], carry[D:])
    return [(carry[D + W * i], carry[D + W * i + 1]) for i in range(len(streams))]


def _aug_const(rows, col_vals):
    lane = lax.broadcasted_iota(jnp.int32, (rows, HEAD_DIM), 1)
    out = jnp.zeros((rows, HEAD_DIM), F32)
    for c, v in enumerate(col_vals):
        out = jnp.where(lane == c, v, out)
    return out


def _diff_kernel(lam_ref, q_ref, k_ref, v_ref, subln_ref, o_ref,
                 k1_ref, k2_ref, vt_ref, acc1_ref, acc2_ref, t_ref, p_ref, *, tq, n_heads, out_scale):
    h = pl.program_id(1)
    qi = pl.program_id(2)
    S = k_ref.shape[0]
    tk = tq
    slope = jnp.exp2(-8.0 * (h + 1).astype(F32) / n_heads)

    @pl.when(qi == 0)
    def _():
        jj = lax.broadcasted_iota(jnp.int32, (tk, 1), 0).astype(F32)
        kaug = _aug_const(tk, [jj]).astype(BF16)

        def body(c, carry):
            rows = pl.ds(pl.multiple_of(c * tk, tk), tk)
            kc = k_ref[rows, :]
            k1_ref[rows, :] = jnp.concatenate([kc[:, :HEAD_DIM], kaug], axis=1)
            k2_ref[rows, :] = jnp.concatenate([kc[:, HEAD_DIM:], kaug], axis=1)
            vt_ref[:, rows] = v_ref[rows, :].astype(F32).T.astype(BF16)
            return carry

        lax.fori_loop(0, S // tk, body, 0)

    qt = q_ref[...]
    qaug = _aug_const(tq, [slope]).astype(BF16)
    q1 = jnp.concatenate([qt[:, :HEAD_DIM], qaug], axis=1)
    q2 = jnp.concatenate([qt[:, HEAD_DIM:], qaug], axis=1)

    rows = lambda n: pl.ds(pl.multiple_of(n * tk, tk), tk)
    qpos = qi * tq + lax.broadcasted_iota(jnp.int32, (1, tq), 1)
    causal = lambda n: n * tk + lax.broadcasted_iota(jnp.int32, (tk, 1), 0) <= qpos
    shift = lambda n: slope * ((n - qi) * tk).astype(F32)
    vt_tile = lambda n: vt_ref[:, rows(n)]
    streams = [
        _Stream(q1, lambda n: k1_ref[rows(n), :], vt_tile, acc1_ref, t_ref.at[0], p_ref.at[0],
                shift, mask=causal),
        _Stream(q2, lambda n: k2_ref[rows(n), :], vt_tile, acc2_ref, t_ref.at[1], p_ref.at[1],
                shift, mask=causal),
    ]
    (m1, l1), (m2, l2) = _flash_sweep(streams, qi + 1, lambda pos: jnp.where(pos == 0, qi, pos - 1))

    d = acc1_ref[...] / l1 - lam_ref[0] * (acc2_ref[...] / l2)
    r = lax.rsqrt(jnp.mean(d * d, axis=0, keepdims=True) + RMS_EPS)
    o_ref[...] = (d * r).T * (subln_ref[...] * out_scale)


def _diff_attention(qkv, lam_f, subln, batch, seq, col_q, col_k, col_v, n_heads, out_scale, tq=256):
    dv = 2 * HEAD_DIM
    kern = functools.partial(_diff_kernel, tq=tq, n_heads=n_heads, out_scale=out_scale)
    grid_spec = pltpu.PrefetchScalarGridSpec(
        num_scalar_prefetch=1,
        grid=(batch, n_heads, seq // tq),
        in_specs=[
            pl.BlockSpec((None, tq, dv), lambda b, h, i, s: (b, i, col_q + h)),
            pl.BlockSpec((None, seq, dv), lambda b, h, i, s: (b, 0, col_k + h)),
            pl.BlockSpec((None, seq, dv), lambda b, h, i, s: (b, 0, col_v + h)),
            pl.BlockSpec((1, dv), lambda b, h, i, s: (0, 0)),
        ],
        out_specs=pl.BlockSpec((None, tq, dv), lambda b, h, i, s: (b, i, h)),
        scratch_shapes=[
            pltpu.VMEM((seq, LANES), BF16), pltpu.VMEM((seq, LANES), BF16),
            pltpu.VMEM((dv, seq), BF16),
            pltpu.VMEM((dv, tq), F32), pltpu.VMEM((dv, tq), F32),
            pltpu.VMEM((2, FLASH_UNROLL, tq, tq), F32), pltpu.VMEM((2, FLASH_STAGED, tq, tq), BF16),
        ],
    )
    return pl.pallas_call(
        kern,
        grid_spec=grid_spec,
        out_shape=jax.ShapeDtypeStruct((batch, seq, n_heads * dv), F32),
        compiler_params=pltpu.CompilerParams(
            dimension_semantics=("parallel", "parallel", "arbitrary"),
            vmem_limit_bytes=VMEM_LIMIT),
        name="diff_attn",
    )(lam_f.reshape(1).astype(F32), qkv, qkv, qkv, subln.reshape(1, dv).astype(F32))


BAND = 128
DILATIONS = (1, 4, 16)


def _dilated_kernel(q_ref, kp_ref, kc_ref, vp_ref, vc_ref, o_ref, kbuf_ref, vbuf_ref, st_ref,
                    *, chunk, n_heads):
    pair = pl.program_id(1)
    ci = pl.program_id(2)
    kbuf_ref[:chunk, :] = kp_ref[...]
    kbuf_ref[chunk:, :] = kc_ref[...]
    vbuf_ref[:chunk, :] = vp_ref[...]
    vbuf_ref[chunk:, :] = vc_ref[...]

    qi = lax.broadcasted_iota(jnp.int32, (BAND, 2 * BAND), 0)
    kj = lax.broadcasted_iota(jnp.int32, (BAND, 2 * BAND), 1)
    dist = qi + BAND - kj
    in_band = dist.astype(jnp.uint32) <= BAND
    dist_f = dist.astype(F32)
    key_col = lax.broadcasted_iota(jnp.int32, (1, 2 * BAND), 1)
    slopes = [jnp.exp2(-8.0 * (2 * pair + e + 1).astype(F32) / n_heads) for e in range(2)]
    lane_q = lax.broadcasted_iota(jnp.int32, (BAND, LANES), 1)
    own_half = [lane_q < HEAD_DIM, lane_q >= HEAD_DIM]

    for pi, dil in enumerate(DILATIONS):
        span = dil * BAND
        n_blk = chunk // span

        def block(idx, carry, dil=dil, span=span, n_blk=n_blk, pi=pi):
            r = idx // n_blk
            b = idx % n_blk
            q0 = r + b * span
            qb = q_ref[pl.ds(q0, BAND, stride=dil), :]
            kb = kbuf_ref[pl.ds(chunk + q0 - span, 2 * BAND, stride=dil), :]
            vb = vbuf_ref[pl.ds(chunk + q0 - span, 2 * BAND, stride=dil), :]
            first = jnp.logical_and(ci == 0, b == 0)
            before_start = jnp.where(key_col < jnp.where(first, BAND, 0), NEG, 0.0)
            kb16 = kb.astype(BF16)
            vb16 = vb.astype(BF16)
            packed = []
            for e in range(2):
                qe = jnp.where(own_half[e], qb, 0.0).astype(BF16)
                s = _dot_nt(qe, kb16)
                s = jnp.where(in_band, s - (slopes[e] * dil) * dist_f, NEG) + before_start
                m = jnp.max(s, axis=1, keepdims=True)
                p = jnp.exp(s - m)
                l = jnp.sum(p, axis=1, keepdims=True)
                o = _dot(p.astype(BF16), vb16) / l
                packed.append(jnp.where(own_half[e], o, m + jnp.log(l)))
            for e in range(2):
                st_ref[pi, e, pl.ds(q0, BAND, stride=dil), :] = packed[e]
            return carry

        lax.fori_loop(0, dil * n_blk, block, 0, unroll=8)

    rows_per = 256
    lane = lax.broadcasted_iota(jnp.int32, (rows_per, LANES), 1)

    def merge(c, carry):
        rows = pl.ds(pl.multiple_of(c * rows_per, rows_per), rows_per)
        outs = []
        for e in range(2):
            xs = [st_ref[pi, e, rows, :] for pi in range(len(DILATIONS))]
            top = functools.reduce(jnp.maximum, xs)
            num = jnp.zeros((rows_per, LANES), F32)
            den = jnp.zeros((rows_per, LANES), F32)
            for x in xs:
                w = pltpu.roll(jnp.exp(x - top), HEAD_DIM, axis=1)
                num = num + w * x
                den = den + w
            outs.append(num / den)
        o_ref[rows, :] = jnp.where(lane < HEAD_DIM, outs[0], outs[1])
        return carry

    lax.fori_loop(0, chunk // rows_per, merge, 0)


def _dilated_attention(act, batch, seq, col_q, col_k, col_v, n_heads, chunk=2048):
    assert chunk % (max(DILATIONS) * BAND) == 0 and seq % chunk == 0
    prev = lambda c: jnp.maximum(c - 1, 0)
    return pl.pallas_call(
        functools.partial(_dilated_kernel, chunk=chunk, n_heads=n_heads),
        grid=(batch, n_heads // 2, seq // chunk),
        in_specs=[
            pl.BlockSpec((None, chunk, LANES), lambda b, p, c: (b, c, col_q + p)),
            pl.BlockSpec((None, chunk, LANES), lambda b, p, c: (b, prev(c), col_k + p)),
            pl.BlockSpec((None, chunk, LANES), lambda b, p, c: (b, c, col_k + p)),
            pl.BlockSpec((None, chunk, LANES), lambda b, p, c: (b, prev(c), col_v + p)),
            pl.BlockSpec((None, chunk, LANES), lambda b, p, c: (b, c, col_v + p)),
        ],
        out_specs=pl.BlockSpec((None, chunk, LANES), lambda b, p, c: (b, c, p)),
        out_shape=jax.ShapeDtypeStruct((batch, seq, n_heads * HEAD_DIM), F32),
        scratch_shapes=[
            pltpu.VMEM((2 * chunk, LANES), F32),
            pltpu.VMEM((2 * chunk, LANES), F32),
            pltpu.VMEM((len(DILATIONS), 2, chunk, LANES), F32),
        ],
        compiler_params=pltpu.CompilerParams(
            dimension_semantics=("parallel", "parallel", "arbitrary"),
            vmem_limit_bytes=VMEM_LIMIT),
        name="dilated_attn",
    )(act, act, act, act, act)


TAKEN = -3e38


def _topk_member(cur, k):
    n = cur.shape[0]
    row = lax.broadcasted_iota(jnp.int32, cur.shape, 0)
    for _ in range(k):
        mx = jnp.max(cur, axis=0, keepdims=True)
        first = jnp.min(jnp.where(cur == mx, row, n), axis=0, keepdims=True)
        cur = jnp.where(row == first, TAKEN, cur)
    return cur == TAKEN


def _moba_kernel(q_ref, k_ref, v_ref, o_ref, kaug_ref, vt_ref, kmean_ref, selb_ref, acc_ref,
                 t_ref, p_ref, *, blk, topk, n_heads):
    pair = pl.program_id(1)
    qi = pl.program_id(2)
    S = k_ref.shape[0]
    nblk = S // blk
    tq = tk = blk

    @pl.when(qi == 0)
    def _():
        jj = lax.broadcasted_iota(jnp.int32, (tk, 1), 0).astype(F32)
        kaug = _aug_const(tk, [jj]).astype(BF16)

        def body(c, carry):
            rows = pl.ds(pl.multiple_of(c * tk, tk), tk)
            kc = k_ref[rows, :]
            vc = v_ref[rows, :].astype(F32).T.astype(BF16)
            for e in range(2):
                kaug_ref[e, rows, :] = jnp.concatenate(
                    [kc[:, e * HEAD_DIM:(e + 1) * HEAD_DIM], kaug], axis=1)
                vt_ref[e, :, rows] = vc[e * HEAD_DIM:(e + 1) * HEAD_DIM, :]
            kmean_ref[pl.ds(c, 1), :] = jnp.mean(kc.astype(F32), axis=0, keepdims=True)
            return carry

        lax.fori_loop(0, nblk, body, 0)

    qt = q_ref[...]
    blk_row = lax.broadcasted_iota(jnp.int32, (nblk, tq), 0)
    past = blk_row < qi
    qs, slopes = [], []
    for e in range(2):
        slope = jnp.exp2(-8.0 * (2 * pair + e + 1).astype(F32) / n_heads)
        qe = qt[:, e * HEAD_DIM:(e + 1) * HEAD_DIM]
        km_hi, km_lo = _split_bf16(kmean_ref[:, e * HEAD_DIM:(e + 1) * HEAD_DIM])
        gate = _dot_nt(km_hi, qe) + _dot_nt(km_lo, qe)
        sel = _topk_member(jnp.where(past, gate, -2e38), topk)
        selb_ref[e] = jnp.where(blk_row == qi, 0.0, jnp.where(past, jnp.where(sel, 0.0, NEG), NEG))
        qs.append(jnp.concatenate([qe, _aug_const(tq, [slope]).astype(BF16)], axis=1))
        slopes.append(slope)

    rows = lambda n: pl.ds(pl.multiple_of(n * tk, tk), tk)
    qpos = qi * tq + lax.broadcasted_iota(jnp.int32, (1, tq), 1)
    causal = lambda n: n * tk + lax.broadcasted_iota(jnp.int32, (tk, 1), 0) <= qpos
    streams = [
        _Stream(qs[e],
                functools.partial(lambda n, e: kaug_ref[e, rows(n), :], e=e),
                functools.partial(lambda n, e: vt_ref[e, :, rows(n)], e=e),
                acc_ref.at[e], t_ref.at[e], p_ref.at[e],
                functools.partial(lambda n, e: slopes[e] * ((n - qi) * tk).astype(F32), e=e),
                bias=functools.partial(lambda n, e: selb_ref[e, pl.ds(n, 1), :], e=e),
                mask=causal)
        for e in range(2)]
    stats = _flash_sweep(streams, qi + 1, lambda pos: jnp.where(pos == 0, qi, pos - 1))
    o_ref[...] = jnp.concatenate([acc_ref[e] / stats[e][1] for e in range(2)], axis=0).T


def _moba_attention(qkv, batch, seq, col_q, col_k, col_v, n_heads, blk, topk):
    kern = functools.partial(_moba_kernel, blk=blk, topk=topk, n_heads=n_heads)
    nblk = seq // blk
    return pl.pallas_call(
        kern,
        grid=(batch, n_heads // 2, nblk),
        in_specs=[
            pl.BlockSpec((None, blk, LANES), lambda b, p, i: (b, i, col_q + p)),
            pl.BlockSpec((None, seq, LANES), lambda b, p, i: (b, 0, col_k + p)),
            pl.BlockSpec((None, seq, LANES), lambda b, p, i: (b, 0, col_v + p)),
        ],
        out_specs=pl.BlockSpec((None, blk, LANES), lambda b, p, i: (b, i, p)),
        out_shape=jax.ShapeDtypeStruct((batch, seq, n_heads * HEAD_DIM), F32),
        scratch_shapes=[
            pltpu.VMEM((2, seq, LANES), BF16),
            pltpu.VMEM((2, HEAD_DIM, seq), BF16),
            pltpu.VMEM((nblk, LANES), F32),
            pltpu.VMEM((2, nblk, blk), F32),
            pltpu.VMEM((2, HEAD_DIM, blk), F32),
            pltpu.VMEM((2, FLASH_UNROLL, blk, blk), F32),
            pltpu.VMEM((2, FLASH_STAGED, blk, blk), BF16),
        ],
        compiler_params=pltpu.CompilerParams(
            dimension_semantics=("parallel", "parallel", "arbitrary"),
            vmem_limit_bytes=VMEM_LIMIT),
        name="moba_attn",
    )(qkv, qkv, qkv)


CMP_STRIDE = 16
CMP_LEN = 32
SLC_BLK = 64
SLC_TOPK = 16
NSA_WINDOW = 512
NSA_TQ = 256
NSA_TK = 256


def _compress_kernel(c_ref, pe_ref, w1_ref, w2_ref, g_ref, o_ref, *, is_key):
    half = c_ref.shape[1]
    n = c_ref.shape[0]
    c = c_ref[...]
    w1 = w1_ref[...]
    a_lo = _dot(c, w1[:half])
    a_hi = _dot(c, w1[half:])
    pe_hi, pe_lo = _split_bf16(pe_ref[...])
    c0 = (_dot(pe_hi, w1) + _dot(pe_lo, w1))[0:1]
    y = a_lo + pltpu.roll(a_hi, n - 1, axis=0) + c0
    hid = (y * jax.nn.sigmoid(y)).astype(BF16)
    if is_key:
        z = _dot(hid, w2_ref[...])
    else:
        z = _dot(hid.astype(F32), w2_ref[...].astype(F32))
    if is_key:
        ms = jnp.sum(z * z, axis=-1, keepdims=True) * (1.0 / HEAD_DIM)
        kn = z * lax.rsqrt(ms + RMS_EPS) * g_ref[...]
        i = lax.broadcasted_iota(jnp.int32, (n, 1), 0)
        aug = _aug_const(n, [(i // 16 * 256).astype(F32), (i % 16 * 16).astype(F32), 1.0])
        o_ref[...] = jnp.concatenate([kn[:, :HEAD_DIM], aug], axis=1).astype(BF16)
    else:
        o_ref[...] = z.T[:HEAD_DIM, :].astype(BF16)


def _compress(chunks, pe, w1, w2, gain, is_key):
    B, Hkv, n, half = chunks.shape
    hidden = w1.shape[1]
    pe8 = jnp.broadcast_to(pe.reshape(1, 2 * half).astype(F32), (8, 2 * half))
    w2p = jnp.pad(w2, ((0, 0), (0, LANES - w2.shape[1]))).astype(BF16)
    g = jnp.pad(gain.astype(F32), (0, LANES - HEAD_DIM)).reshape(1, LANES)
    if is_key:
        out_shape = jax.ShapeDtypeStruct((B, Hkv, n, LANES), BF16)
        out_spec = pl.BlockSpec((None, None, n, LANES), lambda b, h: (b, h, 0, 0))
    else:
        out_shape = jax.ShapeDtypeStruct((B, Hkv, HEAD_DIM, n), BF16)
        out_spec = pl.BlockSpec((None, None, HEAD_DIM, n), lambda b, h: (b, h, 0, 0))
    return pl.pallas_call(
        functools.partial(_compress_kernel, is_key=is_key),
        grid=(B, Hkv),
        in_specs=[
            pl.BlockSpec((None, None, n, half), lambda b, h: (b, h, 0, 0)),
            pl.BlockSpec((8, 2 * half), lambda b, h: (0, 0)),
            pl.BlockSpec((2 * half, hidden), lambda b, h: (0, 0)),
            pl.BlockSpec((hidden, LANES), lambda b, h: (0, 0)),
            pl.BlockSpec((1, LANES), lambda b, h: (0, 0)),
        ],
        out_specs=out_spec,
        out_shape=out_shape,
        compiler_params=pltpu.CompilerParams(
            dimension_semantics=("parallel", "parallel"), vmem_limit_bytes=VMEM_LIMIT),
        name="nsa_compress_k" if is_key else "nsa_compress_v",
    )(chunks, pe8, w1.astype(BF16), w2p, g)


CMP_ROWS = 128


def _nsa_cmp_kernel(q_ref, kc_ref, vct_ref, ct_ref, o_ref, selb_ref, used_ref, imp_ref,
                    *, tq, group, n_heads):
    g = pl.program_id(1)
    qi = pl.program_id(2)
    n_cmp = kc_ref.shape[0]
    n_slc = ct_ref.shape[0]
    i0 = qi * tq
    qpos = i0 + lax.broadcasted_iota(jnp.int32, (1, tq), 1)
    qt = q_ref[...]

    def attend(rows):
        cend = lax.broadcasted_iota(jnp.int32, (rows, 1), 0) * CMP_STRIDE + (CMP_LEN - 1)
        valid = cend <= qpos
        kc = kc_ref[:rows, :]
        vct = vct_ref[:, :rows]
        ts = []
        for e in range(group):
            slope = jnp.exp2(-8.0 * (group * g + e + 1).astype(F32) / n_heads)
            aug = _aug_const(tq, [slope, slope, -slope * i0.astype(F32)]).astype(BF16)
            qa = jnp.concatenate([qt[:, e * HEAD_DIM:(e + 1) * HEAD_DIM], aug], axis=1)
            ts.append(_dot_nt(kc, qa))
        psum = jnp.zeros((rows, tq), F32)
        ps = []
        for t in ts:
            t = jnp.where(valid, t, NEG)
            m = jnp.max(t, axis=0, keepdims=True)
            p = jnp.where(valid, jnp.exp(t - m), 0.0)
            l = jnp.sum(p, axis=0, keepdims=True)
            p = p / jnp.where(l > 0, l, 1.0)
            ps.append(p.astype(BF16))
            psum = psum + p
        o_ref[...] = jnp.concatenate([_dot(vct, p) for p in ps], axis=0).T
        p_hi, p_lo = _split_bf16(psum)
        ct = ct_ref[:, :rows]
        imp_ref[...] = _dot(ct, p_hi) + _dot(ct, p_lo)

    needed = (i0 + tq - CMP_LEN) // CMP_STRIDE + 1
    n_var = max(n_cmp // CMP_ROWS, 1)
    for v in range(n_var):
        rows = n_cmp if v == n_var - 1 else (v + 1) * CMP_ROWS
        cond = needed > v * CMP_ROWS
        if v < n_var - 1:
            cond = jnp.logical_and(cond, needed <= (v + 1) * CMP_ROWS)
        pl.when(cond)(functools.partial(attend, rows))

    imp = imp_ref[...]
    jb = lax.broadcasted_iota(jnp.int32, (n_slc, tq), 0)
    qb = qpos // SLC_BLK
    cur = jnp.where(jb > qb, -2e38, imp)
    for forced in (0, qb, qb - 1):
        cur = jnp.where(jb == forced, 3e38, cur)
    sel = _topk_member(cur, min(SLC_TOPK, n_slc))
    selb = jnp.where(jb <= qb, jnp.where(sel, 0.0, NEG), NEG).T
    selb_ref[...] = selb.astype(BF16)
    for c in range(tq // NSA_TQ):
        hit = jnp.max(selb[c * NSA_TQ:(c + 1) * NSA_TQ], axis=0, keepdims=True) == 0.0
        used_ref[c] = jnp.where(hit, 1, 0)


def _nsa_cmp(act, kc, vct, batch, seq, col_q, n_heads, group, tq=256):
    Hkv = n_heads // group
    n_cmp = kc.shape[2]
    n_slc = seq // SLC_BLK
    cs = jnp.arange(n_cmp) * CMP_STRIDE
    ss = jnp.arange(n_slc) * SLC_BLK
    overlap = (cs[None, :] <= ss[:, None] + SLC_BLK - 1) & (cs[None, :] + CMP_LEN - 1 >= ss[:, None])
    overlap = overlap & (cs[None, :] + CMP_LEN <= seq)
    ct = overlap.astype(BF16)
    gw = group * HEAD_DIM
    return pl.pallas_call(
        functools.partial(_nsa_cmp_kernel, tq=tq, group=group, n_heads=n_heads),
        grid=(batch, Hkv, seq // tq),
        in_specs=[
            pl.BlockSpec((None, tq, gw), lambda b, g, i: (b, i, col_q + g)),
            pl.BlockSpec((None, None, n_cmp, LANES), lambda b, g, i: (b, g, 0, 0)),
            pl.BlockSpec((None, None, HEAD_DIM, n_cmp), lambda b, g, i: (b, g, 0, 0)),
            pl.BlockSpec((n_slc, n_cmp), lambda b, g, i: (0, 0)),
        ],
        out_specs=[
            pl.BlockSpec((None, tq, gw), lambda b, g, i: (b, i, g)),
            pl.BlockSpec((None, None, tq, n_slc), lambda b, g, i: (b, g, i, 0)),
            pl.BlockSpec((None, None, tq // NSA_TQ, 1, n_slc), lambda b, g, i: (b, g, i, 0, 0)),
        ],
        out_shape=[
            jax.ShapeDtypeStruct((batch, seq, n_heads * HEAD_DIM), F32),
            jax.ShapeDtypeStruct((batch, Hkv, seq, n_slc), BF16),
            jax.ShapeDtypeStruct((batch, Hkv, seq // NSA_TQ, 1, n_slc), jnp.int32),
        ],
        scratch_shapes=[pltpu.VMEM((n_slc, tq), F32)],
        compiler_params=pltpu.CompilerParams(
            dimension_semantics=("parallel", "parallel", "parallel"),
            vmem_limit_bytes=VMEM_LIMIT),
        name="nsa_cmp",
    )(act, kc, vct, ct)


def _nsa_main_kernel(used_ref, q_ref, selb_ref, kvs_ref, kvw_ref, kconst_ref, ocmp_ref, gate_ref,
                     o_ref, ks_ref, vst_ref, kw_ref, vwt_ref, accs_ref, accw_ref, gt_ref, t_ref,
                     p_ref, tiles_ref, *, tq, tk, group, n_heads):
    g = pl.program_id(1)
    qi = pl.program_id(2)
    S = kvs_ref.shape[0]
    R = group * tq
    i0 = qi * tq

    @pl.when(qi == 0)
    def _():
        def body(c, carry):
            rows = pl.ds(pl.multiple_of(c * tk, tk), tk)
            kvs = kvs_ref[rows, :]
            kvw = kvw_ref[rows, :]
            kconst = kconst_ref[...]
            ks_ref[rows, :HEAD_DIM] = kvs[:, :HEAD_DIM]
            ks_ref[rows, HEAD_DIM:LANES] = kconst[:, :HEAD_DIM]
            kw_ref[rows, :] = jnp.concatenate([kvw[:, :HEAD_DIM], kconst[:, :HEAD_DIM]], axis=1)
            vst_ref[:, rows] = kvs.astype(F32).T[HEAD_DIM:, :].astype(BF16)
            vwt_ref[:, rows] = kvw.astype(F32).T[HEAD_DIM:, :].astype(BF16)
            key_blk = (c * tk + lax.broadcasted_iota(jnp.int32, (tk, n_slc), 0)) // SLC_BLK
            onehot = key_blk == lax.broadcasted_iota(jnp.int32, (tk, n_slc), 1)
            ks_ref[rows, LANES:] = jnp.where(onehot, 1.0, 0.0).astype(BF16)
            return carry

        n_slc = ks_ref.shape[1] - LANES
        lax.fori_loop(0, S // tk, body, 0)

    qt = q_ref[...]
    selb = selb_ref[...]
    qs_parts, qw_parts, slope_parts = [], [], []
    for e in range(group):
        slope = jnp.exp2(-8.0 * (group * g + e + 1).astype(F32) / n_heads)
        aug = _aug_const(tq, [slope]).astype(BF16)
        qe = qt[:, e * HEAD_DIM:(e + 1) * HEAD_DIM]
        qw_parts.append(jnp.concatenate([qe, aug], axis=1))
        qs_parts.append(jnp.concatenate([qe, aug, selb], axis=1))
        slope_parts.append(jnp.full((1, tq), slope, F32))
    q_slc = jnp.concatenate(qs_parts, axis=0)
    q_win = jnp.concatenate(qw_parts, axis=0)
    slope_row = jnp.concatenate(slope_parts, axis=1)
    qpos = i0 + lax.broadcasted_iota(jnp.int32, (1, R), 1) % tq

    nd = i0 // tk
    rows = lambda n: pl.ds(pl.multiple_of(n * tk, tk), tk)
    shift = lambda n: slope_row * (n * tk - i0).astype(F32)
    kpos = lambda n: n * tk + lax.broadcasted_iota(jnp.int32, (tk, 1), 0)
    causal = lambda n: kpos(n) <= qpos

    n_tiles = S // tk
    base = ((pl.program_id(0) * pl.num_programs(1) + g) * pl.num_programs(2) + qi) * n_tiles

    def collect(n, cnt):
        tiles_ref[cnt] = n
        return cnt + jnp.where(used_ref[base + n] != 0, 1, 0)

    tiles_ref[0] = nd
    n_used = lax.fori_loop(0, nd, collect, 1)
    assert 2 * tk == NSA_WINDOW and tk % tq == 0
    tw, offs = [], []
    for c in range(3):
        n = nd - 2 + c
        t = _dot_nt(kw_ref[rows(jnp.maximum(n, 0)), :], q_win)
        if c == 0:
            t = jnp.where(qpos - kpos(n) < NSA_WINDOW, t, NEG)
        elif c == 2:
            t = jnp.where(kpos(n) <= qpos, t, NEG)
        tw.append(t)
        offs.append(shift(n) + jnp.where(n >= 0, 0.0, NEG))
    mw = functools.reduce(jnp.maximum,
                          [jnp.max(t, axis=0, keepdims=True) + off for t, off in zip(tw, offs)])
    lw = jnp.zeros((1, R), F32)
    o_w = jnp.zeros((HEAD_DIM, R), F32)
    for c in range(3):
        p = jnp.exp(tw[c] - (mw - offs[c]))
        lw = lw + jnp.sum(p, axis=0, keepdims=True)
        o_w = o_w + _dot(vwt_ref[:, rows(jnp.maximum(nd - 2 + c, 0))], p.astype(BF16))
    accw_ref[...] = o_w

    slc = _Stream(q_slc, lambda n: ks_ref[rows(n), :], lambda n: vst_ref[:, rows(n)],
                  accs_ref, t_ref, p_ref, shift, mask=causal)
    ((_, ls),) = _flash_sweep([slc], n_used, lambda pos: tiles_ref[pos])

    gz = gate_ref[...]
    gt_ref[...] = jax.nn.sigmoid(gz).T
    oc_t = ocmp_ref[...].T
    outs = []
    for e in range(group):
        head = group * g + e
        cols = slice(e * tq, (e + 1) * tq)
        o_s = accs_ref[:, cols] / ls[:, cols]
        o_w = accw_ref[:, cols] / lw[:, cols]
        o_c = oc_t[e * HEAD_DIM:(e + 1) * HEAD_DIM, :]
        g_c = gt_ref[pl.ds(3 * head, 1), :]
        g_s = gt_ref[pl.ds(3 * head + 1, 1), :]
        g_w = gt_ref[pl.ds(3 * head + 2, 1), :]
        outs.append(g_c * o_c + g_s * o_s + g_w * o_w)
    o_ref[...] = jnp.concatenate(outs, axis=0).T


def _nsa_main(act, selb, used, ocmp, gates, batch, seq, col_q, col_kvs, col_kvw, col_gate, n_heads,
              group):
    tq, tk = NSA_TQ, NSA_TK
    Hkv = n_heads // group
    n_slc = seq // SLC_BLK
    gw = group * HEAD_DIM
    R = group * tq
    jj = jnp.arange(tk, dtype=F32)
    kconst = jnp.zeros((tk, HEAD_DIM), F32).at[:, 0].set(jj).astype(BF16)
    blk_per_tile = tk // SLC_BLK
    used_tiles = jnp.max(used.reshape(batch, Hkv, seq // tq, n_slc // blk_per_tile, blk_per_tile),
                         axis=-1).reshape(-1).astype(jnp.int32)
    grid_spec = pltpu.PrefetchScalarGridSpec(
        num_scalar_prefetch=1,
        grid=(batch, Hkv, seq // tq),
        in_specs=[
            pl.BlockSpec((None, tq, gw), lambda b, g, i, u: (b, i, col_q + g)),
            pl.BlockSpec((None, None, tq, n_slc), lambda b, g, i, u: (b, g, i, 0)),
            pl.BlockSpec((None, seq, LANES), lambda b, g, i, u: (b, 0, col_kvs + g)),
            pl.BlockSpec((None, seq, LANES), lambda b, g, i, u: (b, 0, col_kvw + g)),
            pl.BlockSpec((tk, HEAD_DIM), lambda b, g, i, u: (0, 0)),
            pl.BlockSpec((None, tq, gw), lambda b, g, i, u: (b, i, g)),
            pl.BlockSpec((None, tq, LANES), lambda b, g, i, u: (b, i, col_gate)),
        ],
        out_specs=pl.BlockSpec((None, tq, gw), lambda b, g, i, u: (b, i, g)),
        scratch_shapes=[
            pltpu.VMEM((seq, LANES + n_slc), BF16),
            pltpu.VMEM((HEAD_DIM, seq), BF16),
            pltpu.VMEM((seq, LANES), BF16),
            pltpu.VMEM((HEAD_DIM, seq), BF16),
            pltpu.VMEM((HEAD_DIM, R), F32),
            pltpu.VMEM((HEAD_DIM, R), F32),
            pltpu.VMEM((LANES, tq), F32),
            pltpu.VMEM((FLASH_UNROLL, tk, R), F32), pltpu.VMEM((FLASH_STAGED, tk, R), BF16),
            pltpu.SMEM((seq // tk,), jnp.int32),
        ],
    )
    return pl.pallas_call(
        functools.partial(_nsa_main_kernel, tq=tq, tk=tk, group=group, n_heads=n_heads),
        grid_spec=grid_spec,
        out_shape=jax.ShapeDtypeStruct((batch, seq, n_heads * HEAD_DIM), F32),
        compiler_params=pltpu.CompilerParams(
            dimension_semantics=("parallel", "parallel", "arbitrary"),
            vmem_limit_bytes=VMEM_LIMIT),
        name="nsa_main",
    )(used_tiles, act, selb, act, act, kconst, ocmp, gates)


def _tile_gain(g, width):
    return jnp.tile(g.astype(F32), width // HEAD_DIM)


def _even_layer(x, ln, w_in, qkn, lam, subln, w_out, layer):
    B, S, D = x.shape
    M = B * S
    x2d = x.reshape(M, D)
    W = 512
    aq, ak, av, az, bq, bk, bv, bz = [w_in[:, i * W:(i + 1) * W] for i in range(8)]
    scale = HEAD_DIM ** -0.5
    ones, zeros = jnp.ones((W,), F32), jnp.zeros((W,), F32)
    w_b = jnp.concatenate([bq, bk, bv], axis=1).astype(BF16)
    gain_b = jnp.concatenate([_tile_gain(qkn[2], W) * scale, _tile_gain(qkn[3], W), ones])
    mask_b = np.repeat([1.0, 1.0, 0.0], W)
    act_b = _project(x2d, ln, w_b, gain_b, mask_b, BF16)
    w_a = jnp.concatenate([aq, ak, av, az, bz], axis=1).astype(BF16)
    gain_a = jnp.concatenate([_tile_gain(qkn[0], W) * scale, _tile_gain(qkn[1], W), ones, ones, ones])
    mask_a = np.repeat([1.0, 1.0, 0.0, 0.0, 0.0], W)
    act_a = _project(x2d, ln, w_a, gain_a, mask_a, F32)

    lam_init = 0.8 - 0.6 * math.exp(-0.3 * layer)
    lf = lam.astype(F32)
    lam_f = jnp.exp(jnp.sum(lf[0] * lf[1])) - jnp.exp(jnp.sum(lf[2] * lf[3])) + lam_init
    ob = _diff_attention(act_b.reshape(B, S, 3 * W), lam_f, subln, B, S, 0, 4, 8, 4, 1.0 - lam_init)
    oa = _dilated_attention(act_a.reshape(B, S, 5 * W), B, S, 0, 4, 8, 8)
    out = _out_project(x2d, oa.reshape(M, W), ob.reshape(M, W), act_a, 3, 4, w_out)
    return out.reshape(B, S, D)


def _odd_layer(x, ln, w_in, qkn, phi_pe, phi_w1, phi_w2, w_out):
    B, S, D = x.shape
    M = B * S
    x2d = x.reshape(M, D)
    W, KW = 512, 128
    offs = [0]
    for s in (W, W, W, W, W, KW, KW, KW, KW, KW, KW, W, 24):
        offs.append(offs[-1] + s)
    cq, ck, cv, cz, dq, dkc, dvc, dks, dvs, dkw, dvw, dz, dg = [
        w_in[:, offs[i]:offs[i + 1]] for i in range(13)]
    scale = HEAD_DIM ** -0.5
    hd = HEAD_DIM
    ones = lambda n: jnp.ones((n,), F32)
    zeros = lambda n: jnp.zeros((n,), F32)
    kv_pair = lambda k, v: jnp.concatenate([k[:, :hd], v[:, :hd], k[:, hd:], v[:, hd:]], axis=1)
    w_b = jnp.concatenate([cq, ck, cv, dq, kv_pair(dks, dvs), kv_pair(dkw, dvw), dkc, dvc,
                           jnp.zeros((D, 2 * KW), w_in.dtype)], axis=1).astype(BF16)
    kv_gain = lambda g: jnp.concatenate([g, ones(hd), g, ones(hd)])
    kv_mask = np.repeat([1.0, 0.0, 1.0, 0.0], hd)
    gain_b = jnp.concatenate([_tile_gain(qkn[0], W) * scale, _tile_gain(qkn[1], W), ones(W),
                              _tile_gain(qkn[2], W) * scale, kv_gain(qkn[4].astype(F32)),
                              kv_gain(qkn[5].astype(F32)), ones(4 * KW)])
    mask_b = np.concatenate([np.repeat([1.0, 1.0, 0.0, 1.0], W), kv_mask, kv_mask, np.zeros(4 * KW)])
    act_b = _project(x2d, ln, w_b, gain_b, mask_b, BF16).reshape(B, S, 6 * W)
    w_g = jnp.concatenate([cz, dz, dg, jnp.zeros((D, W - 24), w_in.dtype)], axis=1).astype(BF16)
    act_g = _project(x2d, ln, w_g, ones(3 * W), np.zeros(3 * W), F32)

    oc = _moba_attention(act_b, B, S, 0, 4, 8, 8, 256, 3)

    def chunks(t):
        t = t.reshape(B, S // CMP_STRIDE, CMP_STRIDE, 2, hd)
        return jnp.transpose(t, (0, 3, 1, 2, 4)).reshape(B, 2, S // CMP_STRIDE, CMP_STRIDE * hd)

    kc = _compress(chunks(act_b[:, :, 20 * KW:21 * KW]), phi_pe[0], phi_w1[0], phi_w2[0], qkn[3], True)
    vct = _compress(chunks(act_b[:, :, 21 * KW:22 * KW]), phi_pe[1], phi_w1[1], phi_w2[1], qkn[3], False)
    ocmp, selb, used = _nsa_cmp(act_b, kc, vct, B, S, 6, 8, 4)
    od = _nsa_main(act_b, selb, used, ocmp, act_g.reshape(B, S, 3 * W), B, S, 6, 16, 18, 8, 8, 4)
    out = _out_project(x2d, oc.reshape(M, W), od.reshape(M, W), act_g, 0, 1, w_out)
    return out.reshape(B, S, D)


def kernel(x, ln_e, w_in_e, qkn_e, lam_e, subln_e, w_out_e, ln_o, w_in_o, qkn_o, phi_pe, phi_w1, phi_w2, w_out_o):
    n_layers = ln_e.shape[0] + ln_o.shape[0]
    for layer in range(n_layers):
        i = layer // 2
        if layer % 2 == 0:
            x = _even_layer(x, ln_e[i], w_in_e[i], qkn_e[i], lam_e[i], subln_e[i], w_out_e[i], layer)
        else:
            x = _odd_layer(x, ln_o[i], w_in_o[i], qkn_o[i], phi_pe[i], phi_w1[i], phi_w2[i], w_out_o[i])
    return x
```

```python
import functools
import math

import jax
import jax.numpy as jnp
import numpy as np
from jax import lax
from jax.experimental import pallas as pl
from jax.experimental.pallas import tpu as pltpu

HEAD_DIM = 64
LANES = 128
RMS_EPS = 1e-6
NEG = -2e30
M_INIT = -1e30
VMEM_LIMIT = 56 * 1024 * 1024

F32 = jnp.float32
BF16 = jnp.bfloat16

_NT = (((1,), (1,)), ((), ()))


def _dot(a, b):
    return jnp.dot(a, b, preferred_element_type=F32)


def _dot_nt(a, b):
    return lax.dot_general(a, b, _NT, preferred_element_type=F32)


def _alibi_slopes(n):
    return [2.0 ** (-8.0 * (i + 1) / n) for i in range(n)]


def _split_bf16(x):
    hi = x.astype(BF16)
    lo = (x - hi.astype(F32)).astype(BF16)
    return hi, lo


NORM_CHUNK = 256


def _proj_kernel(x_ref, ln_ref, w_ref, gain_ref, nmask_ref, bd_ref, o_ref, *, norm_chunks):
    x = x_ref[...]
    ms = jnp.mean(x * x, axis=-1, keepdims=True)
    h = (x * lax.rsqrt(ms + RMS_EPS) * ln_ref[...]).astype(BF16)
    bd = bd_ref[...]
    col = lambda c: slice(c * NORM_CHUNK, (c + 1) * NORM_CHUNK)
    ys = {}
    for c, normed in enumerate(norm_chunks):
        y = _dot(h, w_ref[:, col(c)])
        if normed:
            ys[c] = y
        else:
            o_ref[:, col(c)] = y.astype(o_ref.dtype)
    for c, y in ys.items():
        hi, lo = _split_bf16(y * y)
        ms = _dot(hi, bd) + _dot(lo, bd)
        yn = y * lax.rsqrt(ms + RMS_EPS) * gain_ref[:, col(c)]
        o_ref[:, col(c)] = jnp.where(nmask_ref[:, col(c)] > 0.5, yn, y).astype(o_ref.dtype)


def _project(x2d, ln, w, gain, nmask, out_dtype, tm=1024):
    M, D = x2d.shape
    N = w.shape[1]
    assert M % tm == 0 and N % NORM_CHUNK == 0
    norm_chunks = [bool(c) for c in np.asarray(nmask).reshape(-1, NORM_CHUNK).max(axis=1) > 0.5]
    nmask = jnp.asarray(nmask, F32)
    r = jnp.arange(NORM_CHUNK) // HEAD_DIM
    bd = jnp.where(r[:, None] == r[None, :], 1.0 / HEAD_DIM, 0.0).astype(BF16)
    return pl.pallas_call(
        functools.partial(_proj_kernel, norm_chunks=tuple(norm_chunks)),
        grid=(M // tm,),
        in_specs=[
            pl.BlockSpec((tm, D), lambda i: (i, 0)),
            pl.BlockSpec((1, D), lambda i: (0, 0)),
            pl.BlockSpec((D, N), lambda i: (0, 0)),
            pl.BlockSpec((1, N), lambda i: (0, 0)),
            pl.BlockSpec((1, N), lambda i: (0, 0)),
            pl.BlockSpec((NORM_CHUNK, NORM_CHUNK), lambda i: (0, 0)),
        ],
        out_specs=pl.BlockSpec((tm, N), lambda i: (i, 0)),
        out_shape=jax.ShapeDtypeStruct((M, N), out_dtype),
        compiler_params=pltpu.CompilerParams(
            dimension_semantics=("parallel",), vmem_limit_bytes=VMEM_LIMIT),
        name="proj",
    )(x2d, ln.reshape(1, D).astype(F32), w, gain.reshape(1, N).astype(F32),
      nmask.reshape(1, N).astype(F32), bd)


def _outproj_kernel(x_ref, oa_ref, ob_ref, za_ref, zb_ref, wa_ref, wb_ref, o_ref):
    za = za_ref[...]
    zb = zb_ref[...]
    ma = (oa_ref[...] * (za * jax.nn.sigmoid(za))).astype(BF16)
    mb = (ob_ref[...] * (zb * jax.nn.sigmoid(zb))).astype(BF16)
    o_ref[...] = x_ref[...] + _dot(ma, wa_ref[...]) + _dot(mb, wb_ref[...])


def _out_project(x2d, oa, ob, z, za_blk, zb_blk, w_out, tm=512):
    M, D = x2d.shape
    W = oa.shape[1]
    wa = w_out[:W].astype(BF16)
    wb = w_out[W:].astype(BF16)
    return pl.pallas_call(
        _outproj_kernel,
        grid=(M // tm,),
        in_specs=[
            pl.BlockSpec((tm, D), lambda i: (i, 0)),
            pl.BlockSpec((tm, W), lambda i: (i, 0)),
            pl.BlockSpec((tm, W), lambda i: (i, 0)),
            pl.BlockSpec((tm, W), lambda i: (i, za_blk)),
            pl.BlockSpec((tm, W), lambda i: (i, zb_blk)),
            pl.BlockSpec((W, D), lambda i: (0, 0)),
            pl.BlockSpec((W, D), lambda i: (0, 0)),
        ],
        out_specs=pl.BlockSpec((tm, D), lambda i: (i, 0)),
        out_shape=jax.ShapeDtypeStruct((M, D), F32),
        compiler_params=pltpu.CompilerParams(
            dimension_semantics=("parallel",), vmem_limit_bytes=VMEM_LIMIT),
        name="outproj",
    )(x2d, oa, ob, z, z, wa, wb)


class _Stream:
    def __init__(self, q, k_tile, vt_tile, acc_ref, t_ref, p_ref, shift, bias=None, mask=None,
                 mask_every_tile=False):
        self.q, self.k_tile, self.vt_tile = q, k_tile, vt_tile
        self.acc_ref, self.t_ref, self.p_ref = acc_ref, t_ref, p_ref
        self.shift, self.bias, self.mask = shift, bias, mask
        self.mask_every_tile = mask_every_tile

    def scores(self, n):
        return _dot_nt(self.k_tile(n), self.q)

    def softmax(self, t, n, m, l, pad, masked):
        if masked:
            t = jnp.where(self.mask(n), t, NEG)
        off = self.shift(n) + pad
        if self.bias is not None:
            off = off + self.bias(n)
        m_new = jnp.maximum(m, jnp.max(t, axis=0, keepdims=True) + off)
        p = jnp.exp(t - (m_new - off))
        alpha = jnp.exp(m - m_new)
        return m_new, alpha * l + jnp.sum(p, axis=0, keepdims=True), alpha, p.astype(BF16)


FLASH_UNROLL = 2
FLASH_STAGED = 1
FLASH_QK_FIRST = False


def _flash_sweep(streams, count, tile_of):
    U, D = FLASH_UNROLL, FLASH_STAGED
    R = streams[0].q.shape[0]
    last = count - 1
    W = 2 + D
    state = []
    for s in streams:
        s.acc_ref[...] = jnp.zeros_like(s.acc_ref)
        s.p_ref[...] = jnp.zeros_like(s.p_ref)
        for k in range(U):
            s.t_ref[k] = s.scores(tile_of(jnp.minimum(k, last)))
        state += [jnp.full((1, R), M_INIT, F32), jnp.zeros((1, R), F32)]
        state += [jnp.ones((1, R), F32)] * D

    def flush(n_prev, state):
        pvs = [[_dot(s.vt_tile(n_prev[d]), s.p_ref[d]) for d in range(D)] for s in streams]
        for i, s in enumerate(streams):
            acc = s.acc_ref[...]
            for d in range(D):
                acc = state[W * i + 2 + d] * acc + pvs[i][d]
            s.acc_ref[...] = acc

    def body(g, carry, first_body):
        n_prev, state = list(carry[:D]), list(carry[D:])
        base = U * g
        if FLASH_QK_FIRST:
            nxt = [[s.scores(tile_of(jnp.minimum(base + U + k, last))) for k in range(U)]
                   for s in streams]
            flush(n_prev, state)
        else:
            flush(n_prev, state)
            nxt = [[s.scores(tile_of(jnp.minimum(base + U + k, last))) for k in range(U)]
                   for s in streams]
        for k in range(U):
            n = tile_of(jnp.minimum(base + k, last))
            pad = jnp.where(base + k < count, 0.0, NEG)
            d = k - (U - D)
            for i, s in enumerate(streams):
                m, l = state[W * i:W * i + 2]
                masked = s.mask is not None and (s.mask_every_tile or (first_body and k == 0))
                m, l, a, p = s.softmax(s.t_ref[k], n, m, l, pad, masked)
                state[W * i:W * i + 2] = [m, l]
                if d < 0:
                    s.acc_ref[...] = a * s.acc_ref[...] + _dot(s.vt_tile(n), p)
                else:
                    state[W * i + 2 + d] = a
                    s.p_ref[d] = p
            if d >= 0:
                n_prev[d] = n
        for i, s in enumerate(streams):
            for k in range(U):
                s.t_ref[k] = nxt[i][k]
        return tuple(n_prev) + tuple(state)

    carry = body(0, (tile_of(0),) * D + tuple(state), True)
    carry = lax.fori_loop(1, (count + U - 1) // U, lambda g, c: body(g, c, False), carry)
    flush(carry[:D], carry[D:])
    return [(carry[D + W * i], carry[D + W * i + 1]) for i in range(len(streams))]


def _aug_const(rows, col_vals):
    lane = lax.broadcasted_iota(jnp.int32, (rows, HEAD_DIM), 1)
    out = jnp.zeros((rows, HEAD_DIM), F32)
    for c, v in enumerate(col_vals):
        out = jnp.where(lane == c, v, out)
    return out


def _diff_kernel(lam_ref, q_ref, k_ref, v_ref, subln_ref, o_ref,
                 k1_ref, k2_ref, vt_ref, acc1_ref, acc2_ref, t_ref, p_ref, *, tq, n_heads, out_scale):
    h = pl.program_id(1)
    qi = pl.program_id(2)
    S = k_ref.shape[0]
    tk = tq
    slope = jnp.exp2(-8.0 * (h + 1).astype(F32) / n_heads)

    @pl.when(qi == 0)
    def _():
        jj = lax.broadcasted_iota(jnp.int32, (tk, 1), 0).astype(F32)
        kaug = _aug_const(tk, [jj]).astype(BF16)

        def body(c, carry):
            rows = pl.ds(pl.multiple_of(c * tk, tk), tk)
            kc = k_ref[rows, :]
            k1_ref[rows, :] = jnp.concatenate([kc[:, :HEAD_DIM], kaug], axis=1)
            k2_ref[rows, :] = jnp.concatenate([kc[:, HEAD_DIM:], kaug], axis=1)
            vt_ref[:, rows] = v_ref[rows, :].astype(F32).T.astype(BF16)
            return carry

        lax.fori_loop(0, S // tk, body, 0)

    qt = q_ref[...]
    qaug = _aug_const(tq, [slope]).astype(BF16)
    q1 = jnp.concatenate([qt[:, :HEAD_DIM], qaug], axis=1)
    q2 = jnp.concatenate([qt[:, HEAD_DIM:], qaug], axis=1)

    rows = lambda n: pl.ds(pl.multiple_of(n * tk, tk), tk)
    qpos = qi * tq + lax.broadcasted_iota(jnp.int32, (1, tq), 1)
    causal = lambda n: n * tk + lax.broadcasted_iota(jnp.int32, (tk, 1), 0) <= qpos
    shift = lambda n: slope * ((n - qi) * tk).astype(F32)
    vt_tile = lambda n: vt_ref[:, rows(n)]
    streams = [
        _Stream(q1, lambda n: k1_ref[rows(n), :], vt_tile, acc1_ref, t_ref.at[0], p_ref.at[0],
                shift, mask=causal),
        _Stream(q2, lambda n: k2_ref[rows(n), :], vt_tile, acc2_ref, t_ref.at[1], p_ref.at[1],
                shift, mask=causal),
    ]
    (m1, l1), (m2, l2) = _flash_sweep(streams, qi + 1, lambda pos: jnp.where(pos == 0, qi, pos - 1))

    d = acc1_ref[...] / l1 - lam_ref[0] * (acc2_ref[...] / l2)
    r = lax.rsqrt(jnp.mean(d * d, axis=0, keepdims=True) + RMS_EPS)
    o_ref[...] = (d * r).T * (subln_ref[...] * out_scale)


def _diff_attention(qkv, lam_f, subln, batch, seq, col_q, col_k, col_v, n_heads, out_scale, tq=256):
    dv = 2 * HEAD_DIM
    kern = functools.partial(_diff_kernel, tq=tq, n_heads=n_heads, out_scale=out_scale)
    grid_spec = pltpu.PrefetchScalarGridSpec(
        num_scalar_prefetch=1,
        grid=(batch, n_heads, seq // tq),
        in_specs=[
            pl.BlockSpec((None, tq, dv), lambda b, h, i, s: (b, i, col_q + h)),
            pl.BlockSpec((None, seq, dv), lambda b, h, i, s: (b, 0, col_k + h)),
            pl.BlockSpec((None, seq, dv), lambda b, h, i, s: (b, 0, col_v + h)),
            pl.BlockSpec((1, dv), lambda b, h, i, s: (0, 0)),
        ],
        out_specs=pl.BlockSpec((None, tq, dv), lambda b, h, i, s: (b, i, h)),
        scratch_shapes=[
            pltpu.VMEM((seq, LANES), BF16), pltpu.VMEM((seq, LANES), BF16),
            pltpu.VMEM((dv, seq), BF16),
            pltpu.VMEM((dv, tq), F32), pltpu.VMEM((dv, tq), F32),
            pltpu.VMEM((2, FLASH_UNROLL, tq, tq), F32), pltpu.VMEM((2, FLASH_STAGED, tq, tq), BF16),
        ],
    )
    return pl.pallas_call(
        kern,
        grid_spec=grid_spec,
        out_shape=jax.ShapeDtypeStruct((batch, seq, n_heads * dv), F32),
        compiler_params=pltpu.CompilerParams(
            dimension_semantics=("parallel", "parallel", "arbitrary"),
            vmem_limit_bytes=VMEM_LIMIT),
        name="diff_attn",
    )(lam_f.reshape(1).astype(F32), qkv, qkv, qkv, subln.reshape(1, dv).astype(F32))


BAND = 128
DILATIONS = (1, 4, 16)


def _dilated_kernel(q_ref, kp_ref, kc_ref, vp_ref, vc_ref, o_ref, kbuf_ref, vbuf_ref, st_ref,
                    *, chunk, n_heads):
    pair = pl.program_id(1)
    ci = pl.program_id(2)
    kbuf_ref[:chunk, :] = kp_ref[...]
    kbuf_ref[chunk:, :] = kc_ref[...]
    vbuf_ref[:chunk, :] = vp_ref[...]
    vbuf_ref[chunk:, :] = vc_ref[...]

    qi = lax.broadcasted_iota(jnp.int32, (BAND, 2 * BAND), 0)
    kj = lax.broadcasted_iota(jnp.int32, (BAND, 2 * BAND), 1)
    dist = qi + BAND - kj
    in_band = jnp.abs(2 * dist - BAND) <= BAND
    dist_f = dist.astype(F32)
    key_col = lax.broadcasted_iota(jnp.int32, (1, 2 * BAND), 1)
    slopes = [jnp.exp2(-8.0 * (2 * pair + e + 1).astype(F32) / n_heads) for e in range(2)]
    lane_q = lax.broadcasted_iota(jnp.int32, (BAND, LANES), 1)
    own_half = [lane_q < HEAD_DIM, lane_q >= HEAD_DIM]

    for pi, dil in enumerate(DILATIONS):
        span = dil * BAND
        n_blk = chunk // span

        def block(idx, carry, dil=dil, span=span, n_blk=n_blk, pi=pi):
            r = idx // n_blk
            b = idx % n_blk
            q0 = r + b * span
            qb = q_ref[pl.ds(q0, BAND, stride=dil), :]
            kb = kbuf_ref[pl.ds(chunk + q0 - span, 2 * BAND, stride=dil), :]
            vb = vbuf_ref[pl.ds(chunk + q0 - span, 2 * BAND, stride=dil), :]
            first = jnp.logical_and(ci == 0, b == 0)
            before_start = jnp.where(key_col < jnp.where(first, BAND, 0), NEG, 0.0)
            kb16 = kb.astype(BF16)
            vb16 = vb.astype(BF16)
            packed = []
            for e in range(2):
                qe = jnp.where(own_half[e], qb, 0.0).astype(BF16)
                s = _dot_nt(qe, kb16)
                s = jnp.where(in_band, s - (slopes[e] * dil) * dist_f, NEG) + before_start
                m = jnp.max(s, axis=1, keepdims=True)
                p = jnp.exp(s - m)
                l = jnp.sum(p, axis=1, keepdims=True)
                o = _dot(p.astype(BF16), vb16) / l
                packed.append(jnp.where(own_half[e], o, m + jnp.log(l)))
            for e in range(2):
                st_ref[pi, e, pl.ds(q0, BAND, stride=dil), :] = packed[e]
            return carry

        lax.fori_loop(0, dil * n_blk, block, 0, unroll=8)

    rows_per = 256
    lane = lax.broadcasted_iota(jnp.int32, (rows_per, LANES), 1)

    def merge(c, carry):
        rows = pl.ds(pl.multiple_of(c * rows_per, rows_per), rows_per)
        outs = []
        for e in range(2):
            xs = [st_ref[pi, e, rows, :] for pi in range(len(DILATIONS))]
            top = functools.reduce(jnp.maximum, xs)
            num = jnp.zeros((rows_per, LANES), F32)
            den = jnp.zeros((rows_per, LANES), F32)
            for x in xs:
                w = pltpu.roll(jnp.exp(x - top), HEAD_DIM, axis=1)
                num = num + w * x
                den = den + w
            outs.append(num / den)
        o_ref[rows, :] = jnp.where(lane < HEAD_DIM, outs[0], outs[1])
        return carry

    lax.fori_loop(0, chunk // rows_per, merge, 0)


def _dilated_attention(act, batch, seq, col_q, col_k, col_v, n_heads, chunk=2048):
    assert chunk % (max(DILATIONS) * BAND) == 0 and seq % chunk == 0
    prev = lambda c: jnp.maximum(c - 1, 0)
    return pl.pallas_call(
        functools.partial(_dilated_kernel, chunk=chunk, n_heads=n_heads),
        grid=(batch, n_heads // 2, seq // chunk),
        in_specs=[
            pl.BlockSpec((None, chunk, LANES), lambda b, p, c: (b, c, col_q + p)),
            pl.BlockSpec((None, chunk, LANES), lambda b, p, c: (b, prev(c), col_k + p)),
            pl.BlockSpec((None, chunk, LANES), lambda b, p, c: (b, c, col_k + p)),
            pl.BlockSpec((None, chunk, LANES), lambda b, p, c: (b, prev(c), col_v + p)),
            pl.BlockSpec((None, chunk, LANES), lambda b, p, c: (b, c, col_v + p)),
        ],
        out_specs=pl.BlockSpec((None, chunk, LANES), lambda b, p, c: (b, c, p)),
        out_shape=jax.ShapeDtypeStruct((batch, seq, n_heads * HEAD_DIM), F32),
        scratch_shapes=[
            pltpu.VMEM((2 * chunk, LANES), F32),
            pltpu.VMEM((2 * chunk, LANES), F32),
            pltpu.VMEM((len(DILATIONS), 2, chunk, LANES), F32),
        ],
        compiler_params=pltpu.CompilerParams(
            dimension_semantics=("parallel", "parallel", "arbitrary"),
            vmem_limit_bytes=VMEM_LIMIT),
        name="dilated_attn",
    )(act, act, act, act, act)


TAKEN = -3e38


def _topk_member(cur, k):
    n = cur.shape[0]
    row = lax.broadcasted_iota(jnp.int32, cur.shape, 0)
    for _ in range(k):
        mx = jnp.max(cur, axis=0, keepdims=True)
        first = jnp.min(jnp.where(cur == mx, row, n), axis=0, keepdims=True)
        cur = jnp.where(row == first, TAKEN, cur)
    return cur == TAKEN


def _moba_kernel(q_ref, k_ref, v_ref, o_ref, kaug_ref, vt_ref, kmean_ref, selb_ref, acc_ref,
                 t_ref, p_ref, *, blk, topk, n_heads):
    pair = pl.program_id(1)
    qi = pl.program_id(2)
    S = k_ref.shape[0]
    nblk = S // blk
    tq = tk = blk

    @pl.when(qi == 0)
    def _():
        jj = lax.broadcasted_iota(jnp.int32, (tk, 1), 0).astype(F32)
        kaug = _aug_const(tk, [jj]).astype(BF16)

        def body(c, carry):
            rows = pl.ds(pl.multiple_of(c * tk, tk), tk)
            kc = k_ref[rows, :]
            vc = v_ref[rows, :].astype(F32).T.astype(BF16)
            for e in range(2):
                kaug_ref[e, rows, :] = jnp.concatenate(
                    [kc[:, e * HEAD_DIM:(e + 1) * HEAD_DIM], kaug], axis=1)
                vt_ref[e, :, rows] = vc[e * HEAD_DIM:(e + 1) * HEAD_DIM, :]
            kmean_ref[pl.ds(c, 1), :] = jnp.mean(kc.astype(F32), axis=0, keepdims=True)
            return carry

        lax.fori_loop(0, nblk, body, 0)

    qt = q_ref[...]
    blk_row = lax.broadcasted_iota(jnp.int32, (nblk, tq), 0)
    past = blk_row < qi
    qs, slopes = [], []
    for e in range(2):
        slope = jnp.exp2(-8.0 * (2 * pair + e + 1).astype(F32) / n_heads)
        qe = qt[:, e * HEAD_DIM:(e + 1) * HEAD_DIM]
        km_hi, km_lo = _split_bf16(kmean_ref[:, e * HEAD_DIM:(e + 1) * HEAD_DIM])
        gate = _dot_nt(km_hi, qe) + _dot_nt(km_lo, qe)
        sel = _topk_member(jnp.where(past, gate, -2e38), topk)
        selb_ref[e] = jnp.where(blk_row == qi, 0.0, jnp.where(past, jnp.where(sel, 0.0, NEG), NEG))
        qs.append(jnp.concatenate([qe, _aug_const(tq, [slope]).astype(BF16)], axis=1))
        slopes.append(slope)

    rows = lambda n: pl.ds(pl.multiple_of(n * tk, tk), tk)
    qpos = qi * tq + lax.broadcasted_iota(jnp.int32, (1, tq), 1)
    causal = lambda n: n * tk + lax.broadcasted_iota(jnp.int32, (tk, 1), 0) <= qpos
    streams = [
        _Stream(qs[e],
                functools.partial(lambda n, e: kaug_ref[e, rows(n), :], e=e),
                functools.partial(lambda n, e: vt_ref[e, :, rows(n)], e=e),
                acc_ref.at[e], t_ref.at[e], p_ref.at[e],
                functools.partial(lambda n, e: slopes[e] * ((n - qi) * tk).astype(F32), e=e),
                bias=functools.partial(lambda n, e: selb_ref[e, pl.ds(n, 1), :], e=e),
                mask=causal)
        for e in range(2)]
    stats = _flash_sweep(streams, qi + 1, lambda pos: jnp.where(pos == 0, qi, pos - 1))
    o_ref[...] = jnp.concatenate([acc_ref[e] / stats[e][1] for e in range(2)], axis=0).T


def _moba_attention(qkv, batch, seq, col_q, col_k, col_v, n_heads, blk, topk):
    kern = functools.partial(_moba_kernel, blk=blk, topk=topk, n_heads=n_heads)
    nblk = seq // blk
    return pl.pallas_call(
        kern,
        grid=(batch, n_heads // 2, nblk),
        in_specs=[
            pl.BlockSpec((None, blk, LANES), lambda b, p, i: (b, i, col_q + p)),
            pl.BlockSpec((None, seq, LANES), lambda b, p, i: (b, 0, col_k + p)),
            pl.BlockSpec((None, seq, LANES), lambda b, p, i: (b, 0, col_v + p)),
        ],
        out_specs=pl.BlockSpec((None, blk, LANES), lambda b, p, i: (b, i, p)),
        out_shape=jax.ShapeDtypeStruct((batch, seq, n_heads * HEAD_DIM), F32),
        scratch_shapes=[
            pltpu.VMEM((2, seq, LANES), BF16),
            pltpu.VMEM((2, HEAD_DIM, seq), BF16),
            pltpu.VMEM((nblk, LANES), F32),
            pltpu.VMEM((2, nblk, blk), F32),
            pltpu.VMEM((2, HEAD_DIM, blk), F32),
            pltpu.VMEM((2, FLASH_UNROLL, blk, blk), F32),
            pltpu.VMEM((2, FLASH_STAGED, blk, blk), BF16),
        ],
        compiler_params=pltpu.CompilerParams(
            dimension_semantics=("parallel", "parallel", "arbitrary"),
            vmem_limit_bytes=VMEM_LIMIT),
        name="moba_attn",
    )(qkv, qkv, qkv)


CMP_STRIDE = 16
CMP_LEN = 32
SLC_BLK = 64
SLC_TOPK = 16
NSA_WINDOW = 512
NSA_TQ = 256
NSA_TK = 256


def _compress_kernel(c_ref, pe_ref, w1_ref, w2_ref, g_ref, o_ref, *, is_key):
    half = c_ref.shape[1]
    n = c_ref.shape[0]
    c = c_ref[...]
    w1 = w1_ref[...]
    a_lo = _dot(c, w1[:half])
    a_hi = _dot(c, w1[half:])
    pe_hi, pe_lo = _split_bf16(pe_ref[...])
    c0 = (_dot(pe_hi, w1) + _dot(pe_lo, w1))[0:1]
    y = a_lo + pltpu.roll(a_hi, n - 1, axis=0) + c0
    hid = (y * jax.nn.sigmoid(y)).astype(BF16)
    if is_key:
        z = _dot(hid, w2_ref[...])
    else:
        z = _dot(hid.astype(F32), w2_ref[...].astype(F32))
    if is_key:
        ms = jnp.sum(z * z, axis=-1, keepdims=True) * (1.0 / HEAD_DIM)
        kn = z * lax.rsqrt(ms + RMS_EPS) * g_ref[...]
        i = lax.broadcasted_iota(jnp.int32, (n, 1), 0)
        aug = _aug_const(n, [(i // 16 * 256).astype(F32), (i % 16 * 16).astype(F32), 1.0])
        o_ref[...] = jnp.concatenate([kn[:, :HEAD_DIM], aug], axis=1).astype(BF16)
    else:
        o_ref[...] = z.T[:HEAD_DIM, :].astype(BF16)


def _compress(chunks, pe, w1, w2, gain, is_key):
    B, n, half = chunks.shape
    Hkv = half // (CMP_STRIDE * HEAD_DIM)
    hidden = w1.shape[1]

    def per_head(t, tail):
        t = t.reshape(2, CMP_STRIDE, 1, HEAD_DIM, *tail)
        onehot = jnp.eye(Hkv, dtype=t.dtype).reshape(Hkv, 1, 1, Hkv, 1, *([1] * len(tail)))
        return (onehot * t[None]).reshape(Hkv, 2 * half, *tail)

    pe8 = jnp.broadcast_to(per_head(pe.reshape(-1).astype(F32), ())[:, None, :], (Hkv, 8, 2 * half))
    w1 = per_head(w1, (hidden,))
    w2p = jnp.pad(w2, ((0, 0), (0, LANES - w2.shape[1]))).astype(BF16)
    g = jnp.pad(gain.astype(F32), (0, LANES - HEAD_DIM)).reshape(1, LANES)
    if is_key:
        out_shape = jax.ShapeDtypeStruct((B, Hkv, n, LANES), BF16)
        out_spec = pl.BlockSpec((None, None, n, LANES), lambda b, h: (b, h, 0, 0))
    else:
        out_shape = jax.ShapeDtypeStruct((B, Hkv, HEAD_DIM, n), BF16)
        out_spec = pl.BlockSpec((None, None, HEAD_DIM, n), lambda b, h: (b, h, 0, 0))
    return pl.pallas_call(
        functools.partial(_compress_kernel, is_key=is_key),
        grid=(B, Hkv),
        in_specs=[
            pl.BlockSpec((None, n, half), lambda b, h: (b, 0, 0)),
            pl.BlockSpec((None, 8, 2 * half), lambda b, h: (h, 0, 0)),
            pl.BlockSpec((None, 2 * half, hidden), lambda b, h: (h, 0, 0)),
            pl.BlockSpec((hidden, LANES), lambda b, h: (0, 0)),
            pl.BlockSpec((1, LANES), lambda b, h: (0, 0)),
        ],
        out_specs=out_spec,
        out_shape=out_shape,
        compiler_params=pltpu.CompilerParams(
            dimension_semantics=("parallel", "parallel"), vmem_limit_bytes=VMEM_LIMIT),
        name="nsa_compress_k" if is_key else "nsa_compress_v",
    )(chunks, pe8, w1.astype(BF16), w2p, g)


CMP_ROWS = 128


def _nsa_cmp_kernel(q_ref, kc_ref, vct_ref, ct_ref, o_ref, selb_ref, used_ref, imp_ref,
                    *, tq, group, n_heads):
    g = pl.program_id(1)
    qi = pl.program_id(2)
    n_cmp = kc_ref.shape[0]
    n_slc = ct_ref.shape[0]
    i0 = qi * tq
    qpos = i0 + lax.broadcasted_iota(jnp.int32, (1, tq), 1)
    qt = q_ref[...]

    def attend(rows):
        cend = lax.broadcasted_iota(jnp.int32, (rows, 1), 0) * CMP_STRIDE + (CMP_LEN - 1)
        valid = cend <= qpos
        kc = kc_ref[:rows, :]
        vct = vct_ref[:, :rows]
        ts = []
        for e in range(group):
            slope = jnp.exp2(-8.0 * (group * g + e + 1).astype(F32) / n_heads)
            aug = _aug_const(tq, [slope, slope, -slope * i0.astype(F32)]).astype(BF16)
            qa = jnp.concatenate([qt[:, e * HEAD_DIM:(e + 1) * HEAD_DIM], aug], axis=1)
            ts.append(_dot_nt(kc, qa))
        psum = jnp.zeros((rows, tq), F32)
        ps = []
        for t in ts:
            t = jnp.where(valid, t, NEG)
            m = jnp.max(t, axis=0, keepdims=True)
            p = jnp.where(valid, jnp.exp(t - m), 0.0)
            l = jnp.sum(p, axis=0, keepdims=True)
            p = p / jnp.where(l > 0, l, 1.0)
            ps.append(p.astype(BF16))
            psum = psum + p
        o_ref[...] = jnp.concatenate([_dot(vct, p) for p in ps], axis=0).T
        p_hi, p_lo = _split_bf16(psum)
        ct = ct_ref[:, :rows]
        imp_ref[...] = _dot(ct, p_hi) + _dot(ct, p_lo)

    needed = (i0 + tq - CMP_LEN) // CMP_STRIDE + 1
    n_var = max(n_cmp // CMP_ROWS, 1)
    for v in range(n_var):
        rows = n_cmp if v == n_var - 1 else (v + 1) * CMP_ROWS
        cond = needed > v * CMP_ROWS
        if v < n_var - 1:
            cond = jnp.logical_and(cond, needed <= (v + 1) * CMP_ROWS)
        pl.when(cond)(functools.partial(attend, rows))

    imp = imp_ref[...]
    jb = lax.broadcasted_iota(jnp.int32, (n_slc, tq), 0)
    qb = qpos // SLC_BLK
    cur = jnp.where(jb > qb, -2e38, imp)
    for forced in (0, qb, qb - 1):
        cur = jnp.where(jb == forced, 3e38, cur)
    sel = _topk_member(cur, min(SLC_TOPK, n_slc))
    selb = jnp.where(jb <= qb, jnp.where(sel, 0.0, NEG), NEG).T
    selb_ref[...] = selb.astype(BF16)
    for c in range(tq // NSA_TQ):
        hit = jnp.max(selb[c * NSA_TQ:(c + 1) * NSA_TQ], axis=0, keepdims=True) == 0.0
        used_ref[c] = jnp.where(hit, 1, 0)


def _nsa_cmp(act, kc, vct, batch, seq, col_q, n_heads, group, tq=256):
    Hkv = n_heads // group
    n_cmp = kc.shape[2]
    n_slc = seq // SLC_BLK
    cs = jnp.arange(n_cmp) * CMP_STRIDE
    ss = jnp.arange(n_slc) * SLC_BLK
    overlap = (cs[None, :] <= ss[:, None] + SLC_BLK - 1) & (cs[None, :] + CMP_LEN - 1 >= ss[:, None])
    overlap = overlap & (cs[None, :] + CMP_LEN <= seq)
    ct = overlap.astype(BF16)
    gw = group * HEAD_DIM
    return pl.pallas_call(
        functools.partial(_nsa_cmp_kernel, tq=tq, group=group, n_heads=n_heads),
        grid=(batch, Hkv, seq // tq),
        in_specs=[
            pl.BlockSpec((None, tq, gw), lambda b, g, i: (b, i, col_q + g)),
            pl.BlockSpec((None, None, n_cmp, LANES), lambda b, g, i: (b, g, 0, 0)),
            pl.BlockSpec((None, None, HEAD_DIM, n_cmp), lambda b, g, i: (b, g, 0, 0)),
            pl.BlockSpec((n_slc, n_cmp), lambda b, g, i: (0, 0)),
        ],
        out_specs=[
            pl.BlockSpec((None, tq, gw), lambda b, g, i: (b, i, g)),
            pl.BlockSpec((None, None, tq, n_slc), lambda b, g, i: (b, g, i, 0)),
            pl.BlockSpec((None, None, tq // NSA_TQ, 1, n_slc), lambda b, g, i: (b, g, i, 0, 0)),
        ],
        out_shape=[
            jax.ShapeDtypeStruct((batch, seq, n_heads * HEAD_DIM), F32),
            jax.ShapeDtypeStruct((batch, Hkv, seq, n_slc), BF16),
            jax.ShapeDtypeStruct((batch, Hkv, seq // NSA_TQ, 1, n_slc), jnp.int32),
        ],
        scratch_shapes=[pltpu.VMEM((n_slc, tq), F32)],
        compiler_params=pltpu.CompilerParams(
            dimension_semantics=("parallel", "parallel", "parallel"),
            vmem_limit_bytes=VMEM_LIMIT),
        name="nsa_cmp",
    )(act, kc, vct, ct)


def _nsa_main_kernel(used_ref, q_ref, selb_ref, kvs_ref, kvw_ref, kconst_ref, ocmp_ref, gate_ref,
                     o_ref, ks_ref, vst_ref, kw_ref, vwt_ref, accs_ref, accw_ref, gt_ref, t_ref,
                     p_ref, tiles_ref, *, tq, tk, group, n_heads):
    g = pl.program_id(1)
    qi = pl.program_id(2)
    S = kvs_ref.shape[0]
    R = group * tq
    i0 = qi * tq

    @pl.when(qi == 0)
    def _():
        def body(c, carry):
            rows = pl.ds(pl.multiple_of(c * tk, tk), tk)
            kvs = kvs_ref[rows, :]
            kvw = kvw_ref[rows, :]
            kconst = kconst_ref[...]
            ks_ref[rows, :HEAD_DIM] = kvs[:, :HEAD_DIM]
            ks_ref[rows, HEAD_DIM:LANES] = kconst[:, :HEAD_DIM]
            kw_ref[rows, :] = jnp.concatenate([kvw[:, :HEAD_DIM], kconst[:, :HEAD_DIM]], axis=1)
            vst_ref[:, rows] = kvs.astype(F32).T[HEAD_DIM:, :].astype(BF16)
            vwt_ref[:, rows] = kvw.astype(F32).T[HEAD_DIM:, :].astype(BF16)
            key_blk = (c * tk + lax.broadcasted_iota(jnp.int32, (tk, n_slc), 0)) // SLC_BLK
            onehot = key_blk == lax.broadcasted_iota(jnp.int32, (tk, n_slc), 1)
            ks_ref[rows, LANES:] = jnp.where(onehot, 1.0, 0.0).astype(BF16)
            return carry

        n_slc = ks_ref.shape[1] - LANES
        lax.fori_loop(0, S // tk, body, 0)

    qt = q_ref[...]
    selb = selb_ref[...]
    qs_parts, qw_parts, slope_parts = [], [], []
    for e in range(group):
        slope = jnp.exp2(-8.0 * (group * g + e + 1).astype(F32) / n_heads)
        aug = _aug_const(tq, [slope]).astype(BF16)
        qe = qt[:, e * HEAD_DIM:(e + 1) * HEAD_DIM]
        qw_parts.append(jnp.concatenate([qe, aug], axis=1))
        qs_parts.append(jnp.concatenate([qe, aug, selb], axis=1))
        slope_parts.append(jnp.full((1, tq), slope, F32))
    q_slc = jnp.concatenate(qs_parts, axis=0)
    q_win = jnp.concatenate(qw_parts, axis=0)
    slope_row = jnp.concatenate(slope_parts, axis=1)
    qpos = i0 + lax.broadcasted_iota(jnp.int32, (1, R), 1) % tq

    nd = i0 // tk
    rows = lambda n: pl.ds(pl.multiple_of(n * tk, tk), tk)
    shift = lambda n: slope_row * (n * tk - i0).astype(F32)
    kpos = lambda n: n * tk + lax.broadcasted_iota(jnp.int32, (tk, 1), 0)
    causal = lambda n: kpos(n) <= qpos

    n_tiles = S // tk
    base = ((pl.program_id(0) * pl.num_programs(1) + g) * pl.num_programs(2) + qi) * n_tiles

    def collect(n, cnt):
        tiles_ref[cnt] = n
        return cnt + jnp.where(used_ref[base + n] != 0, 1, 0)

    tiles_ref[0] = nd
    n_used = lax.fori_loop(0, nd, collect, 1)
    assert 2 * tk == NSA_WINDOW and tk % tq == 0
    tw, offs = [], []
    for c in range(3):
        n = nd - 2 + c
        t = _dot_nt(kw_ref[rows(jnp.maximum(n, 0)), :], q_win)
        if c == 0:
            t = jnp.where(qpos - kpos(n) < NSA_WINDOW, t, NEG)
        elif c == 2:
            t = jnp.where(kpos(n) <= qpos, t, NEG)
        tw.append(t)
        offs.append(shift(n) + jnp.where(n >= 0, 0.0, NEG))
    mw = functools.reduce(jnp.maximum,
                          [jnp.max(t, axis=0, keepdims=True) + off for t, off in zip(tw, offs)])
    lw = jnp.zeros((1, R), F32)
    o_w = jnp.zeros((HEAD_DIM, R), F32)
    for c in range(3):
        p = jnp.exp(tw[c] - (mw - offs[c]))
        lw = lw + jnp.sum(p, axis=0, keepdims=True)
        o_w = o_w + _dot(vwt_ref[:, rows(jnp.maximum(nd - 2 + c, 0))], p.astype(BF16))
    accw_ref[...] = o_w

    slc = _Stream(q_slc, lambda n: ks_ref[rows(n), :], lambda n: vst_ref[:, rows(n)],
                  accs_ref, t_ref, p_ref, shift, mask=causal)
    ((_, ls),) = _flash_sweep([slc], n_used, lambda pos: tiles_ref[pos])

    gz = gate_ref[...]
    gt_ref[...] = jax.nn.sigmoid(gz).T
    oc_t = ocmp_ref[...].T
    outs = []
    for e in range(group):
        head = group * g + e
        cols = slice(e * tq, (e + 1) * tq)
        o_s = accs_ref[:, cols] / ls[:, cols]
        o_w = accw_ref[:, cols] / lw[:, cols]
        o_c = oc_t[e * HEAD_DIM:(e + 1) * HEAD_DIM, :]
        g_c = gt_ref[pl.ds(3 * head, 1), :]
        g_s = gt_ref[pl.ds(3 * head + 1, 1), :]
        g_w = gt_ref[pl.ds(3 * head + 2, 1), :]
        outs.append(g_c * o_c + g_s * o_s + g_w * o_w)
    o_ref[...] = jnp.concatenate(outs, axis=0).T


def _nsa_main(act, selb, used, ocmp, gates, batch, seq, col_q, col_kvs, col_kvw, col_gate, n_heads,
              group):
    tq, tk = NSA_TQ, NSA_TK
    Hkv = n_heads // group
    n_slc = seq // SLC_BLK
    gw = group * HEAD_DIM
    R = group * tq
    jj = jnp.arange(tk, dtype=F32)
    kconst = jnp.zeros((tk, HEAD_DIM), F32).at[:, 0].set(jj).astype(BF16)
    blk_per_tile = tk // SLC_BLK
    used_tiles = jnp.max(used.reshape(batch, Hkv, seq // tq, n_slc // blk_per_tile, blk_per_tile),
                         axis=-1).reshape(-1).astype(jnp.int32)
    grid_spec = pltpu.PrefetchScalarGridSpec(
        num_scalar_prefetch=1,
        grid=(batch, Hkv, seq // tq),
        in_specs=[
            pl.BlockSpec((None, tq, gw), lambda b, g, i, u: (b, i, col_q + g)),
            pl.BlockSpec((None, None, tq, n_slc), lambda b, g, i, u: (b, g, i, 0)),
            pl.BlockSpec((None, seq, LANES), lambda b, g, i, u: (b, 0, col_kvs + g)),
            pl.BlockSpec((None, seq, LANES), lambda b, g, i, u: (b, 0, col_kvw + g)),
            pl.BlockSpec((tk, HEAD_DIM), lambda b, g, i, u: (0, 0)),
            pl.BlockSpec((None, tq, gw), lambda b, g, i, u: (b, i, g)),
            pl.BlockSpec((None, tq, LANES), lambda b, g, i, u: (b, i, col_gate)),
        ],
        out_specs=pl.BlockSpec((None, tq, gw), lambda b, g, i, u: (b, i, g)),
        scratch_shapes=[
            pltpu.VMEM((seq, LANES + n_slc), BF16),
            pltpu.VMEM((HEAD_DIM, seq), BF16),
            pltpu.VMEM((seq, LANES), BF16),
            pltpu.VMEM((HEAD_DIM, seq), BF16),
            pltpu.VMEM((HEAD_DIM, R), F32),
            pltpu.VMEM((HEAD_DIM, R), F32),
            pltpu.VMEM((LANES, tq), F32),
            pltpu.VMEM((FLASH_UNROLL, tk, R), F32), pltpu.VMEM((FLASH_STAGED, tk, R), BF16),
            pltpu.SMEM((seq // tk,), jnp.int32),
        ],
    )
    return pl.pallas_call(
        functools.partial(_nsa_main_kernel, tq=tq, tk=tk, group=group, n_heads=n_heads),
        grid_spec=grid_spec,
        out_shape=jax.ShapeDtypeStruct((batch, seq, n_heads * HEAD_DIM), F32),
        compiler_params=pltpu.CompilerParams(
            dimension_semantics=("parallel", "parallel", "arbitrary"),
            vmem_limit_bytes=VMEM_LIMIT),
        name="nsa_main",
    )(used_tiles, act, selb, act, act, kconst, ocmp, gates)


def _tile_gain(g, width):
    return jnp.tile(g.astype(F32), width // HEAD_DIM)


def _even_layer(x, ln, w_in, qkn, lam, subln, w_out, layer):
    B, S, D = x.shape
    M = B * S
    x2d = x.reshape(M, D)
    W = 512
    aq, ak, av, az, bq, bk, bv, bz = [w_in[:, i * W:(i + 1) * W] for i in range(8)]
    scale = HEAD_DIM ** -0.5
    ones, zeros = jnp.ones((W,), F32), jnp.zeros((W,), F32)
    w_b = jnp.concatenate([bq, bk, bv], axis=1).astype(BF16)
    gain_b = jnp.concatenate([_tile_gain(qkn[2], W) * scale, _tile_gain(qkn[3], W), ones])
    mask_b = np.repeat([1.0, 1.0, 0.0], W)
    act_b = _project(x2d, ln, w_b, gain_b, mask_b, BF16)
    w_a = jnp.concatenate([aq, ak, av, az, bz], axis=1).astype(BF16)
    gain_a = jnp.concatenate([_tile_gain(qkn[0], W) * scale, _tile_gain(qkn[1], W), ones, ones, ones])
    mask_a = np.repeat([1.0, 1.0, 0.0, 0.0, 0.0], W)
    act_a = _project(x2d, ln, w_a, gain_a, mask_a, F32)

    lam_init = 0.8 - 0.6 * math.exp(-0.3 * layer)
    lf = lam.astype(F32)
    lam_f = jnp.exp(jnp.sum(lf[0] * lf[1])) - jnp.exp(jnp.sum(lf[2] * lf[3])) + lam_init
    ob = _diff_attention(act_b.reshape(B, S, 3 * W), lam_f, subln, B, S, 0, 4, 8, 4, 1.0 - lam_init)
    oa = _dilated_attention(act_a.reshape(B, S, 5 * W), B, S, 0, 4, 8, 8)
    out = _out_project(x2d, oa.reshape(M, W), ob.reshape(M, W), act_a, 3, 4, w_out)
    return out.reshape(B, S, D)


def _odd_layer(x, ln, w_in, qkn, phi_pe, phi_w1, phi_w2, w_out):
    B, S, D = x.shape
    M = B * S
    x2d = x.reshape(M, D)
    W, KW = 512, 128
    offs = [0]
    for s in (W, W, W, W, W, KW, KW, KW, KW, KW, KW, W, 24):
        offs.append(offs[-1] + s)
    cq, ck, cv, cz, dq, dkc, dvc, dks, dvs, dkw, dvw, dz, dg = [
        w_in[:, offs[i]:offs[i + 1]] for i in range(13)]
    scale = HEAD_DIM ** -0.5
    hd = HEAD_DIM
    ones = lambda n: jnp.ones((n,), F32)
    zeros = lambda n: jnp.zeros((n,), F32)
    kv_pair = lambda k, v: jnp.concatenate([k[:, :hd], v[:, :hd], k[:, hd:], v[:, hd:]], axis=1)
    w_b = jnp.concatenate([cq, ck, cv, dq, kv_pair(dks, dvs), kv_pair(dkw, dvw), dkc, dvc,
                           jnp.zeros((D, 2 * KW), w_in.dtype)], axis=1).astype(BF16)
    kv_gain = lambda g: jnp.concatenate([g, ones(hd), g, ones(hd)])
    kv_mask = np.repeat([1.0, 0.0, 1.0, 0.0], hd)
    gain_b = jnp.concatenate([_tile_gain(qkn[0], W) * scale, _tile_gain(qkn[1], W), ones(W),
                              _tile_gain(qkn[2], W) * scale, kv_gain(qkn[4].astype(F32)),
                              kv_gain(qkn[5].astype(F32)), ones(4 * KW)])
    mask_b = np.concatenate([np.repeat([1.0, 1.0, 0.0, 1.0], W), kv_mask, kv_mask, np.zeros(4 * KW)])
    act_b = _project(x2d, ln, w_b, gain_b, mask_b, BF16).reshape(B, S, 6 * W)
    w_g = jnp.concatenate([cz, dz, dg, jnp.zeros((D, W - 24), w_in.dtype)], axis=1).astype(BF16)
    act_g = _project(x2d, ln, w_g, ones(3 * W), np.zeros(3 * W), F32)

    oc = _moba_attention(act_b, B, S, 0, 4, 8, 8, 256, 3)

    def chunks(t):
        return t.reshape(B, S // CMP_STRIDE, CMP_STRIDE * 2 * hd)

    kc = _compress(chunks(act_b[:, :, 20 * KW:21 * KW]), phi_pe[0], phi_w1[0], phi_w2[0], qkn[3], True)
    vct = _compress(chunks(act_b[:, :, 21 * KW:22 * KW]), phi_pe[1], phi_w1[1], phi_w2[1], qkn[3], False)
    ocmp, selb, used = _nsa_cmp(act_b, kc, vct, B, S, 6, 8, 4)
    od = _nsa_main(act_b, selb, used, ocmp, act_g.reshape(B, S, 3 * W), B, S, 6, 16, 18, 8, 8, 4)
    out = _out_project(x2d, oc.reshape(M, W), od.reshape(M, W), act_g, 0, 1, w_out)
    return out.reshape(B, S, D)


def kernel(x, ln_e, w_in_e, qkn_e, lam_e, subln_e, w_out_e, ln_o, w_in_o, qkn_o, phi_pe, phi_w1, phi_w2, w_out_o):
    n_layers = ln_e.shape[0] + ln_o.shape[0]
    for layer in range(n_layers):
        i = layer // 2
        if layer % 2 == 0:
            x = _even_layer(x, ln_e[i], w_in_e[i], qkn_e[i], lam_e[i], subln_e[i], w_out_e[i], layer)
        else:
            x = _odd_layer(x, ln_o[i], w_in_o[i], qkn_o[i], phi_pe[i], phi_w1[i], phi_w2[i], w_out_o[i])
    return x
```

```python
import functools
import math

import jax
import jax.numpy as jnp
import numpy as np
from jax import lax
from jax.experimental import pallas as pl
from jax.experimental.pallas import tpu as pltpu

HEAD_DIM = 64
LANES = 128
RMS_EPS = 1e-6
NEG = -2e30
M_INIT = -1e30
VMEM_LIMIT = 56 * 1024 * 1024

F32 = jnp.float32
BF16 = jnp.bfloat16

_NT = (((1,), (1,)), ((), ()))


def _dot(a, b):
    return jnp.dot(a, b, preferred_element_type=F32)


def _dot_nt(a, b):
    return lax.dot_general(a, b, _NT, preferred_element_type=F32)


def _alibi_slopes(n):
    return [2.0 ** (-8.0 * (i + 1) / n) for i in range(n)]


def _split_bf16(x):
    hi = x.astype(BF16)
    lo = (x - hi.astype(F32)).astype(BF16)
    return hi, lo


NORM_CHUNK = 256


def _proj_kernel(x_ref, ln_ref, w_ref, gain_ref, nmask_ref, bd_ref, o_ref, *, norm_chunks):
    x = x_ref[...]
    ms = jnp.mean(x * x, axis=-1, keepdims=True)
    h = (x * lax.rsqrt(ms + RMS_EPS) * ln_ref[...]).astype(BF16)
    bd = bd_ref[...]
    col = lambda c: slice(c * NORM_CHUNK, (c + 1) * NORM_CHUNK)
    ys = {}
    for c, normed in enumerate(norm_chunks):
        y = _dot(h, w_ref[:, col(c)])
        if normed:
            ys[c] = y
        else:
            o_ref[:, col(c)] = y.astype(o_ref.dtype)
    for c, y in ys.items():
        hi, lo = _split_bf16(y * y)
        ms = _dot(hi, bd) + _dot(lo, bd)
        yn = y * lax.rsqrt(ms + RMS_EPS) * gain_ref[:, col(c)]
        o_ref[:, col(c)] = jnp.where(nmask_ref[:, col(c)] > 0.5, yn, y).astype(o_ref.dtype)


def _project(x2d, ln, w, gain, nmask, out_dtype, tm=1024):
    M, D = x2d.shape
    N = w.shape[1]
    assert M % tm == 0 and N % NORM_CHUNK == 0
    norm_chunks = [bool(c) for c in np.asarray(nmask).reshape(-1, NORM_CHUNK).max(axis=1) > 0.5]
    nmask = jnp.asarray(nmask, F32)
    r = jnp.arange(NORM_CHUNK) // HEAD_DIM
    bd = jnp.where(r[:, None] == r[None, :], 1.0 / HEAD_DIM, 0.0).astype(BF16)
    return pl.pallas_call(
        functools.partial(_proj_kernel, norm_chunks=tuple(norm_chunks)),
        grid=(M // tm,),
        in_specs=[
            pl.BlockSpec((tm, D), lambda i: (i, 0)),
            pl.BlockSpec((1, D), lambda i: (0, 0)),
            pl.BlockSpec((D, N), lambda i: (0, 0)),
            pl.BlockSpec((1, N), lambda i: (0, 0)),
            pl.BlockSpec((1, N), lambda i: (0, 0)),
            pl.BlockSpec((NORM_CHUNK, NORM_CHUNK), lambda i: (0, 0)),
        ],
        out_specs=pl.BlockSpec((tm, N), lambda i: (i, 0)),
        out_shape=jax.ShapeDtypeStruct((M, N), out_dtype),
        compiler_params=pltpu.CompilerParams(
            dimension_semantics=("parallel",), vmem_limit_bytes=VMEM_LIMIT),
        name="proj",
    )(x2d, ln.reshape(1, D).astype(F32), w, gain.reshape(1, N).astype(F32),
      nmask.reshape(1, N).astype(F32), bd)


def _outproj_kernel(x_ref, oa_ref, ob_ref, za_ref, zb_ref, wa_ref, wb_ref, o_ref):
    za = za_ref[...]
    zb = zb_ref[...]
    ma = (oa_ref[...] * (za * jax.nn.sigmoid(za))).astype(BF16)
    mb = (ob_ref[...] * (zb * jax.nn.sigmoid(zb))).astype(BF16)
    o_ref[...] = x_ref[...] + _dot(ma, wa_ref[...]) + _dot(mb, wb_ref[...])


def _out_project(x2d, oa, ob, z, za_blk, zb_blk, w_out, tm=512):
    M, D = x2d.shape
    W = oa.shape[1]
    wa = w_out[:W].astype(BF16)
    wb = w_out[W:].astype(BF16)
    return pl.pallas_call(
        _outproj_kernel,
        grid=(M // tm,),
        in_specs=[
            pl.BlockSpec((tm, D), lambda i: (i, 0)),
            pl.BlockSpec((tm, W), lambda i: (i, 0)),
            pl.BlockSpec((tm, W), lambda i: (i, 0)),
            pl.BlockSpec((tm, W), lambda i: (i, za_blk)),
            pl.BlockSpec((tm, W), lambda i: (i, zb_blk)),
            pl.BlockSpec((W, D), lambda i: (0, 0)),
            pl.BlockSpec((W, D), lambda i: (0, 0)),
        ],
        out_specs=pl.BlockSpec((tm, D), lambda i: (i, 0)),
        out_shape=jax.ShapeDtypeStruct((M, D), F32),
        compiler_params=pltpu.CompilerParams(
            dimension_semantics=("parallel",), vmem_limit_bytes=VMEM_LIMIT),
        name="outproj",
    )(x2d, oa, ob, z, z, wa, wb)


class _Stream:
    def __init__(self, q, k_tile, vt_tile, acc_ref, t_ref, p_ref, shift, bias=None, mask=None,
                 mask_every_tile=False):
        self.q, self.k_tile, self.vt_tile = q, k_tile, vt_tile
        self.acc_ref, self.t_ref, self.p_ref = acc_ref, t_ref, p_ref
        self.shift, self.bias, self.mask = shift, bias, mask
        self.mask_every_tile = mask_every_tile

    def scores(self, n):
        return _dot_nt(self.k_tile(n), self.q)

    def softmax(self, t, n, m, l, pad, masked):
        if masked:
            t = jnp.where(self.mask(n), t, NEG)
        off = self.shift(n) + pad
        if self.bias is not None:
            off = off + self.bias(n)
        m_new = jnp.maximum(m, jnp.max(t, axis=0, keepdims=True) + off)
        p = jnp.exp(t - (m_new - off))
        alpha = jnp.exp(m - m_new)
        return m_new, alpha * l + jnp.sum(p, axis=0, keepdims=True), alpha, p.astype(BF16)


FLASH_UNROLL = 2
FLASH_STAGED = 1
FLASH_QK_FIRST = False


def _flash_sweep(streams, count, tile_of):
    U, D = FLASH_UNROLL, FLASH_STAGED
    R = streams[0].q.shape[0]
    last = count - 1
    W = 2 + D
    state = []
    for s in streams:
        s.acc_ref[...] = jnp.zeros_like(s.acc_ref)
        s.p_ref[...] = jnp.zeros_like(s.p_ref)
        for k in range(U):
            s.t_ref[k] = s.scores(tile_of(jnp.minimum(k, last)))
        state += [jnp.full((1, R), M_INIT, F32), jnp.zeros((1, R), F32)]
        state += [jnp.ones((1, R), F32)] * D

    def flush(n_prev, state):
        pvs = [[_dot(s.vt_tile(n_prev[d]), s.p_ref[d]) for d in range(D)] for s in streams]
        for i, s in enumerate(streams):
            acc = s.acc_ref[...]
            for d in range(D):
                acc = state[W * i + 2 + d] * acc + pvs[i][d]
            s.acc_ref[...] = acc

    def body(g, carry, first_body):
        n_prev, state = list(carry[:D]), list(carry[D:])
        base = U * g
        if FLASH_QK_FIRST:
            nxt = [[s.scores(tile_of(jnp.minimum(base + U + k, last))) for k in range(U)]
                   for s in streams]
            flush(n_prev, state)
        else:
            flush(n_prev, state)
            nxt = [[s.scores(tile_of(jnp.minimum(base + U + k, last))) for k in range(U)]
                   for s in streams]
        for k in range(U):
            n = tile_of(jnp.minimum(base + k, last))
            pad = jnp.where(base + k < count, 0.0, NEG)
            d = k - (U - D)
            for i, s in enumerate(streams):
                m, l = state[W * i:W * i + 2]
                masked = s.mask is not None and (s.mask_every_tile or (first_body and k == 0))
                m, l, a, p = s.softmax(s.t_ref[k], n, m, l, pad, masked)
                state[W * i:W * i + 2] = [m, l]
                if d < 0:
                    s.acc_ref[...] = a * s.acc_ref[...] + _dot(s.vt_tile(n), p)
                else:
                    state[W * i + 2 + d] = a
                    s.p_ref[d] = p
            if d >= 0:
                n_prev[d] = n
        for i, s in enumerate(streams):
            for k in range(U):
                s.t_ref[k] = nxt[i][k]
        return tuple(n_prev) + tuple(state)

    carry = body(0, (tile_of(0),) * D + tuple(state), True)
    carry = lax.fori_loop(1, (count + U - 1) // U, lambda g, c: body(g, c, False), carry)
    flush(carry[:D], carry[D:])
    return [(carry[D + W * i], carry[D + W * i + 1]) for i in range(len(streams))]


def _aug_const(rows, col_vals):
    lane = lax.broadcasted_iota(jnp.int32, (rows, HEAD_DIM), 1)
    out = jnp.zeros((rows, HEAD_DIM), F32)
    for c, v in enumerate(col_vals):
        out = jnp.where(lane == c, v, out)
    return out


def _diff_kernel(lam_ref, q_ref, k_ref, v_ref, subln_ref, o_ref,
                 k1_ref, k2_ref, vt_ref, acc1_ref, acc2_ref, t_ref, p_ref, *, tq, n_heads, out_scale):
    h = pl.program_id(1)
    qi = pl.program_id(2)
    S = k_ref.shape[0]
    tk = tq
    slope = jnp.exp2(-8.0 * (h + 1).astype(F32) / n_heads)

    @pl.when(qi == 0)
    def _():
        jj = lax.broadcasted_iota(jnp.int32, (tk, 1), 0).astype(F32)
        kaug = _aug_const(tk, [jj]).astype(BF16)

        def body(c, carry):
            rows = pl.ds(pl.multiple_of(c * tk, tk), tk)
            kc = k_ref[rows, :]
            k1_ref[rows, :] = jnp.concatenate([kc[:, :HEAD_DIM], kaug], axis=1)
            k2_ref[rows, :] = jnp.concatenate([kc[:, HEAD_DIM:], kaug], axis=1)
            vt_ref[:, rows] = v_ref[rows, :].astype(F32).T.astype(BF16)
            return carry

        lax.fori_loop(0, S // tk, body, 0)

    qt = q_ref[...]
    qaug = _aug_const(tq, [slope]).astype(BF16)
    q1 = jnp.concatenate([qt[:, :HEAD_DIM], qaug], axis=1)
    q2 = jnp.concatenate([qt[:, HEAD_DIM:], qaug], axis=1)

    rows = lambda n: pl.ds(pl.multiple_of(n * tk, tk), tk)
    qpos = qi * tq + lax.broadcasted_iota(jnp.int32, (1, tq), 1)
    causal = lambda n: n * tk + lax.broadcasted_iota(jnp.int32, (tk, 1), 0) <= qpos
    shift = lambda n: slope * ((n - qi) * tk).astype(F32)
    vt_tile = lambda n: vt_ref[:, rows(n)]
    streams = [
        _Stream(q1, lambda n: k1_ref[rows(n), :], vt_tile, acc1_ref, t_ref.at[0], p_ref.at[0],
                shift, mask=causal),
        _Stream(q2, lambda n: k2_ref[rows(n), :], vt_tile, acc2_ref, t_ref.at[1], p_ref.at[1],
                shift, mask=causal),
    ]
    (m1, l1), (m2, l2) = _flash_sweep(streams, qi + 1, lambda pos: jnp.where(pos == 0, qi, pos - 1))

    d = acc1_ref[...] / l1 - lam_ref[0] * (acc2_ref[...] / l2)
    r = lax.rsqrt(jnp.mean(d * d, axis=0, keepdims=True) + RMS_EPS)
    o_ref[...] = (d * r).T * (subln_ref[...] * out_scale)


def _diff_attention(qkv, lam_f, subln, batch, seq, col_q, col_k, col_v, n_heads, out_scale, tq=256):
    dv = 2 * HEAD_DIM
    kern = functools.partial(_diff_kernel, tq=tq, n_heads=n_heads, out_scale=out_scale)
    grid_spec = pltpu.PrefetchScalarGridSpec(
        num_scalar_prefetch=1,
        grid=(batch, n_heads, seq // tq),
        in_specs=[
            pl.BlockSpec((None, tq, dv), lambda b, h, i, s: (b, i, col_q + h)),
            pl.BlockSpec((None, seq, dv), lambda b, h, i, s: (b, 0, col_k + h)),
            pl.BlockSpec((None, seq, dv), lambda b, h, i, s: (b, 0, col_v + h)),
            pl.BlockSpec((1, dv), lambda b, h, i, s: (0, 0)),
        ],
        out_specs=pl.BlockSpec((None, tq, dv), lambda b, h, i, s: (b, i, h)),
        scratch_shapes=[
            pltpu.VMEM((seq, LANES), BF16), pltpu.VMEM((seq, LANES), BF16),
            pltpu.VMEM((dv, seq), BF16),
            pltpu.VMEM((dv, tq), F32), pltpu.VMEM((dv, tq), F32),
            pltpu.VMEM((2, FLASH_UNROLL, tq, tq), F32), pltpu.VMEM((2, FLASH_STAGED, tq, tq), BF16),
        ],
    )
    return pl.pallas_call(
        kern,
        grid_spec=grid_spec,
        out_shape=jax.ShapeDtypeStruct((batch, seq, n_heads * dv), F32),
        compiler_params=pltpu.CompilerParams(
            dimension_semantics=("parallel", "parallel", "arbitrary"),
            vmem_limit_bytes=VMEM_LIMIT),
        name="diff_attn",
    )(lam_f.reshape(1).astype(F32), qkv, qkv, qkv, subln.reshape(1, dv).astype(F32))


BAND = 128
DILATIONS = (1, 4, 16)


def _dilated_kernel(q_ref, kp_ref, kc_ref, vp_ref, vc_ref, o_ref, kbuf_ref, vbuf_ref, st_ref,
                    *, chunk, n_heads):
    pair = pl.program_id(1)
    ci = pl.program_id(2)
    kbuf_ref[:chunk, :] = kp_ref[...]
    kbuf_ref[chunk:, :] = kc_ref[...]
    vbuf_ref[:chunk, :] = vp_ref[...]
    vbuf_ref[chunk:, :] = vc_ref[...]

    qi = lax.broadcasted_iota(jnp.int32, (BAND, 2 * BAND), 0)
    kj = lax.broadcasted_iota(jnp.int32, (BAND, 2 * BAND), 1)
    dist = qi + BAND - kj
    in_band = jnp.abs(2 * dist - BAND) <= BAND
    dist_f = dist.astype(F32)
    key_col = lax.broadcasted_iota(jnp.int32, (1, 2 * BAND), 1)
    slopes = [jnp.exp2(-8.0 * (2 * pair + e + 1).astype(F32) / n_heads) for e in range(2)]
    lane_q = lax.broadcasted_iota(jnp.int32, (BAND, LANES), 1)
    own_half = [lane_q < HEAD_DIM, lane_q >= HEAD_DIM]

    for pi, dil in enumerate(DILATIONS):
        span = dil * BAND
        n_blk = chunk // span

        def block(idx, carry, dil=dil, span=span, n_blk=n_blk, pi=pi):
            r = idx // n_blk
            b = idx % n_blk
            q0 = r + b * span
            qb = q_ref[pl.ds(q0, BAND, stride=dil), :]
            kb = kbuf_ref[pl.ds(chunk + q0 - span, 2 * BAND, stride=dil), :]
            vb = vbuf_ref[pl.ds(chunk + q0 - span, 2 * BAND, stride=dil), :]
            first = jnp.logical_and(ci == 0, b == 0)
            before_start = jnp.where(key_col < jnp.where(first, BAND, 0), NEG, 0.0)
            kb16 = kb.astype(BF16)
            vb16 = vb.astype(BF16)
            packed = []
            for e in range(2):
                qe = jnp.where(own_half[e], qb, 0.0).astype(BF16)
                s = _dot_nt(qe, kb16)
                s = jnp.where(in_band, s - (slopes[e] * dil) * dist_f, NEG) + before_start
                m = jnp.max(s, axis=1, keepdims=True)
                p = jnp.exp(s - m)
                l = jnp.sum(p, axis=1, keepdims=True)
                o = _dot(p.astype(BF16), vb16) / l
                packed.append(jnp.where(own_half[e], o, m + jnp.log(l)))
            for e in range(2):
                st_ref[pi, e, pl.ds(q0, BAND, stride=dil), :] = packed[e]
            return carry

        lax.fori_loop(0, dil * n_blk, block, 0, unroll=8)

    rows_per = 256
    lane = lax.broadcasted_iota(jnp.int32, (rows_per, LANES), 1)

    def merge(c, carry):
        rows = pl.ds(pl.multiple_of(c * rows_per, rows_per), rows_per)
        outs = []
        for e in range(2):
            xs = [st_ref[pi, e, rows, :] for pi in range(len(DILATIONS))]
            top = functools.reduce(jnp.maximum, xs)
            num = jnp.zeros((rows_per, LANES), F32)
            den = jnp.zeros((rows_per, LANES), F32)
            for x in xs:
                w = pltpu.roll(jnp.exp(x - top), HEAD_DIM, axis=1)
                num = num + w * x
                den = den + w
            outs.append(num / den)
        o_ref[rows, :] = jnp.where(lane < HEAD_DIM, outs[0], outs[1])
        return carry

    lax.fori_loop(0, chunk // rows_per, merge, 0)


def _dilated_attention(act, batch, seq, col_q, col_k, col_v, n_heads, chunk=2048):
    assert chunk % (max(DILATIONS) * BAND) == 0 and seq % chunk == 0
    prev = lambda c: jnp.maximum(c - 1, 0)
    return pl.pallas_call(
        functools.partial(_dilated_kernel, chunk=chunk, n_heads=n_heads),
        grid=(batch, n_heads // 2, seq // chunk),
        in_specs=[
            pl.BlockSpec((None, chunk, LANES), lambda b, p, c: (b, c, col_q + p)),
            pl.BlockSpec((None, chunk, LANES), lambda b, p, c: (b, prev(c), col_k + p)),
            pl.BlockSpec((None, chunk, LANES), lambda b, p, c: (b, c, col_k + p)),
            pl.BlockSpec((None, chunk, LANES), lambda b, p, c: (b, prev(c), col_v + p)),
            pl.BlockSpec((None, chunk, LANES), lambda b, p, c: (b, c, col_v + p)),
        ],
        out_specs=pl.BlockSpec((None, chunk, LANES), lambda b, p, c: (b, c, p)),
        out_shape=jax.ShapeDtypeStruct((batch, seq, n_heads * HEAD_DIM), F32),
        scratch_shapes=[
            pltpu.VMEM((2 * chunk, LANES), F32),
            pltpu.VMEM((2 * chunk, LANES), F32),
            pltpu.VMEM((len(DILATIONS), 2, chunk, LANES), F32),
        ],
        compiler_params=pltpu.CompilerParams(
            dimension_semantics=("parallel", "parallel", "arbitrary"),
            vmem_limit_bytes=VMEM_LIMIT),
        name="dilated_attn",
    )(act, act, act, act, act)


TAKEN = -3e38


def _topk_member(cur, k):
    n = cur.shape[0]
    row = lax.broadcasted_iota(jnp.int32, cur.shape, 0)
    sel = jnp.zeros(cur.shape, F32)
    for _ in range(k):
        mx = jnp.max(cur, axis=0, keepdims=True)
        first = jnp.min(jnp.where(cur == mx, row, n), axis=0, keepdims=True)
        pick = row == first
        sel = jnp.where(pick, 1.0, sel)
        cur = jnp.where(pick, TAKEN, cur)
    return sel > 0.5


def _moba_kernel(q_ref, k_ref, v_ref, o_ref, kaug_ref, vt_ref, kmean_ref, selb_ref, acc_ref,
                 t_ref, p_ref, *, blk, topk, n_heads):
    pair = pl.program_id(1)
    qi = pl.program_id(2)
    S = k_ref.shape[0]
    nblk = S // blk
    tq = tk = blk

    @pl.when(qi == 0)
    def _():
        jj = lax.broadcasted_iota(jnp.int32, (tk, 1), 0).astype(F32)
        kaug = _aug_const(tk, [jj]).astype(BF16)

        def body(c, carry):
            rows = pl.ds(pl.multiple_of(c * tk, tk), tk)
            kc = k_ref[rows, :]
            vc = v_ref[rows, :].astype(F32).T.astype(BF16)
            for e in range(2):
                kaug_ref[e, rows, :] = jnp.concatenate(
                    [kc[:, e * HEAD_DIM:(e + 1) * HEAD_DIM], kaug], axis=1)
                vt_ref[e, :, rows] = vc[e * HEAD_DIM:(e + 1) * HEAD_DIM, :]
            kmean_ref[pl.ds(c, 1), :] = jnp.mean(kc.astype(F32), axis=0, keepdims=True)
            return carry

        lax.fori_loop(0, nblk, body, 0)

    qt = q_ref[...]
    blk_row = lax.broadcasted_iota(jnp.int32, (nblk, tq), 0)
    past = blk_row < qi
    qs, slopes = [], []
    for e in range(2):
        slope = jnp.exp2(-8.0 * (2 * pair + e + 1).astype(F32) / n_heads)
        qe = qt[:, e * HEAD_DIM:(e + 1) * HEAD_DIM]
        km_hi, km_lo = _split_bf16(kmean_ref[:, e * HEAD_DIM:(e + 1) * HEAD_DIM])
        gate = _dot_nt(km_hi, qe) + _dot_nt(km_lo, qe)
        sel = _topk_member(jnp.where(past, gate, -2e38), topk)
        selb_ref[e] = jnp.where(blk_row == qi, 0.0, jnp.where(past, jnp.where(sel, 0.0, NEG), NEG))
        qs.append(jnp.concatenate([qe, _aug_const(tq, [slope]).astype(BF16)], axis=1))
        slopes.append(slope)

    rows = lambda n: pl.ds(pl.multiple_of(n * tk, tk), tk)
    qpos = qi * tq + lax.broadcasted_iota(jnp.int32, (1, tq), 1)
    causal = lambda n: n * tk + lax.broadcasted_iota(jnp.int32, (tk, 1), 0) <= qpos
    streams = [
        _Stream(qs[e],
                functools.partial(lambda n, e: kaug_ref[e, rows(n), :], e=e),
                functools.partial(lambda n, e: vt_ref[e, :, rows(n)], e=e),
                acc_ref.at[e], t_ref.at[e], p_ref.at[e],
                functools.partial(lambda n, e: slopes[e] * ((n - qi) * tk).astype(F32), e=e),
                bias=functools.partial(lambda n, e: selb_ref[e, pl.ds(n, 1), :], e=e),
                mask=causal)
        for e in range(2)]
    stats = _flash_sweep(streams, qi + 1, lambda pos: jnp.where(pos == 0, qi, pos - 1))
    o_ref[...] = jnp.concatenate([acc_ref[e] / stats[e][1] for e in range(2)], axis=0).T


def _moba_attention(qkv, batch, seq, col_q, col_k, col_v, n_heads, blk, topk):
    kern = functools.partial(_moba_kernel, blk=blk, topk=topk, n_heads=n_heads)
    nblk = seq // blk
    return pl.pallas_call(
        kern,
        grid=(batch, n_heads // 2, nblk),
        in_specs=[
            pl.BlockSpec((None, blk, LANES), lambda b, p, i: (b, i, col_q + p)),
            pl.BlockSpec((None, seq, LANES), lambda b, p, i: (b, 0, col_k + p)),
            pl.BlockSpec((None, seq, LANES), lambda b, p, i: (b, 0, col_v + p)),
        ],
        out_specs=pl.BlockSpec((None, blk, LANES), lambda b, p, i: (b, i, p)),
        out_shape=jax.ShapeDtypeStruct((batch, seq, n_heads * HEAD_DIM), F32),
        scratch_shapes=[
            pltpu.VMEM((2, seq, LANES), BF16),
            pltpu.VMEM((2, HEAD_DIM, seq), BF16),
            pltpu.VMEM((nblk, LANES), F32),
            pltpu.VMEM((2, nblk, blk), F32),
            pltpu.VMEM((2, HEAD_DIM, blk), F32),
            pltpu.VMEM((2, FLASH_UNROLL, blk, blk), F32),
            pltpu.VMEM((2, FLASH_STAGED, blk, blk), BF16),
        ],
        compiler_params=pltpu.CompilerParams(
            dimension_semantics=("parallel", "parallel", "arbitrary"),
            vmem_limit_bytes=VMEM_LIMIT),
        name="moba_attn",
    )(qkv, qkv, qkv)


CMP_STRIDE = 16
CMP_LEN = 32
SLC_BLK = 64
SLC_TOPK = 16
NSA_WINDOW = 512
NSA_TQ = 256
NSA_TK = 256


def _compress_kernel(c_ref, pe_ref, w1_ref, w2_ref, g_ref, o_ref, *, is_key):
    half = c_ref.shape[1]
    n = c_ref.shape[0]
    c = c_ref[...]
    w1 = w1_ref[...]
    a_lo = _dot(c, w1[:half])
    a_hi = _dot(c, w1[half:])
    pe_hi, pe_lo = _split_bf16(pe_ref[...])
    c0 = (_dot(pe_hi, w1) + _dot(pe_lo, w1))[0:1]
    y = a_lo + pltpu.roll(a_hi, n - 1, axis=0) + c0
    hid = (y * jax.nn.sigmoid(y)).astype(BF16)
    if is_key:
        z = _dot(hid, w2_ref[...])
    else:
        z = _dot(hid.astype(F32), w2_ref[...].astype(F32))
    if is_key:
        ms = jnp.sum(z * z, axis=-1, keepdims=True) * (1.0 / HEAD_DIM)
        kn = z * lax.rsqrt(ms + RMS_EPS) * g_ref[...]
        i = lax.broadcasted_iota(jnp.int32, (n, 1), 0)
        aug = _aug_const(n, [(i // 16 * 256).astype(F32), (i % 16 * 16).astype(F32), 1.0])
        o_ref[...] = jnp.concatenate([kn[:, :HEAD_DIM], aug], axis=1).astype(BF16)
    else:
        o_ref[...] = z.T[:HEAD_DIM, :].astype(BF16)


def _compress(chunks, pe, w1, w2, gain, is_key):
    B, n, half = chunks.shape
    Hkv = half // (CMP_STRIDE * HEAD_DIM)
    hidden = w1.shape[1]

    def per_head(t, tail):
        t = t.reshape(2, CMP_STRIDE, 1, HEAD_DIM, *tail)
        onehot = jnp.eye(Hkv, dtype=t.dtype).reshape(Hkv, 1, 1, Hkv, 1, *([1] * len(tail)))
        return (onehot * t[None]).reshape(Hkv, 2 * half, *tail)

    pe8 = jnp.broadcast_to(per_head(pe.reshape(-1).astype(F32), ())[:, None, :], (Hkv, 8, 2 * half))
    w1 = per_head(w1, (hidden,))
    w2p = jnp.pad(w2, ((0, 0), (0, LANES - w2.shape[1]))).astype(BF16)
    g = jnp.pad(gain.astype(F32), (0, LANES - HEAD_DIM)).reshape(1, LANES)
    if is_key:
        out_shape = jax.ShapeDtypeStruct((B, Hkv, n, LANES), BF16)
        out_spec = pl.BlockSpec((None, None, n, LANES), lambda b, h: (b, h, 0, 0))
    else:
        out_shape = jax.ShapeDtypeStruct((B, Hkv, HEAD_DIM, n), BF16)
        out_spec = pl.BlockSpec((None, None, HEAD_DIM, n), lambda b, h: (b, h, 0, 0))
    return pl.pallas_call(
        functools.partial(_compress_kernel, is_key=is_key),
        grid=(B, Hkv),
        in_specs=[
            pl.BlockSpec((None, n, half), lambda b, h: (b, 0, 0)),
            pl.BlockSpec((None, 8, 2 * half), lambda b, h: (h, 0, 0)),
            pl.BlockSpec((None, 2 * half, hidden), lambda b, h: (h, 0, 0)),
            pl.BlockSpec((hidden, LANES), lambda b, h: (0, 0)),
            pl.BlockSpec((1, LANES), lambda b, h: (0, 0)),
        ],
        out_specs=out_spec,
        out_shape=out_shape,
        compiler_params=pltpu.CompilerParams(
            dimension_semantics=("parallel", "parallel"), vmem_limit_bytes=VMEM_LIMIT),
        name="nsa_compress_k" if is_key else "nsa_compress_v",
    )(chunks, pe8, w1.astype(BF16), w2p, g)


CMP_ROWS = 128


def _nsa_cmp_kernel(q_ref, kc_ref, vct_ref, ct_ref, o_ref, selb_ref, used_ref, imp_ref,
                    *, tq, group, n_heads):
    g = pl.program_id(1)
    qi = pl.program_id(2)
    n_cmp = kc_ref.shape[0]
    n_slc = ct_ref.shape[0]
    i0 = qi * tq
    qpos = i0 + lax.broadcasted_iota(jnp.int32, (1, tq), 1)
    qt = q_ref[...]

    def attend(rows):
        cend = lax.broadcasted_iota(jnp.int32, (rows, 1), 0) * CMP_STRIDE + (CMP_LEN - 1)
        valid = cend <= qpos
        kc = kc_ref[:rows, :]
        vct = vct_ref[:, :rows]
        ts = []
        for e in range(group):
            slope = jnp.exp2(-8.0 * (group * g + e + 1).astype(F32) / n_heads)
            aug = _aug_const(tq, [slope, slope, -slope * i0.astype(F32)]).astype(BF16)
            qa = jnp.concatenate([qt[:, e * HEAD_DIM:(e + 1) * HEAD_DIM], aug], axis=1)
            ts.append(_dot_nt(kc, qa))
        psum = jnp.zeros((rows, tq), F32)
        ps = []
        for t in ts:
            t = jnp.where(valid, t, NEG)
            m = jnp.max(t, axis=0, keepdims=True)
            p = jnp.where(valid, jnp.exp(t - m), 0.0)
            l = jnp.sum(p, axis=0, keepdims=True)
            p = p / jnp.where(l > 0, l, 1.0)
            ps.append(p.astype(BF16))
            psum = psum + p
        o_ref[...] = jnp.concatenate([_dot(vct, p) for p in ps], axis=0).T
        p_hi, p_lo = _split_bf16(psum)
        ct = ct_ref[:, :rows]
        imp_ref[...] = _dot(ct, p_hi) + _dot(ct, p_lo)

    needed = (i0 + tq - CMP_LEN) // CMP_STRIDE + 1
    n_var = max(n_cmp // CMP_ROWS, 1)
    for v in range(n_var):
        rows = n_cmp if v == n_var - 1 else (v + 1) * CMP_ROWS
        cond = needed > v * CMP_ROWS
        if v < n_var - 1:
            cond = jnp.logical_and(cond, needed <= (v + 1) * CMP_ROWS)
        pl.when(cond)(functools.partial(attend, rows))

    imp = imp_ref[...]
    jb = lax.broadcasted_iota(jnp.int32, (n_slc, tq), 0)
    qb = qpos // SLC_BLK
    cur = jnp.where(jb > qb, -2e38, imp)
    for forced in (0, qb, qb - 1):
        cur = jnp.where(jb == forced, 3e38, cur)
    sel = _topk_member(cur, min(SLC_TOPK, n_slc))
    selb = jnp.where(jb <= qb, jnp.where(sel, 0.0, NEG), NEG).T
    selb_ref[...] = selb.astype(BF16)
    for c in range(tq // NSA_TQ):
        hit = jnp.max(selb[c * NSA_TQ:(c + 1) * NSA_TQ], axis=0, keepdims=True) == 0.0
        used_ref[c] = jnp.where(hit, 1, 0)


def _nsa_cmp(act, kc, vct, batch, seq, col_q, n_heads, group, tq=256):
    Hkv = n_heads // group
    n_cmp = kc.shape[2]
    n_slc = seq // SLC_BLK
    cs = jnp.arange(n_cmp) * CMP_STRIDE
    ss = jnp.arange(n_slc) * SLC_BLK
    overlap = (cs[None, :] <= ss[:, None] + SLC_BLK - 1) & (cs[None, :] + CMP_LEN - 1 >= ss[:, None])
    overlap = overlap & (cs[None, :] + CMP_LEN <= seq)
    ct = overlap.astype(BF16)
    gw = group * HEAD_DIM
    return pl.pallas_call(
        functools.partial(_nsa_cmp_kernel, tq=tq, group=group, n_heads=n_heads),
        grid=(batch, Hkv, seq // tq),
        in_specs=[
            pl.BlockSpec((None, tq, gw), lambda b, g, i: (b, i, col_q + g)),
            pl.BlockSpec((None, None, n_cmp, LANES), lambda b, g, i: (b, g, 0, 0)),
            pl.BlockSpec((None, None, HEAD_DIM, n_cmp), lambda b, g, i: (b, g, 0, 0)),
            pl.BlockSpec((n_slc, n_cmp), lambda b, g, i: (0, 0)),
        ],
        out_specs=[
            pl.BlockSpec((None, tq, gw), lambda b, g, i: (b, i, g)),
            pl.BlockSpec((None, None, tq, n_slc), lambda b, g, i: (b, g, i, 0)),
            pl.BlockSpec((None, None, tq // NSA_TQ, 1, n_slc), lambda b, g, i: (b, g, i, 0, 0)),
        ],
        out_shape=[
            jax.ShapeDtypeStruct((batch, seq, n_heads * HEAD_DIM), F32),
            jax.ShapeDtypeStruct((batch, Hkv, seq, n_slc), BF16),
            jax.ShapeDtypeStruct((batch, Hkv, seq // NSA_TQ, 1, n_slc), jnp.int32),
        ],
        scratch_shapes=[pltpu.VMEM((n_slc, tq), F32)],
        compiler_params=pltpu.CompilerParams(
            dimension_semantics=("parallel", "parallel", "parallel"),
            vmem_limit_bytes=VMEM_LIMIT),
        name="nsa_cmp",
    )(act, kc, vct, ct)


def _nsa_main_kernel(used_ref, q_ref, selb_ref, kvs_ref, kvw_ref, kconst_ref, ocmp_ref, gate_ref,
                     o_ref, ks_ref, vst_ref, kw_ref, vwt_ref, accs_ref, accw_ref, gt_ref, t_ref,
                     p_ref, tiles_ref, *, tq, tk, group, n_heads):
    g = pl.program_id(1)
    qi = pl.program_id(2)
    S = kvs_ref.shape[0]
    R = group * tq
    i0 = qi * tq

    @pl.when(qi == 0)
    def _():
        def body(c, carry):
            rows = pl.ds(pl.multiple_of(c * tk, tk), tk)
            kvs = kvs_ref[rows, :]
            kvw = kvw_ref[rows, :]
            kconst = kconst_ref[...]
            ks_ref[rows, :HEAD_DIM] = kvs[:, :HEAD_DIM]
            ks_ref[rows, HEAD_DIM:LANES] = kconst[:, :HEAD_DIM]
            kw_ref[rows, :] = jnp.concatenate([kvw[:, :HEAD_DIM], kconst[:, :HEAD_DIM]], axis=1)
            vst_ref[:, rows] = kvs.astype(F32).T[HEAD_DIM:, :].astype(BF16)
            vwt_ref[:, rows] = kvw.astype(F32).T[HEAD_DIM:, :].astype(BF16)
            key_blk = (c * tk + lax.broadcasted_iota(jnp.int32, (tk, n_slc), 0)) // SLC_BLK
            onehot = key_blk == lax.broadcasted_iota(jnp.int32, (tk, n_slc), 1)
            ks_ref[rows, LANES:] = jnp.where(onehot, 1.0, 0.0).astype(BF16)
            return carry

        n_slc = ks_ref.shape[1] - LANES
        lax.fori_loop(0, S // tk, body, 0)

    qt = q_ref[...]
    selb = selb_ref[...]
    qs_parts, qw_parts, slope_parts = [], [], []
    for e in range(group):
        slope = jnp.exp2(-8.0 * (group * g + e + 1).astype(F32) / n_heads)
        aug = _aug_const(tq, [slope]).astype(BF16)
        qe = qt[:, e * HEAD_DIM:(e + 1) * HEAD_DIM]
        qw_parts.append(jnp.concatenate([qe, aug], axis=1))
        qs_parts.append(jnp.concatenate([qe, aug, selb], axis=1))
        slope_parts.append(jnp.full((1, tq), slope, F32))
    q_slc = jnp.concatenate(qs_parts, axis=0)
    q_win = jnp.concatenate(qw_parts, axis=0)
    slope_row = jnp.concatenate(slope_parts, axis=1)
    qpos = i0 + lax.broadcasted_iota(jnp.int32, (1, R), 1) % tq

    nd = i0 // tk
    rows = lambda n: pl.ds(pl.multiple_of(n * tk, tk), tk)
    shift = lambda n: slope_row * (n * tk - i0).astype(F32)
    kpos = lambda n: n * tk + lax.broadcasted_iota(jnp.int32, (tk, 1), 0)
    causal = lambda n: kpos(n) <= qpos

    n_tiles = S // tk
    base = ((pl.program_id(0) * pl.num_programs(1) + g) * pl.num_programs(2) + qi) * n_tiles

    def collect(n, cnt):
        tiles_ref[cnt] = n
        return cnt + jnp.where(used_ref[base + n] != 0, 1, 0)

    tiles_ref[0] = nd
    n_used = lax.fori_loop(0, nd, collect, 1)
    assert 2 * tk == NSA_WINDOW and tk % tq == 0
    tw, offs = [], []
    for c in range(3):
        n = nd - 2 + c
        t = _dot_nt(kw_ref[rows(jnp.maximum(n, 0)), :], q_win)
        if c == 0:
            t = jnp.where(qpos - kpos(n) < NSA_WINDOW, t, NEG)
        elif c == 2:
            t = jnp.where(kpos(n) <= qpos, t, NEG)
        tw.append(t)
        offs.append(shift(n) + jnp.where(n >= 0, 0.0, NEG))
    mw = functools.reduce(jnp.maximum,
                          [jnp.max(t, axis=0, keepdims=True) + off for t, off in zip(tw, offs)])
    lw = jnp.zeros((1, R), F32)
    o_w = jnp.zeros((HEAD_DIM, R), F32)
    for c in range(3):
        p = jnp.exp(tw[c] - (mw - offs[c]))
        lw = lw + jnp.sum(p, axis=0, keepdims=True)
        o_w = o_w + _dot(vwt_ref[:, rows(jnp.maximum(nd - 2 + c, 0))], p.astype(BF16))
    accw_ref[...] = o_w

    slc = _Stream(q_slc, lambda n: ks_ref[rows(n), :], lambda n: vst_ref[:, rows(n)],
                  accs_ref, t_ref, p_ref, shift, mask=causal)
    ((_, ls),) = _flash_sweep([slc], n_used, lambda pos: tiles_ref[pos])

    gz = gate_ref[...]
    gt_ref[...] = jax.nn.sigmoid(gz).T
    oc_t = ocmp_ref[...].T
    outs = []
    for e in range(group):
        head = group * g + e
        cols = slice(e * tq, (e + 1) * tq)
        o_s = accs_ref[:, cols] / ls[:, cols]
        o_w = accw_ref[:, cols] / lw[:, cols]
        o_c = oc_t[e * HEAD_DIM:(e + 1) * HEAD_DIM, :]
        g_c = gt_ref[pl.ds(3 * head, 1), :]
        g_s = gt_ref[pl.ds(3 * head + 1, 1), :]
        g_w = gt_ref[pl.ds(3 * head + 2, 1), :]
        outs.append(g_c * o_c + g_s * o_s + g_w * o_w)
    o_ref[...] = jnp.concatenate(outs, axis=0).T


def _nsa_main(act, selb, used, ocmp, gates, batch, seq, col_q, col_kvs, col_kvw, col_gate, n_heads,
              group):
    tq, tk = NSA_TQ, NSA_TK
    Hkv = n_heads // group
    n_slc = seq // SLC_BLK
    gw = group * HEAD_DIM
    R = group * tq
    jj = jnp.arange(tk, dtype=F32)
    kconst = jnp.zeros((tk, HEAD_DIM), F32).at[:, 0].set(jj).astype(BF16)
    blk_per_tile = tk // SLC_BLK
    used_tiles = jnp.max(used.reshape(batch, Hkv, seq // tq, n_slc // blk_per_tile, blk_per_tile),
                         axis=-1).reshape(-1).astype(jnp.int32)
    grid_spec = pltpu.PrefetchScalarGridSpec(
        num_scalar_prefetch=1,
        grid=(batch, Hkv, seq // tq),
        in_specs=[
            pl.BlockSpec((None, tq, gw), lambda b, g, i, u: (b, i, col_q + g)),
            pl.BlockSpec((None, None, tq, n_slc), lambda b, g, i, u: (b, g, i, 0)),
            pl.BlockSpec((None, seq, LANES), lambda b, g, i, u: (b, 0, col_kvs + g)),
            pl.BlockSpec((None, seq, LANES), lambda b, g, i, u: (b, 0, col_kvw + g)),
            pl.BlockSpec((tk, HEAD_DIM), lambda b, g, i, u: (0, 0)),
            pl.BlockSpec((None, tq, gw), lambda b, g, i, u: (b, i, g)),
            pl.BlockSpec((None, tq, LANES), lambda b, g, i, u: (b, i, col_gate)),
        ],
        out_specs=pl.BlockSpec((None, tq, gw), lambda b, g, i, u: (b, i, g)),
        scratch_shapes=[
            pltpu.VMEM((seq, LANES + n_slc), BF16),
            pltpu.VMEM((HEAD_DIM, seq), BF16),
            pltpu.VMEM((seq, LANES), BF16),
            pltpu.VMEM((HEAD_DIM, seq), BF16),
            pltpu.VMEM((HEAD_DIM, R), F32),
            pltpu.VMEM((HEAD_DIM, R), F32),
            pltpu.VMEM((LANES, tq), F32),
            pltpu.VMEM((FLASH_UNROLL, tk, R), F32), pltpu.VMEM((FLASH_STAGED, tk, R), BF16),
            pltpu.SMEM((seq // tk,), jnp.int32),
        ],
    )
    return pl.pallas_call(
        functools.partial(_nsa_main_kernel, tq=tq, tk=tk, group=group, n_heads=n_heads),
        grid_spec=grid_spec,
        out_shape=jax.ShapeDtypeStruct((batch, seq, n_heads * HEAD_DIM), F32),
        compiler_params=pltpu.CompilerParams(
            dimension_semantics=("parallel", "parallel", "arbitrary"),
            vmem_limit_bytes=VMEM_LIMIT),
        name="nsa_main",
    )(used_tiles, act, selb, act, act, kconst, ocmp, gates)


def _tile_gain(g, width):
    return jnp.tile(g.astype(F32), width // HEAD_DIM)


def _even_layer(x, ln, w_in, qkn, lam, subln, w_out, layer):
    B, S, D = x.shape
    M = B * S
    x2d = x.reshape(M, D)
    W = 512
    aq, ak, av, az, bq, bk, bv, bz = [w_in[:, i * W:(i + 1) * W] for i in range(8)]
    scale = HEAD_DIM ** -0.5
    ones, zeros = jnp.ones((W,), F32), jnp.zeros((W,), F32)
    w_b = jnp.concatenate([bq, bk, bv], axis=1).astype(BF16)
    gain_b = jnp.concatenate([_tile_gain(qkn[2], W) * scale, _tile_gain(qkn[3], W), ones])
    mask_b = np.repeat([1.0, 1.0, 0.0], W)
    act_b = _project(x2d, ln, w_b, gain_b, mask_b, BF16)
    w_a = jnp.concatenate([aq, ak, av, az, bz], axis=1).astype(BF16)
    gain_a = jnp.concatenate([_tile_gain(qkn[0], W) * scale, _tile_gain(qkn[1], W), ones, ones, ones])
    mask_a = np.repeat([1.0, 1.0, 0.0, 0.0, 0.0], W)
    act_a = _project(x2d, ln, w_a, gain_a, mask_a, F32)

    lam_init = 0.8 - 0.6 * math.exp(-0.3 * layer)
    lf = lam.astype(F32)
    lam_f = jnp.exp(jnp.sum(lf[0] * lf[1])) - jnp.exp(jnp.sum(lf[2] * lf[3])) + lam_init
    ob = _diff_attention(act_b.reshape(B, S, 3 * W), lam_f, subln, B, S, 0, 4, 8, 4, 1.0 - lam_init)
    oa = _dilated_attention(act_a.reshape(B, S, 5 * W), B, S, 0, 4, 8, 8)
    out = _out_project(x2d, oa.reshape(M, W), ob.reshape(M, W), act_a, 3, 4, w_out)
    return out.reshape(B, S, D)


def _odd_layer(x, ln, w_in, qkn, phi_pe, phi_w1, phi_w2, w_out):
    B, S, D = x.shape
    M = B * S
    x2d = x.reshape(M, D)
    W, KW = 512, 128
    offs = [0]
    for s in (W, W, W, W, W, KW, KW, KW, KW, KW, KW, W, 24):
        offs.append(offs[-1] + s)
    cq, ck, cv, cz, dq, dkc, dvc, dks, dvs, dkw, dvw, dz, dg = [
        w_in[:, offs[i]:offs[i + 1]] for i in range(13)]
    scale = HEAD_DIM ** -0.5
    hd = HEAD_DIM
    ones = lambda n: jnp.ones((n,), F32)
    zeros = lambda n: jnp.zeros((n,), F32)
    kv_pair = lambda k, v: jnp.concatenate([k[:, :hd], v[:, :hd], k[:, hd:], v[:, hd:]], axis=1)
    w_b = jnp.concatenate([cq, ck, cv, dq, kv_pair(dks, dvs), kv_pair(dkw, dvw), dkc, dvc,
                           jnp.zeros((D, 2 * KW), w_in.dtype)], axis=1).astype(BF16)
    kv_gain = lambda g: jnp.concatenate([g, ones(hd), g, ones(hd)])
    kv_mask = np.repeat([1.0, 0.0, 1.0, 0.0], hd)
    gain_b = jnp.concatenate([_tile_gain(qkn[0], W) * scale, _tile_gain(qkn[1], W), ones(W),
                              _tile_gain(qkn[2], W) * scale, kv_gain(qkn[4].astype(F32)),
                              kv_gain(qkn[5].astype(F32)), ones(4 * KW)])
    mask_b = np.concatenate([np.repeat([1.0, 1.0, 0.0, 1.0], W), kv_mask, kv_mask, np.zeros(4 * KW)])
    act_b = _project(x2d, ln, w_b, gain_b, mask_b, BF16).reshape(B, S, 6 * W)
    w_g = jnp.concatenate([cz, dz, dg, jnp.zeros((D, W - 24), w_in.dtype)], axis=1).astype(BF16)
    act_g = _project(x2d, ln, w_g, ones(3 * W), np.zeros(3 * W), F32)

    oc = _moba_attention(act_b, B, S, 0, 4, 8, 8, 256, 3)

    def chunks(t):
        return t.reshape(B, S // CMP_STRIDE, CMP_STRIDE * 2 * hd)

    kc = _compress(chunks(act_b[:, :, 20 * KW:21 * KW]), phi_pe[0], phi_w1[0], phi_w2[0], qkn[3], True)
    vct = _compress(chunks(act_b[:, :, 21 * KW:22 * KW]), phi_pe[1], phi_w1[1], phi_w2[1], qkn[3], False)
    ocmp, selb, used = _nsa_cmp(act_b, kc, vct, B, S, 6, 8, 4)
    od = _nsa_main(act_b, selb, used, ocmp, act_g.reshape(B, S, 3 * W), B, S, 6, 16, 18, 8, 8, 4)
    out = _out_project(x2d, oc.reshape(M, W), od.reshape(M, W), act_g, 0, 1, w_out)
    return out.reshape(B, S, D)


def kernel(x, ln_e, w_in_e, qkn_e, lam_e, subln_e, w_out_e, ln_o, w_in_o, qkn_o, phi_pe, phi_w1, phi_w2, w_out_o):
    n_layers = ln_e.shape[0] + ln_o.shape[0]
    for layer in range(n_layers):
        i = layer // 2
        if layer % 2 == 0:
            x = _even_layer(x, ln_e[i], w_in_e[i], qkn_e[i], lam_e[i], subln_e[i], w_out_e[i], layer)
        else:
            x = _odd_layer(x, ln_o[i], w_in_o[i], qkn_o[i], phi_pe[i], phi_w1[i], phi_w2[i], w_out_o[i])
    return x
```

```python
import functools
import math

import jax
import jax.numpy as jnp
import numpy as np
from jax import lax
from jax.experimental import pallas as pl
from jax.experimental.pallas import tpu as pltpu

HEAD_DIM = 64
LANES = 128
RMS_EPS = 1e-6
NEG = -2e30
M_INIT = -1e30
VMEM_LIMIT = 56 * 1024 * 1024

F32 = jnp.float32
BF16 = jnp.bfloat16

_NT = (((1,), (1,)), ((), ()))


def _dot(a, b):
    return jnp.dot(a, b, preferred_element_type=F32)


def _dot_nt(a, b):
    return lax.dot_general(a, b, _NT, preferred_element_type=F32)


def _alibi_slopes(n):
    return [2.0 ** (-8.0 * (i + 1) / n) for i in range(n)]


def _split_bf16(x):
    hi = x.astype(BF16)
    lo = (x - hi.astype(F32)).astype(BF16)
    return hi, lo


NORM_CHUNK = 256


def _proj_kernel(x_ref, ln_ref, w_ref, gain_ref, nmask_ref, bd_ref, o_ref, *, norm_chunks):
    x = x_ref[...]
    ms = jnp.mean(x * x, axis=-1, keepdims=True)
    h = (x * lax.rsqrt(ms + RMS_EPS) * ln_ref[...]).astype(BF16)
    bd = bd_ref[...]
    col = lambda c: slice(c * NORM_CHUNK, (c + 1) * NORM_CHUNK)
    ys = {}
    for c, normed in enumerate(norm_chunks):
        y = _dot(h, w_ref[:, col(c)])
        if normed:
            ys[c] = y
        else:
            o_ref[:, col(c)] = y.astype(o_ref.dtype)
    for c, y in ys.items():
        hi, lo = _split_bf16(y * y)
        ms = _dot(hi, bd) + _dot(lo, bd)
        yn = y * lax.rsqrt(ms + RMS_EPS) * gain_ref[:, col(c)]
        o_ref[:, col(c)] = jnp.where(nmask_ref[:, col(c)] > 0.5, yn, y).astype(o_ref.dtype)


def _project(x2d, ln, w, gain, nmask, out_dtype, tm=1024):
    M, D = x2d.shape
    N = w.shape[1]
    assert M % tm == 0 and N % NORM_CHUNK == 0
    norm_chunks = [bool(c) for c in np.asarray(nmask).reshape(-1, NORM_CHUNK).max(axis=1) > 0.5]
    nmask = jnp.asarray(nmask, F32)
    r = jnp.arange(NORM_CHUNK) // HEAD_DIM
    bd = jnp.where(r[:, None] == r[None, :], 1.0 / HEAD_DIM, 0.0).astype(BF16)
    return pl.pallas_call(
        functools.partial(_proj_kernel, norm_chunks=tuple(norm_chunks)),
        grid=(M // tm,),
        in_specs=[
            pl.BlockSpec((tm, D), lambda i: (i, 0)),
            pl.BlockSpec((1, D), lambda i: (0, 0)),
            pl.BlockSpec((D, N), lambda i: (0, 0)),
            pl.BlockSpec((1, N), lambda i: (0, 0)),
            pl.BlockSpec((1, N), lambda i: (0, 0)),
            pl.BlockSpec((NORM_CHUNK, NORM_CHUNK), lambda i: (0, 0)),
        ],
        out_specs=pl.BlockSpec((tm, N), lambda i: (i, 0)),
        out_shape=jax.ShapeDtypeStruct((M, N), out_dtype),
        compiler_params=pltpu.CompilerParams(
            dimension_semantics=("parallel",), vmem_limit_bytes=VMEM_LIMIT),
        name="proj",
    )(x2d, ln.reshape(1, D).astype(F32), w, gain.reshape(1, N).astype(F32),
      nmask.reshape(1, N).astype(F32), bd)


def _outproj_kernel(x_ref, oa_ref, ob_ref, za_ref, zb_ref, wa_ref, wb_ref, o_ref):
    za = za_ref[...]
    zb = zb_ref[...]
    ma = (oa_ref[...] * (za * jax.nn.sigmoid(za))).astype(BF16)
    mb = (ob_ref[...] * (zb * jax.nn.sigmoid(zb))).astype(BF16)
    o_ref[...] = x_ref[...] + _dot(ma, wa_ref[...]) + _dot(mb, wb_ref[...])


def _out_project(x2d, oa, ob, z, za_blk, zb_blk, w_out, tm=512):
    M, D = x2d.shape
    W = oa.shape[1]
    wa = w_out[:W].astype(BF16)
    wb = w_out[W:].astype(BF16)
    return pl.pallas_call(
        _outproj_kernel,
        grid=(M // tm,),
        in_specs=[
            pl.BlockSpec((tm, D), lambda i: (i, 0)),
            pl.BlockSpec((tm, W), lambda i: (i, 0)),
            pl.BlockSpec((tm, W), lambda i: (i, 0)),
            pl.BlockSpec((tm, W), lambda i: (i, za_blk)),
            pl.BlockSpec((tm, W), lambda i: (i, zb_blk)),
            pl.BlockSpec((W, D), lambda i: (0, 0)),
            pl.BlockSpec((W, D), lambda i: (0, 0)),
        ],
        out_specs=pl.BlockSpec((tm, D), lambda i: (i, 0)),
        out_shape=jax.ShapeDtypeStruct((M, D), F32),
        compiler_params=pltpu.CompilerParams(
            dimension_semantics=("parallel",), vmem_limit_bytes=VMEM_LIMIT),
        name="outproj",
    )(x2d, oa, ob, z, z, wa, wb)


class _Stream:
    def __init__(self, q, k_tile, vt_tile, acc_ref, t_ref, p_ref, shift, bias=None, mask=None,
                 mask_every_tile=False):
        self.q, self.k_tile, self.vt_tile = q, k_tile, vt_tile
        self.acc_ref, self.t_ref, self.p_ref = acc_ref, t_ref, p_ref
        self.shift, self.bias, self.mask = shift, bias, mask
        self.mask_every_tile = mask_every_tile

    def scores(self, n):
        return _dot_nt(self.k_tile(n), self.q)

    def softmax(self, t, n, m, l, pad, masked):
        if masked:
            t = jnp.where(self.mask(n), t, NEG)
        off = self.shift(n) + pad
        if self.bias is not None:
            off = off + self.bias(n)
        m_new = jnp.maximum(m, jnp.max(t, axis=0, keepdims=True) + off)
        p = jnp.exp(t - (m_new - off))
        alpha = jnp.exp(m - m_new)
        return m_new, alpha * l + jnp.sum(p, axis=0, keepdims=True), alpha, p.astype(BF16)


FLASH_UNROLL = 2
FLASH_STAGED = 1
FLASH_QK_FIRST = False


def _flash_sweep(streams, count, tile_of):
    U, D = FLASH_UNROLL, FLASH_STAGED
    R = streams[0].q.shape[0]
    last = count - 1
    W = 2 + D
    state = []
    for s in streams:
        s.acc_ref[...] = jnp.zeros_like(s.acc_ref)
        s.p_ref[...] = jnp.zeros_like(s.p_ref)
        for k in range(U):
            s.t_ref[k] = s.scores(tile_of(jnp.minimum(k, last)))
        state += [jnp.full((1, R), M_INIT, F32), jnp.zeros((1, R), F32)]
        state += [jnp.ones((1, R), F32)] * D

    def flush(n_prev, state):
        pvs = [[_dot(s.vt_tile(n_prev[d]), s.p_ref[d]) for d in range(D)] for s in streams]
        for i, s in enumerate(streams):
            acc = s.acc_ref[...]
            for d in range(D):
                acc = state[W * i + 2 + d] * acc + pvs[i][d]
            s.acc_ref[...] = acc

    def body(g, carry, first_body):
        n_prev, state = list(carry[:D]), list(carry[D:])
        base = U * g
        if FLASH_QK_FIRST:
            nxt = [[s.scores(tile_of(jnp.minimum(base + U + k, last))) for k in range(U)]
                   for s in streams]
            flush(n_prev, state)
        else:
            flush(n_prev, state)
            nxt = [[s.scores(tile_of(jnp.minimum(base + U + k, last))) for k in range(U)]
                   for s in streams]
        for k in range(U):
            n = tile_of(jnp.minimum(base + k, last))
            pad = jnp.where(base + k < count, 0.0, NEG)
            d = k - (U - D)
            for i, s in enumerate(streams):
                m, l = state[W * i:W * i + 2]
                masked = s.mask is not None and (s.mask_every_tile or (first_body and k == 0))
                m, l, a, p = s.softmax(s.t_ref[k], n, m, l, pad, masked)
                state[W * i:W * i + 2] = [m, l]
                if d < 0:
                    s.acc_ref[...] = a * s.acc_ref[...] + _dot(s.vt_tile(n), p)
                else:
                    state[W * i + 2 + d] = a
                    s.p_ref[d] = p
            if d >= 0:
                n_prev[d] = n
        for i, s in enumerate(streams):
            for k in range(U):
                s.t_ref[k] = nxt[i][k]
        return tuple(n_prev) + tuple(state)

    carry = body(0, (tile_of(0),) * D + tuple(state), True)
    carry = lax.fori_loop(1, (count + U - 1) // U, lambda g, c: body(g, c, False), carry)
    flush(carry[:D], carry[D:])
    return [(carry[D + W * i], carry[D + W * i + 1]) for i in range(len(streams))]


def _aug_const(rows, col_vals):
    lane = lax.broadcasted_iota(jnp.int32, (rows, HEAD_DIM), 1)
    out = jnp.zeros((rows, HEAD_DIM), F32)
    for c, v in enumerate(col_vals):
        out = jnp.where(lane == c, v, out)
    return out


def _diff_kernel(lam_ref, q_ref, k_ref, v_ref, subln_ref, o_ref,
                 k1_ref, k2_ref, vt_ref, acc1_ref, acc2_ref, t_ref, p_ref, *, tq, n_heads, out_scale):
    h = pl.program_id(1)
    qi = pl.program_id(2)
    S = k_ref.shape[0]
    tk = tq
    slope = jnp.exp2(-8.0 * (h + 1).astype(F32) / n_heads)

    @pl.when(qi == 0)
    def _():
        jj = lax.broadcasted_iota(jnp.int32, (tk, 1), 0).astype(F32)
        kaug = _aug_const(tk, [jj]).astype(BF16)

        def body(c, carry):
            rows = pl.ds(pl.multiple_of(c * tk, tk), tk)
            kc = k_ref[rows, :]
            k1_ref[rows, :] = jnp.concatenate([kc[:, :HEAD_DIM], kaug], axis=1)
            k2_ref[rows, :] = jnp.concatenate([kc[:, HEAD_DIM:], kaug], axis=1)
            vt_ref[:, rows] = v_ref[rows, :].astype(F32).T.astype(BF16)
            return carry

        lax.fori_loop(0, S // tk, body, 0)

    qt = q_ref[...]
    qaug = _aug_const(tq, [slope]).astype(BF16)
    q1 = jnp.concatenate([qt[:, :HEAD_DIM], qaug], axis=1)
    q2 = jnp.concatenate([qt[:, HEAD_DIM:], qaug], axis=1)

    rows = lambda n: pl.ds(pl.multiple_of(n * tk, tk), tk)
    qpos = qi * tq + lax.broadcasted_iota(jnp.int32, (1, tq), 1)
    causal = lambda n: n * tk + lax.broadcasted_iota(jnp.int32, (tk, 1), 0) <= qpos
    shift = lambda n: slope * ((n - qi) * tk).astype(F32)
    vt_tile = lambda n: vt_ref[:, rows(n)]
    streams = [
        _Stream(q1, lambda n: k1_ref[rows(n), :], vt_tile, acc1_ref, t_ref.at[0], p_ref.at[0],
                shift, mask=causal),
        _Stream(q2, lambda n: k2_ref[rows(n), :], vt_tile, acc2_ref, t_ref.at[1], p_ref.at[1],
                shift, mask=causal),
    ]
    (m1, l1), (m2, l2) = _flash_sweep(streams, qi + 1, lambda pos: jnp.where(pos == 0, qi, pos - 1))

    d = acc1_ref[...] / l1 - lam_ref[0] * (acc2_ref[...] / l2)
    r = lax.rsqrt(jnp.mean(d * d, axis=0, keepdims=True) + RMS_EPS)
    o_ref[...] = (d * r).T * (subln_ref[...] * out_scale)


def _diff_attention(qkv, lam_f, subln, batch, seq, col_q, col_k, col_v, n_heads, out_scale, tq=256):
    dv = 2 * HEAD_DIM
    kern = functools.partial(_diff_kernel, tq=tq, n_heads=n_heads, out_scale=out_scale)
    grid_spec = pltpu.PrefetchScalarGridSpec(
        num_scalar_prefetch=1,
        grid=(batch, n_heads, seq // tq),
        in_specs=[
            pl.BlockSpec((None, tq, dv), lambda b, h, i, s: (b, i, col_q + h)),
            pl.BlockSpec((None, seq, dv), lambda b, h, i, s: (b, 0, col_k + h)),
            pl.BlockSpec((None, seq, dv), lambda b, h, i, s: (b, 0, col_v + h)),
            pl.BlockSpec((1, dv), lambda b, h, i, s: (0, 0)),
        ],
        out_specs=pl.BlockSpec((None, tq, dv), lambda b, h, i, s: (b, i, h)),
        scratch_shapes=[
            pltpu.VMEM((seq, LANES), BF16), pltpu.VMEM((seq, LANES), BF16),
            pltpu.VMEM((dv, seq), BF16),
            pltpu.VMEM((dv, tq), F32), pltpu.VMEM((dv, tq), F32),
            pltpu.VMEM((2, FLASH_UNROLL, tq, tq), F32), pltpu.VMEM((2, FLASH_STAGED, tq, tq), BF16),
        ],
    )
    return pl.pallas_call(
        kern,
        grid_spec=grid_spec,
        out_shape=jax.ShapeDtypeStruct((batch, seq, n_heads * dv), F32),
        compiler_params=pltpu.CompilerParams(
            dimension_semantics=("parallel", "parallel", "arbitrary"),
            vmem_limit_bytes=VMEM_LIMIT),
        name="diff_attn",
    )(lam_f.reshape(1).astype(F32), qkv, qkv, qkv, subln.reshape(1, dv).astype(F32))


BAND = 128
DILATIONS = (1, 4, 16)


def _dilated_kernel(q_ref, kp_ref, kc_ref, vp_ref, vc_ref, o_ref, kbuf_ref, vbuf_ref, st_ref,
                    *, chunk, n_heads):
    pair = pl.program_id(1)
    ci = pl.program_id(2)
    kbuf_ref[:chunk, :] = kp_ref[...]
    kbuf_ref[chunk:, :] = kc_ref[...]
    vbuf_ref[:chunk, :] = vp_ref[...]
    vbuf_ref[chunk:, :] = vc_ref[...]

    qi = lax.broadcasted_iota(jnp.int32, (BAND, 2 * BAND), 0)
    kj = lax.broadcasted_iota(jnp.int32, (BAND, 2 * BAND), 1)
    dist = qi + BAND - kj
    in_band = jnp.abs(2 * dist - BAND) <= BAND
    dist_f = dist.astype(F32)
    key_col = lax.broadcasted_iota(jnp.int32, (1, 2 * BAND), 1)
    slopes = [jnp.exp2(-8.0 * (2 * pair + e + 1).astype(F32) / n_heads) for e in range(2)]
    lane_q = lax.broadcasted_iota(jnp.int32, (BAND, LANES), 1)
    own_half = [lane_q < HEAD_DIM, lane_q >= HEAD_DIM]

    for pi, dil in enumerate(DILATIONS):
        span = dil * BAND
        n_blk = chunk // span

        def block(idx, carry, dil=dil, span=span, n_blk=n_blk, pi=pi):
            r = idx // n_blk
            b = idx % n_blk
            q0 = r + b * span
            qb = q_ref[pl.ds(q0, BAND, stride=dil), :]
            kb = kbuf_ref[pl.ds(chunk + q0 - span, 2 * BAND, stride=dil), :]
            vb = vbuf_ref[pl.ds(chunk + q0 - span, 2 * BAND, stride=dil), :]
            first = jnp.logical_and(ci == 0, b == 0)
            before_start = jnp.where(key_col < jnp.where(first, BAND, 0), NEG, 0.0)
            kb16 = kb.astype(BF16)
            vb16 = vb.astype(BF16)
            packed = []
            for e in range(2):
                qe = jnp.where(own_half[e], qb, 0.0).astype(BF16)
                s = _dot_nt(qe, kb16)
                s = jnp.where(in_band, s - (slopes[e] * dil) * dist_f, NEG) + before_start
                m = jnp.max(s, axis=1, keepdims=True)
                p = jnp.exp(s - m)
                l = jnp.sum(p, axis=1, keepdims=True)
                o = _dot(p.astype(BF16), vb16) / l
                packed.append(jnp.where(own_half[e], o, m + jnp.log(l)))
            for e in range(2):
                st_ref[pi, e, pl.ds(q0, BAND, stride=dil), :] = packed[e]
            return carry

        lax.fori_loop(0, dil * n_blk, block, 0, unroll=16)

    rows_per = 256
    lane = lax.broadcasted_iota(jnp.int32, (rows_per, LANES), 1)

    def merge(c, carry):
        rows = pl.ds(pl.multiple_of(c * rows_per, rows_per), rows_per)
        outs = []
        for e in range(2):
            xs = [st_ref[pi, e, rows, :] for pi in range(len(DILATIONS))]
            top = functools.reduce(jnp.maximum, xs)
            num = jnp.zeros((rows_per, LANES), F32)
            den = jnp.zeros((rows_per, LANES), F32)
            for x in xs:
                w = pltpu.roll(jnp.exp(x - top), HEAD_DIM, axis=1)
                num = num + w * x
                den = den + w
            outs.append(num / den)
        o_ref[rows, :] = jnp.where(lane < HEAD_DIM, outs[0], outs[1])
        return carry

    lax.fori_loop(0, chunk // rows_per, merge, 0)


def _dilated_attention(act, batch, seq, col_q, col_k, col_v, n_heads, chunk=2048):
    assert chunk % (max(DILATIONS) * BAND) == 0 and seq % chunk == 0
    prev = lambda c: jnp.maximum(c - 1, 0)
    return pl.pallas_call(
        functools.partial(_dilated_kernel, chunk=chunk, n_heads=n_heads),
        grid=(batch, n_heads // 2, seq // chunk),
        in_specs=[
            pl.BlockSpec((None, chunk, LANES), lambda b, p, c: (b, c, col_q + p)),
            pl.BlockSpec((None, chunk, LANES), lambda b, p, c: (b, prev(c), col_k + p)),
            pl.BlockSpec((None, chunk, LANES), lambda b, p, c: (b, c, col_k + p)),
            pl.BlockSpec((None, chunk, LANES), lambda b, p, c: (b, prev(c), col_v + p)),
            pl.BlockSpec((None, chunk, LANES), lambda b, p, c: (b, c, col_v + p)),
        ],
        out_specs=pl.BlockSpec((None, chunk, LANES), lambda b, p, c: (b, c, p)),
        out_shape=jax.ShapeDtypeStruct((batch, seq, n_heads * HEAD_DIM), F32),
        scratch_shapes=[
            pltpu.VMEM((2 * chunk, LANES), F32),
            pltpu.VMEM((2 * chunk, LANES), F32),
            pltpu.VMEM((len(DILATIONS), 2, chunk, LANES), F32),
        ],
        compiler_params=pltpu.CompilerParams(
            dimension_semantics=("parallel", "parallel", "arbitrary"),
            vmem_limit_bytes=VMEM_LIMIT),
        name="dilated_attn",
    )(act, act, act, act, act)


TAKEN = -3e38


def _topk_member(cur, k):
    n = cur.shape[0]
    row = lax.broadcasted_iota(jnp.int32, cur.shape, 0)
    sel = jnp.zeros(cur.shape, F32)
    for _ in range(k):
        mx = jnp.max(cur, axis=0, keepdims=True)
        first = jnp.min(jnp.where(cur == mx, row, n), axis=0, keepdims=True)
        pick = row == first
        sel = jnp.where(pick, 1.0, sel)
        cur = jnp.where(pick, TAKEN, cur)
    return sel > 0.5


def _moba_kernel(q_ref, k_ref, v_ref, o_ref, kaug_ref, vt_ref, kmean_ref, selb_ref, acc_ref,
                 t_ref, p_ref, *, blk, topk, n_heads):
    pair = pl.program_id(1)
    qi = pl.program_id(2)
    S = k_ref.shape[0]
    nblk = S // blk
    tq = tk = blk

    @pl.when(qi == 0)
    def _():
        jj = lax.broadcasted_iota(jnp.int32, (tk, 1), 0).astype(F32)
        kaug = _aug_const(tk, [jj]).astype(BF16)

        def body(c, carry):
            rows = pl.ds(pl.multiple_of(c * tk, tk), tk)
            kc = k_ref[rows, :]
            vc = v_ref[rows, :].astype(F32).T.astype(BF16)
            for e in range(2):
                kaug_ref[e, rows, :] = jnp.concatenate(
                    [kc[:, e * HEAD_DIM:(e + 1) * HEAD_DIM], kaug], axis=1)
                vt_ref[e, :, rows] = vc[e * HEAD_DIM:(e + 1) * HEAD_DIM, :]
            kmean_ref[pl.ds(c, 1), :] = jnp.mean(kc.astype(F32), axis=0, keepdims=True)
            return carry

        lax.fori_loop(0, nblk, body, 0)

    qt = q_ref[...]
    blk_row = lax.broadcasted_iota(jnp.int32, (nblk, tq), 0)
    past = blk_row < qi
    qs, slopes = [], []
    for e in range(2):
        slope = jnp.exp2(-8.0 * (2 * pair + e + 1).astype(F32) / n_heads)
        qe = qt[:, e * HEAD_DIM:(e + 1) * HEAD_DIM]
        km_hi, km_lo = _split_bf16(kmean_ref[:, e * HEAD_DIM:(e + 1) * HEAD_DIM])
        gate = _dot_nt(km_hi, qe) + _dot_nt(km_lo, qe)
        sel = _topk_member(jnp.where(past, gate, -2e38), topk)
        selb_ref[e] = jnp.where(blk_row == qi, 0.0, jnp.where(past, jnp.where(sel, 0.0, NEG), NEG))
        qs.append(jnp.concatenate([qe, _aug_const(tq, [slope]).astype(BF16)], axis=1))
        slopes.append(slope)

    rows = lambda n: pl.ds(pl.multiple_of(n * tk, tk), tk)
    qpos = qi * tq + lax.broadcasted_iota(jnp.int32, (1, tq), 1)
    causal = lambda n: n * tk + lax.broadcasted_iota(jnp.int32, (tk, 1), 0) <= qpos
    streams = [
        _Stream(qs[e],
                functools.partial(lambda n, e: kaug_ref[e, rows(n), :], e=e),
                functools.partial(lambda n, e: vt_ref[e, :, rows(n)], e=e),
                acc_ref.at[e], t_ref.at[e], p_ref.at[e],
                functools.partial(lambda n, e: slopes[e] * ((n - qi) * tk).astype(F32), e=e),
                bias=functools.partial(lambda n, e: selb_ref[e, pl.ds(n, 1), :], e=e),
                mask=causal)
        for e in range(2)]
    stats = _flash_sweep(streams, qi + 1, lambda pos: jnp.where(pos == 0, qi, pos - 1))
    o_ref[...] = jnp.concatenate([acc_ref[e] / stats[e][1] for e in range(2)], axis=0).T


def _moba_attention(qkv, batch, seq, col_q, col_k, col_v, n_heads, blk, topk):
    kern = functools.partial(_moba_kernel, blk=blk, topk=topk, n_heads=n_heads)
    nblk = seq // blk
    return pl.pallas_call(
        kern,
        grid=(batch, n_heads // 2, nblk),
        in_specs=[
            pl.BlockSpec((None, blk, LANES), lambda b, p, i: (b, i, col_q + p)),
            pl.BlockSpec((None, seq, LANES), lambda b, p, i: (b, 0, col_k + p)),
            pl.BlockSpec((None, seq, LANES), lambda b, p, i: (b, 0, col_v + p)),
        ],
        out_specs=pl.BlockSpec((None, blk, LANES), lambda b, p, i: (b, i, p)),
        out_shape=jax.ShapeDtypeStruct((batch, seq, n_heads * HEAD_DIM), F32),
        scratch_shapes=[
            pltpu.VMEM((2, seq, LANES), BF16),
            pltpu.VMEM((2, HEAD_DIM, seq), BF16),
            pltpu.VMEM((nblk, LANES), F32),
            pltpu.VMEM((2, nblk, blk), F32),
            pltpu.VMEM((2, HEAD_DIM, blk), F32),
            pltpu.VMEM((2, FLASH_UNROLL, blk, blk), F32),
            pltpu.VMEM((2, FLASH_STAGED, blk, blk), BF16),
        ],
        compiler_params=pltpu.CompilerParams(
            dimension_semantics=("parallel", "parallel", "arbitrary"),
            vmem_limit_bytes=VMEM_LIMIT),
        name="moba_attn",
    )(qkv, qkv, qkv)


CMP_STRIDE = 16
CMP_LEN = 32
SLC_BLK = 64
SLC_TOPK = 16
NSA_WINDOW = 512
NSA_TQ = 256
NSA_TK = 256


def _compress_kernel(c_ref, pe_ref, w1_ref, w2_ref, g_ref, o_ref, *, is_key):
    half = c_ref.shape[1]
    n = c_ref.shape[0]
    c = c_ref[...]
    w1 = w1_ref[...]
    a_lo = _dot(c, w1[:half])
    a_hi = _dot(c, w1[half:])
    pe_hi, pe_lo = _split_bf16(pe_ref[...])
    c0 = (_dot(pe_hi, w1) + _dot(pe_lo, w1))[0:1]
    y = a_lo + pltpu.roll(a_hi, n - 1, axis=0) + c0
    hid = (y * jax.nn.sigmoid(y)).astype(BF16)
    if is_key:
        z = _dot(hid, w2_ref[...])
    else:
        z = _dot(hid.astype(F32), w2_ref[...].astype(F32))
    if is_key:
        ms = jnp.sum(z * z, axis=-1, keepdims=True) * (1.0 / HEAD_DIM)
        kn = z * lax.rsqrt(ms + RMS_EPS) * g_ref[...]
        i = lax.broadcasted_iota(jnp.int32, (n, 1), 0)
        aug = _aug_const(n, [(i // 16 * 256).astype(F32), (i % 16 * 16).astype(F32), 1.0])
        o_ref[...] = jnp.concatenate([kn[:, :HEAD_DIM], aug], axis=1).astype(BF16)
    else:
        o_ref[...] = z.T[:HEAD_DIM, :].astype(BF16)


def _compress(chunks, pe, w1, w2, gain, is_key):
    B, n, half = chunks.shape
    Hkv = half // (CMP_STRIDE * HEAD_DIM)
    hidden = w1.shape[1]

    def per_head(t, tail):
        t = t.reshape(2, CMP_STRIDE, 1, HEAD_DIM, *tail)
        onehot = jnp.eye(Hkv, dtype=t.dtype).reshape(Hkv, 1, 1, Hkv, 1, *([1] * len(tail)))
        return (onehot * t[None]).reshape(Hkv, 2 * half, *tail)

    pe8 = jnp.broadcast_to(per_head(pe.reshape(-1).astype(F32), ())[:, None, :], (Hkv, 8, 2 * half))
    w1 = per_head(w1, (hidden,))
    w2p = jnp.pad(w2, ((0, 0), (0, LANES - w2.shape[1]))).astype(BF16)
    g = jnp.pad(gain.astype(F32), (0, LANES - HEAD_DIM)).reshape(1, LANES)
    if is_key:
        out_shape = jax.ShapeDtypeStruct((B, Hkv, n, LANES), BF16)
        out_spec = pl.BlockSpec((None, None, n, LANES), lambda b, h: (b, h, 0, 0))
    else:
        out_shape = jax.ShapeDtypeStruct((B, Hkv, HEAD_DIM, n), BF16)
        out_spec = pl.BlockSpec((None, None, HEAD_DIM, n), lambda b, h: (b, h, 0, 0))
    return pl.pallas_call(
        functools.partial(_compress_kernel, is_key=is_key),
        grid=(B, Hkv),
        in_specs=[
            pl.BlockSpec((None, n, half), lambda b, h: (b, 0, 0)),
            pl.BlockSpec((None, 8, 2 * half), lambda b, h: (h, 0, 0)),
            pl.BlockSpec((None, 2 * half, hidden), lambda b, h: (h, 0, 0)),
            pl.BlockSpec((hidden, LANES), lambda b, h: (0, 0)),
            pl.BlockSpec((1, LANES), lambda b, h: (0, 0)),
        ],
        out_specs=out_spec,
        out_shape=out_shape,
        compiler_params=pltpu.CompilerParams(
            dimension_semantics=("parallel", "parallel"), vmem_limit_bytes=VMEM_LIMIT),
        name="nsa_compress_k" if is_key else "nsa_compress_v",
    )(chunks, pe8, w1.astype(BF16), w2p, g)


CMP_ROWS = 128


def _nsa_cmp_kernel(q_ref, kc_ref, vct_ref, ct_ref, o_ref, selb_ref, used_ref, imp_ref,
                    *, tq, group, n_heads):
    g = pl.program_id(1)
    qi = pl.program_id(2)
    n_cmp = kc_ref.shape[0]
    n_slc = ct_ref.shape[0]
    i0 = qi * tq
    qpos = i0 + lax.broadcasted_iota(jnp.int32, (1, tq), 1)
    qt = q_ref[...]

    def attend(rows):
        cend = lax.broadcasted_iota(jnp.int32, (rows, 1), 0) * CMP_STRIDE + (CMP_LEN - 1)
        valid = cend <= qpos
        kc = kc_ref[:rows, :]
        vct = vct_ref[:, :rows]
        ts = []
        for e in range(group):
            slope = jnp.exp2(-8.0 * (group * g + e + 1).astype(F32) / n_heads)
            aug = _aug_const(tq, [slope, slope, -slope * i0.astype(F32)]).astype(BF16)
            qa = jnp.concatenate([qt[:, e * HEAD_DIM:(e + 1) * HEAD_DIM], aug], axis=1)
            ts.append(_dot_nt(kc, qa))
        psum = jnp.zeros((rows, tq), F32)
        ps = []
        for t in ts:
            t = jnp.where(valid, t, NEG)
            m = jnp.max(t, axis=0, keepdims=True)
            p = jnp.where(valid, jnp.exp(t - m), 0.0)
            l = jnp.sum(p, axis=0, keepdims=True)
            p = p / jnp.where(l > 0, l, 1.0)
            ps.append(p.astype(BF16))
            psum = psum + p
        o_ref[...] = jnp.concatenate([_dot(vct, p) for p in ps], axis=0).T
        p_hi, p_lo = _split_bf16(psum)
        ct = ct_ref[:, :rows]
        imp_ref[...] = _dot(ct, p_hi) + _dot(ct, p_lo)

    needed = (i0 + tq - CMP_LEN) // CMP_STRIDE + 1
    n_var = max(n_cmp // CMP_ROWS, 1)
    for v in range(n_var):
        rows = n_cmp if v == n_var - 1 else (v + 1) * CMP_ROWS
        cond = needed > v * CMP_ROWS
        if v < n_var - 1:
            cond = jnp.logical_and(cond, needed <= (v + 1) * CMP_ROWS)
        pl.when(cond)(functools.partial(attend, rows))

    imp = imp_ref[...]
    jb = lax.broadcasted_iota(jnp.int32, (n_slc, tq), 0)
    qb = qpos // SLC_BLK
    cur = jnp.where(jb > qb, -2e38, imp)
    for forced in (0, qb, qb - 1):
        cur = jnp.where(jb == forced, 3e38, cur)
    sel = _topk_member(cur, min(SLC_TOPK, n_slc))
    selb = jnp.where(jb <= qb, jnp.where(sel, 0.0, NEG), NEG).T
    selb_ref[...] = selb.astype(BF16)
    for c in range(tq // NSA_TQ):
        hit = jnp.max(selb[c * NSA_TQ:(c + 1) * NSA_TQ], axis=0, keepdims=True) == 0.0
        used_ref[c] = jnp.where(hit, 1, 0)


def _nsa_cmp(act, kc, vct, batch, seq, col_q, n_heads, group, tq=256):
    Hkv = n_heads // group
    n_cmp = kc.shape[2]
    n_slc = seq // SLC_BLK
    cs = jnp.arange(n_cmp) * CMP_STRIDE
    ss = jnp.arange(n_slc) * SLC_BLK
    overlap = (cs[None, :] <= ss[:, None] + SLC_BLK - 1) & (cs[None, :] + CMP_LEN - 1 >= ss[:, None])
    overlap = overlap & (cs[None, :] + CMP_LEN <= seq)
    ct = overlap.astype(BF16)
    gw = group * HEAD_DIM
    return pl.pallas_call(
        functools.partial(_nsa_cmp_kernel, tq=tq, group=group, n_heads=n_heads),
        grid=(batch, Hkv, seq // tq),
        in_specs=[
            pl.BlockSpec((None, tq, gw), lambda b, g, i: (b, i, col_q + g)),
            pl.BlockSpec((None, None, n_cmp, LANES), lambda b, g, i: (b, g, 0, 0)),
            pl.BlockSpec((None, None, HEAD_DIM, n_cmp), lambda b, g, i: (b, g, 0, 0)),
            pl.BlockSpec((n_slc, n_cmp), lambda b, g, i: (0, 0)),
        ],
        out_specs=[
            pl.BlockSpec((None, tq, gw), lambda b, g, i: (b, i, g)),
            pl.BlockSpec((None, None, tq, n_slc), lambda b, g, i: (b, g, i, 0)),
            pl.BlockSpec((None, None, tq // NSA_TQ, 1, n_slc), lambda b, g, i: (b, g, i, 0, 0)),
        ],
        out_shape=[
            jax.ShapeDtypeStruct((batch, seq, n_heads * HEAD_DIM), F32),
            jax.ShapeDtypeStruct((batch, Hkv, seq, n_slc), BF16),
            jax.ShapeDtypeStruct((batch, Hkv, seq // NSA_TQ, 1, n_slc), jnp.int32),
        ],
        scratch_shapes=[pltpu.VMEM((n_slc, tq), F32)],
        compiler_params=pltpu.CompilerParams(
            dimension_semantics=("parallel", "parallel", "parallel"),
            vmem_limit_bytes=VMEM_LIMIT),
        name="nsa_cmp",
    )(act, kc, vct, ct)


def _nsa_main_kernel(used_ref, q_ref, selb_ref, kvs_ref, kvw_ref, kconst_ref, ocmp_ref, gate_ref,
                     o_ref, ks_ref, vst_ref, kw_ref, vwt_ref, accs_ref, accw_ref, gt_ref, t_ref,
                     p_ref, tiles_ref, *, tq, tk, group, n_heads):
    g = pl.program_id(1)
    qi = pl.program_id(2)
    S = kvs_ref.shape[0]
    R = group * tq
    i0 = qi * tq

    @pl.when(qi == 0)
    def _():
        def body(c, carry):
            rows = pl.ds(pl.multiple_of(c * tk, tk), tk)
            kvs = kvs_ref[rows, :]
            kvw = kvw_ref[rows, :]
            kconst = kconst_ref[...]
            ks_ref[rows, :HEAD_DIM] = kvs[:, :HEAD_DIM]
            ks_ref[rows, HEAD_DIM:LANES] = kconst[:, :HEAD_DIM]
            kw_ref[rows, :] = jnp.concatenate([kvw[:, :HEAD_DIM], kconst[:, :HEAD_DIM]], axis=1)
            vst_ref[:, rows] = kvs.astype(F32).T[HEAD_DIM:, :].astype(BF16)
            vwt_ref[:, rows] = kvw.astype(F32).T[HEAD_DIM:, :].astype(BF16)
            key_blk = (c * tk + lax.broadcasted_iota(jnp.int32, (tk, n_slc), 0)) // SLC_BLK
            onehot = key_blk == lax.broadcasted_iota(jnp.int32, (tk, n_slc), 1)
            ks_ref[rows, LANES:] = jnp.where(onehot, 1.0, 0.0).astype(BF16)
            return carry

        n_slc = ks_ref.shape[1] - LANES
        lax.fori_loop(0, S // tk, body, 0)

    qt = q_ref[...]
    selb = selb_ref[...]
    qs_parts, qw_parts, slope_parts = [], [], []
    for e in range(group):
        slope = jnp.exp2(-8.0 * (group * g + e + 1).astype(F32) / n_heads)
        aug = _aug_const(tq, [slope]).astype(BF16)
        qe = qt[:, e * HEAD_DIM:(e + 1) * HEAD_DIM]
        qw_parts.append(jnp.concatenate([qe, aug], axis=1))
        qs_parts.append(jnp.concatenate([qe, aug, selb], axis=1))
        slope_parts.append(jnp.full((1, tq), slope, F32))
    q_slc = jnp.concatenate(qs_parts, axis=0)
    q_win = jnp.concatenate(qw_parts, axis=0)
    slope_row = jnp.concatenate(slope_parts, axis=1)
    qpos = i0 + lax.broadcasted_iota(jnp.int32, (1, R), 1) % tq

    nd = i0 // tk
    rows = lambda n: pl.ds(pl.multiple_of(n * tk, tk), tk)
    shift = lambda n: slope_row * (n * tk - i0).astype(F32)
    kpos = lambda n: n * tk + lax.broadcasted_iota(jnp.int32, (tk, 1), 0)
    causal = lambda n: kpos(n) <= qpos

    n_tiles = S // tk
    base = ((pl.program_id(0) * pl.num_programs(1) + g) * pl.num_programs(2) + qi) * n_tiles

    def collect(n, cnt):
        tiles_ref[cnt] = n
        return cnt + jnp.where(used_ref[base + n] != 0, 1, 0)

    tiles_ref[0] = nd
    n_used = lax.fori_loop(0, nd, collect, 1)
    assert 2 * tk == NSA_WINDOW and tk % tq == 0
    tw, offs = [], []
    for c in range(3):
        n = nd - 2 + c
        t = _dot_nt(kw_ref[rows(jnp.maximum(n, 0)), :], q_win)
        if c == 0:
            t = jnp.where(qpos - kpos(n) < NSA_WINDOW, t, NEG)
        elif c == 2:
            t = jnp.where(kpos(n) <= qpos, t, NEG)
        tw.append(t)
        offs.append(shift(n) + jnp.where(n >= 0, 0.0, NEG))
    mw = functools.reduce(jnp.maximum,
                          [jnp.max(t, axis=0, keepdims=True) + off for t, off in zip(tw, offs)])
    lw = jnp.zeros((1, R), F32)
    o_w = jnp.zeros((HEAD_DIM, R), F32)
    for c in range(3):
        p = jnp.exp(tw[c] - (mw - offs[c]))
        lw = lw + jnp.sum(p, axis=0, keepdims=True)
        o_w = o_w + _dot(vwt_ref[:, rows(jnp.maximum(nd - 2 + c, 0))], p.astype(BF16))
    accw_ref[...] = o_w

    slc = _Stream(q_slc, lambda n: ks_ref[rows(n), :], lambda n: vst_ref[:, rows(n)],
                  accs_ref, t_ref, p_ref, shift, mask=causal)
    ((_, ls),) = _flash_sweep([slc], n_used, lambda pos: tiles_ref[pos])

    gz = gate_ref[...]
    gt_ref[...] = jax.nn.sigmoid(gz).T
    oc_t = ocmp_ref[...].T
    outs = []
    for e in range(group):
        head = group * g + e
        cols = slice(e * tq, (e + 1) * tq)
        o_s = accs_ref[:, cols] / ls[:, cols]
        o_w = accw_ref[:, cols] / lw[:, cols]
        o_c = oc_t[e * HEAD_DIM:(e + 1) * HEAD_DIM, :]
        g_c = gt_ref[pl.ds(3 * head, 1), :]
        g_s = gt_ref[pl.ds(3 * head + 1, 1), :]
        g_w = gt_ref[pl.ds(3 * head + 2, 1), :]
        outs.append(g_c * o_c + g_s * o_s + g_w * o_w)
    o_ref[...] = jnp.concatenate(outs, axis=0).T


def _nsa_main(act, selb, used, ocmp, gates, batch, seq, col_q, col_kvs, col_kvw, col_gate, n_heads,
              group):
    tq, tk = NSA_TQ, NSA_TK
    Hkv = n_heads // group
    n_slc = seq // SLC_BLK
    gw = group * HEAD_DIM
    R = group * tq
    jj = jnp.arange(tk, dtype=F32)
    kconst = jnp.zeros((tk, HEAD_DIM), F32).at[:, 0].set(jj).astype(BF16)
    blk_per_tile = tk // SLC_BLK
    used_tiles = jnp.max(used.reshape(batch, Hkv, seq // tq, n_slc // blk_per_tile, blk_per_tile),
                         axis=-1).reshape(-1).astype(jnp.int32)
    grid_spec = pltpu.PrefetchScalarGridSpec(
        num_scalar_prefetch=1,
        grid=(batch, Hkv, seq // tq),
        in_specs=[
            pl.BlockSpec((None, tq, gw), lambda b, g, i, u: (b, i, col_q + g)),
            pl.BlockSpec((None, None, tq, n_slc), lambda b, g, i, u: (b, g, i, 0)),
            pl.BlockSpec((None, seq, LANES), lambda b, g, i, u: (b, 0, col_kvs + g)),
            pl.BlockSpec((None, seq, LANES), lambda b, g, i, u: (b, 0, col_kvw + g)),
            pl.BlockSpec((tk, HEAD_DIM), lambda b, g, i, u: (0, 0)),
            pl.BlockSpec((None, tq, gw), lambda b, g, i, u: (b, i, g)),
            pl.BlockSpec((None, tq, LANES), lambda b, g, i, u: (b, i, col_gate)),
        ],
        out_specs=pl.BlockSpec((None, tq, gw), lambda b, g, i, u: (b, i, g)),
        scratch_shapes=[
            pltpu.VMEM((seq, LANES + n_slc), BF16),
            pltpu.VMEM((HEAD_DIM, seq), BF16),
            pltpu.VMEM((seq, LANES), BF16),
            pltpu.VMEM((HEAD_DIM, seq), BF16),
            pltpu.VMEM((HEAD_DIM, R), F32),
            pltpu.VMEM((HEAD_DIM, R), F32),
            pltpu.VMEM((LANES, tq), F32),
            pltpu.VMEM((FLASH_UNROLL, tk, R), F32), pltpu.VMEM((FLASH_STAGED, tk, R), BF16),
            pltpu.SMEM((seq // tk,), jnp.int32),
        ],
    )
    return pl.pallas_call(
        functools.partial(_nsa_main_kernel, tq=tq, tk=tk, group=group, n_heads=n_heads),
        grid_spec=grid_spec,
        out_shape=jax.ShapeDtypeStruct((batch, seq, n_heads * HEAD_DIM), F32),
        compiler_params=pltpu.CompilerParams(
            dimension_semantics=("parallel", "parallel", "arbitrary"),
            vmem_limit_bytes=VMEM_LIMIT),
        name="nsa_main",
    )(used_tiles, act, selb, act, act, kconst, ocmp, gates)


def _tile_gain(g, width):
    return jnp.tile(g.astype(F32), width // HEAD_DIM)


def _even_layer(x, ln, w_in, qkn, lam, subln, w_out, layer):
    B, S, D = x.shape
    M = B * S
    x2d = x.reshape(M, D)
    W = 512
    aq, ak, av, az, bq, bk, bv, bz = [w_in[:, i * W:(i + 1) * W] for i in range(8)]
    scale = HEAD_DIM ** -0.5
    ones, zeros = jnp.ones((W,), F32), jnp.zeros((W,), F32)
    w_b = jnp.concatenate([bq, bk, bv], axis=1).astype(BF16)
    gain_b = jnp.concatenate([_tile_gain(qkn[2], W) * scale, _tile_gain(qkn[3], W), ones])
    mask_b = np.repeat([1.0, 1.0, 0.0], W)
    act_b = _project(x2d, ln, w_b, gain_b, mask_b, BF16)
    w_a = jnp.concatenate([aq, ak, av, az, bz], axis=1).astype(BF16)
    gain_a = jnp.concatenate([_tile_gain(qkn[0], W) * scale, _tile_gain(qkn[1], W), ones, ones, ones])
    mask_a = np.repeat([1.0, 1.0, 0.0, 0.0, 0.0], W)
    act_a = _project(x2d, ln, w_a, gain_a, mask_a, F32)

    lam_init = 0.8 - 0.6 * math.exp(-0.3 * layer)
    lf = lam.astype(F32)
    lam_f = jnp.exp(jnp.sum(lf[0] * lf[1])) - jnp.exp(jnp.sum(lf[2] * lf[3])) + lam_init
    ob = _diff_attention(act_b.reshape(B, S, 3 * W), lam_f, subln, B, S, 0, 4, 8, 4, 1.0 - lam_init)
    oa = _dilated_attention(act_a.reshape(B, S, 5 * W), B, S, 0, 4, 8, 8)
    out = _out_project(x2d, oa.reshape(M, W), ob.reshape(M, W), act_a, 3, 4, w_out)
    return out.reshape(B, S, D)


def _odd_layer(x, ln, w_in, qkn, phi_pe, phi_w1, phi_w2, w_out):
    B, S, D = x.shape
    M = B * S
    x2d = x.reshape(M, D)
    W, KW = 512, 128
    offs = [0]
    for s in (W, W, W, W, W, KW, KW, KW, KW, KW, KW, W, 24):
        offs.append(offs[-1] + s)
    cq, ck, cv, cz, dq, dkc, dvc, dks, dvs, dkw, dvw, dz, dg = [
        w_in[:, offs[i]:offs[i + 1]] for i in range(13)]
    scale = HEAD_DIM ** -0.5
    hd = HEAD_DIM
    ones = lambda n: jnp.ones((n,), F32)
    zeros = lambda n: jnp.zeros((n,), F32)
    kv_pair = lambda k, v: jnp.concatenate([k[:, :hd], v[:, :hd], k[:, hd:], v[:, hd:]], axis=1)
    w_b = jnp.concatenate([cq, ck, cv, dq, kv_pair(dks, dvs), kv_pair(dkw, dvw), dkc, dvc,
                           jnp.zeros((D, 2 * KW), w_in.dtype)], axis=1).astype(BF16)
    kv_gain = lambda g: jnp.concatenate([g, ones(hd), g, ones(hd)])
    kv_mask = np.repeat([1.0, 0.0, 1.0, 0.0], hd)
    gain_b = jnp.concatenate([_tile_gain(qkn[0], W) * scale, _tile_gain(qkn[1], W), ones(W),
                              _tile_gain(qkn[2], W) * scale, kv_gain(qkn[4].astype(F32)),
                              kv_gain(qkn[5].astype(F32)), ones(4 * KW)])
    mask_b = np.concatenate([np.repeat([1.0, 1.0, 0.0, 1.0], W), kv_mask, kv_mask, np.zeros(4 * KW)])
    act_b = _project(x2d, ln, w_b, gain_b, mask_b, BF16).reshape(B, S, 6 * W)
    w_g = jnp.concatenate([cz, dz, dg, jnp.zeros((D, W - 24), w_in.dtype)], axis=1).astype(BF16)
    act_g = _project(x2d, ln, w_g, ones(3 * W), np.zeros(3 * W), F32)

    oc = _moba_attention(act_b, B, S, 0, 4, 8, 8, 256, 3)

    def chunks(t):
        return t.reshape(B, S // CMP_STRIDE, CMP_STRIDE * 2 * hd)

    kc = _compress(chunks(act_b[:, :, 20 * KW:21 * KW]), phi_pe[0], phi_w1[0], phi_w2[0], qkn[3], True)
    vct = _compress(chunks(act_b[:, :, 21 * KW:22 * KW]), phi_pe[1], phi_w1[1], phi_w2[1], qkn[3], False)
    ocmp, selb, used = _nsa_cmp(act_b, kc, vct, B, S, 6, 8, 4)
    od = _nsa_main(act_b, selb, used, ocmp, act_g.reshape(B, S, 3 * W), B, S, 6, 16, 18, 8, 8, 4)
    out = _out_project(x2d, oc.reshape(M, W), od.reshape(M, W), act_g, 0, 1, w_out)
    return out.reshape(B, S, D)


def kernel(x, ln_e, w_in_e, qkn_e, lam_e, subln_e, w_out_e, ln_o, w_in_o, qkn_o, phi_pe, phi_w1, phi_w2, w_out_o):
    n_layers = ln_e.shape[0] + ln_o.shape[0]
    for layer in range(n_layers):
        i = layer // 2
        if layer % 2 == 0:
            x = _even_layer(x, ln_e[i], w_in_e[i], qkn_e[i], lam_e[i], subln_e[i], w_out_e[i], layer)
        else:
            x = _odd_layer(x, ln_o[i], w_in_o[i], qkn_o[i], phi_pe[i], phi_w1[i], phi_w2[i], w_out_o[i])
    return x
```

```python
import functools
import math

import jax
import jax.numpy as jnp
import numpy as np
from jax import lax
from jax.experimental import pallas as pl
from jax.experimental.pallas import tpu as pltpu

HEAD_DIM = 64
LANES = 128
RMS_EPS = 1e-6
NEG = -2e30
M_INIT = -1e30
VMEM_LIMIT = 56 * 1024 * 1024

F32 = jnp.float32
BF16 = jnp.bfloat16

_NT = (((1,), (1,)), ((), ()))


def _dot(a, b):
    return jnp.dot(a, b, preferred_element_type=F32)


def _dot_nt(a, b):
    return lax.dot_general(a, b, _NT, preferred_element_type=F32)


def _alibi_slopes(n):
    return [2.0 ** (-8.0 * (i + 1) / n) for i in range(n)]


def _split_bf16(x):
    hi = x.astype(BF16)
    lo = (x - hi.astype(F32)).astype(BF16)
    return hi, lo


NORM_CHUNK = 256


def _proj_kernel(x_ref, ln_ref, w_ref, gain_ref, nmask_ref, bd_ref, o_ref, *, norm_chunks):
    x = x_ref[...]
    ms = jnp.mean(x * x, axis=-1, keepdims=True)
    h = (x * lax.rsqrt(ms + RMS_EPS) * ln_ref[...]).astype(BF16)
    bd = bd_ref[...]
    col = lambda c: slice(c * NORM_CHUNK, (c + 1) * NORM_CHUNK)
    ys = {}
    for c, normed in enumerate(norm_chunks):
        y = _dot(h, w_ref[:, col(c)])
        if normed:
            ys[c] = y
        else:
            o_ref[:, col(c)] = y.astype(o_ref.dtype)
    for c, y in ys.items():
        hi, lo = _split_bf16(y * y)
        ms = _dot(hi, bd) + _dot(lo, bd)
        yn = y * lax.rsqrt(ms + RMS_EPS) * gain_ref[:, col(c)]
        o_ref[:, col(c)] = jnp.where(nmask_ref[:, col(c)] > 0.5, yn, y).astype(o_ref.dtype)


def _project(x2d, ln, w, gain, nmask, out_dtype, tm=1024):
    M, D = x2d.shape
    N = w.shape[1]
    assert M % tm == 0 and N % NORM_CHUNK == 0
    norm_chunks = [bool(c) for c in np.asarray(nmask).reshape(-1, NORM_CHUNK).max(axis=1) > 0.5]
    nmask = jnp.asarray(nmask, F32)
    r = jnp.arange(NORM_CHUNK) // HEAD_DIM
    bd = jnp.where(r[:, None] == r[None, :], 1.0 / HEAD_DIM, 0.0).astype(BF16)
    return pl.pallas_call(
        functools.partial(_proj_kernel, norm_chunks=tuple(norm_chunks)),
        grid=(M // tm,),
        in_specs=[
            pl.BlockSpec((tm, D), lambda i: (i, 0)),
            pl.BlockSpec((1, D), lambda i: (0, 0)),
            pl.BlockSpec((D, N), lambda i: (0, 0)),
            pl.BlockSpec((1, N), lambda i: (0, 0)),
            pl.BlockSpec((1, N), lambda i: (0, 0)),
            pl.BlockSpec((NORM_CHUNK, NORM_CHUNK), lambda i: (0, 0)),
        ],
        out_specs=pl.BlockSpec((tm, N), lambda i: (i, 0)),
        out_shape=jax.ShapeDtypeStruct((M, N), out_dtype),
        compiler_params=pltpu.CompilerParams(
            dimension_semantics=("parallel",), vmem_limit_bytes=VMEM_LIMIT),
        name="proj",
    )(x2d, ln.reshape(1, D).astype(F32), w, gain.reshape(1, N).astype(F32),
      nmask.reshape(1, N).astype(F32), bd)


def _outproj_kernel(x_ref, oa_ref, ob_ref, za_ref, zb_ref, wa_ref, wb_ref, o_ref):
    za = za_ref[...]
    zb = zb_ref[...]
    ma = (oa_ref[...] * (za * jax.nn.sigmoid(za))).astype(BF16)
    mb = (ob_ref[...] * (zb * jax.nn.sigmoid(zb))).astype(BF16)
    o_ref[...] = x_ref[...] + _dot(ma, wa_ref[...]) + _dot(mb, wb_ref[...])


def _out_project(x2d, oa, ob, z, za_blk, zb_blk, w_out, tm=512):
    M, D = x2d.shape
    W = oa.shape[1]
    wa = w_out[:W].astype(BF16)
    wb = w_out[W:].astype(BF16)
    return pl.pallas_call(
        _outproj_kernel,
        grid=(M // tm,),
        in_specs=[
            pl.BlockSpec((tm, D), lambda i: (i, 0)),
            pl.BlockSpec((tm, W), lambda i: (i, 0)),
            pl.BlockSpec((tm, W), lambda i: (i, 0)),
            pl.BlockSpec((tm, W), lambda i: (i, za_blk)),
            pl.BlockSpec((tm, W), lambda i: (i, zb_blk)),
            pl.BlockSpec((W, D), lambda i: (0, 0)),
            pl.BlockSpec((W, D), lambda i: (0, 0)),
        ],
        out_specs=pl.BlockSpec((tm, D), lambda i: (i, 0)),
        out_shape=jax.ShapeDtypeStruct((M, D), F32),
        compiler_params=pltpu.CompilerParams(
            dimension_semantics=("parallel",), vmem_limit_bytes=VMEM_LIMIT),
        name="outproj",
    )(x2d, oa, ob, z, z, wa, wb)


class _Stream:
    def __init__(self, q, k_tile, vt_tile, acc_ref, t_ref, p_ref, shift, bias=None, mask=None,
                 mask_every_tile=False):
        self.q, self.k_tile, self.vt_tile = q, k_tile, vt_tile
        self.acc_ref, self.t_ref, self.p_ref = acc_ref, t_ref, p_ref
        self.shift, self.bias, self.mask = shift, bias, mask
        self.mask_every_tile = mask_every_tile

    def scores(self, n):
        return _dot_nt(self.k_tile(n), self.q)

    def softmax(self, t, n, m, l, pad, masked):
        if masked:
            t = jnp.where(self.mask(n), t, NEG)
        off = self.shift(n) + pad
        if self.bias is not None:
            off = off + self.bias(n)
        m_new = jnp.maximum(m, jnp.max(t, axis=0, keepdims=True) + off)
        p = jnp.exp(t - (m_new - off))
        alpha = jnp.exp(m - m_new)
        return m_new, alpha * l + jnp.sum(p, axis=0, keepdims=True), alpha, p.astype(BF16)


FLASH_UNROLL = 2
FLASH_STAGED = 1
FLASH_QK_FIRST = False


def _flash_sweep(streams, count, tile_of):
    U, D = FLASH_UNROLL, FLASH_STAGED
    R = streams[0].q.shape[0]
    last = count - 1
    W = 2 + D
    state = []
    for s in streams:
        s.acc_ref[...] = jnp.zeros_like(s.acc_ref)
        s.p_ref[...] = jnp.zeros_like(s.p_ref)
        for k in range(U):
            s.t_ref[k] = s.scores(tile_of(jnp.minimum(k, last)))
        state += [jnp.full((1, R), M_INIT, F32), jnp.zeros((1, R), F32)]
        state += [jnp.ones((1, R), F32)] * D

    def flush(n_prev, state):
        pvs = [[_dot(s.vt_tile(n_prev[d]), s.p_ref[d]) for d in range(D)] for s in streams]
        for i, s in enumerate(streams):
            acc = s.acc_ref[...]
            for d in range(D):
                acc = state[W * i + 2 + d] * acc + pvs[i][d]
            s.acc_ref[...] = acc

    def body(g, carry, first_body):
        n_prev, state = list(carry[:D]), list(carry[D:])
        base = U * g
        if FLASH_QK_FIRST:
            nxt = [[s.scores(tile_of(jnp.minimum(base + U + k, last))) for k in range(U)]
                   for s in streams]
            flush(n_prev, state)
        else:
            flush(n_prev, state)
            nxt = [[s.scores(tile_of(jnp.minimum(base + U + k, last))) for k in range(U)]
                   for s in streams]
        for k in range(U):
            n = tile_of(jnp.minimum(base + k, last))
            pad = jnp.where(base + k < count, 0.0, NEG)
            d = k - (U - D)
            for i, s in enumerate(streams):
                m, l = state[W * i:W * i + 2]
                masked = s.mask is not None and (s.mask_every_tile or (first_body and k == 0))
                m, l, a, p = s.softmax(s.t_ref[k], n, m, l, pad, masked)
                state[W * i:W * i + 2] = [m, l]
                if d < 0:
                    s.acc_ref[...] = a * s.acc_ref[...] + _dot(s.vt_tile(n), p)
                else:
                    state[W * i + 2 + d] = a
                    s.p_ref[d] = p
            if d >= 0:
                n_prev[d] = n
        for i, s in enumerate(streams):
            for k in range(U):
                s.t_ref[k] = nxt[i][k]
        return tuple(n_prev) + tuple(state)

    carry = body(0, (tile_of(0),) * D + tuple(state), True)
    carry = lax.fori_loop(1, (count + U - 1) // U, lambda g, c: body(g, c, False), carry)
    flush(carry[:D], carry[D:])
    return [(carry[D + W * i], carry[D + W * i + 1]) for i in range(len(streams))]


def _aug_const(rows, col_vals):
    lane = lax.broadcasted_iota(jnp.int32, (rows, HEAD_DIM), 1)
    out = jnp.zeros((rows, HEAD_DIM), F32)
    for c, v in enumerate(col_vals):
        out = jnp.where(lane == c, v, out)
    return out


def _diff_kernel(lam_ref, q_ref, k_ref, v_ref, subln_ref, o_ref,
                 k1_ref, k2_ref, vt_ref, acc1_ref, acc2_ref, t_ref, p_ref, *, tq, n_heads, out_scale):
    h = pl.program_id(1)
    qi = pl.program_id(2)
    S = k_ref.shape[0]
    tk = tq
    slope = jnp.exp2(-8.0 * (h + 1).astype(F32) / n_heads)

    @pl.when(qi == 0)
    def _():
        jj = lax.broadcasted_iota(jnp.int32, (tk, 1), 0).astype(F32)
        kaug = _aug_const(tk, [jj]).astype(BF16)

        def body(c, carry):
            rows = pl.ds(pl.multiple_of(c * tk, tk), tk)
            kc = k_ref[rows, :]
            k1_ref[rows, :] = jnp.concatenate([kc[:, :HEAD_DIM], kaug], axis=1)
            k2_ref[rows, :] = jnp.concatenate([kc[:, HEAD_DIM:], kaug], axis=1)
            vt_ref[:, rows] = v_ref[rows, :].astype(F32).T.astype(BF16)
            return carry

        lax.fori_loop(0, S // tk, body, 0)

    qt = q_ref[...]
    qaug = _aug_const(tq, [slope]).astype(BF16)
    q1 = jnp.concatenate([qt[:, :HEAD_DIM], qaug], axis=1)
    q2 = jnp.concatenate([qt[:, HEAD_DIM:], qaug], axis=1)

    rows = lambda n: pl.ds(pl.multiple_of(n * tk, tk), tk)
    qpos = qi * tq + lax.broadcasted_iota(jnp.int32, (1, tq), 1)
    causal = lambda n: n * tk + lax.broadcasted_iota(jnp.int32, (tk, 1), 0) <= qpos
    shift = lambda n: slope * ((n - qi) * tk).astype(F32)
    vt_tile = lambda n: vt_ref[:, rows(n)]
    streams = [
        _Stream(q1, lambda n: k1_ref[rows(n), :], vt_tile, acc1_ref, t_ref.at[0], p_ref.at[0],
                shift, mask=causal),
        _Stream(q2, lambda n: k2_ref[rows(n), :], vt_tile, acc2_ref, t_ref.at[1], p_ref.at[1],
                shift, mask=causal),
    ]
    (m1, l1), (m2, l2) = _flash_sweep(streams, qi + 1, lambda pos: jnp.where(pos == 0, qi, pos - 1))

    d = acc1_ref[...] / l1 - lam_ref[0] * (acc2_ref[...] / l2)
    r = lax.rsqrt(jnp.mean(d * d, axis=0, keepdims=True) + RMS_EPS)
    o_ref[...] = (d * r).T * (subln_ref[...] * out_scale)


def _diff_attention(qkv, lam_f, subln, batch, seq, col_q, col_k, col_v, n_heads, out_scale, tq=256):
    dv = 2 * HEAD_DIM
    kern = functools.partial(_diff_kernel, tq=tq, n_heads=n_heads, out_scale=out_scale)
    grid_spec = pltpu.PrefetchScalarGridSpec(
        num_scalar_prefetch=1,
        grid=(batch, n_heads, seq // tq),
        in_specs=[
            pl.BlockSpec((None, tq, dv), lambda b, h, i, s: (b, i, col_q + h)),
            pl.BlockSpec((None, seq, dv), lambda b, h, i, s: (b, 0, col_k + h)),
            pl.BlockSpec((None, seq, dv), lambda b, h, i, s: (b, 0, col_v + h)),
            pl.BlockSpec((1, dv), lambda b, h, i, s: (0, 0)),
        ],
        out_specs=pl.BlockSpec((None, tq, dv), lambda b, h, i, s: (b, i, h)),
        scratch_shapes=[
            pltpu.VMEM((seq, LANES), BF16), pltpu.VMEM((seq, LANES), BF16),
            pltpu.VMEM((dv, seq), BF16),
            pltpu.VMEM((dv, tq), F32), pltpu.VMEM((dv, tq), F32),
            pltpu.VMEM((2, FLASH_UNROLL, tq, tq), F32), pltpu.VMEM((2, FLASH_STAGED, tq, tq), BF16),
        ],
    )
    return pl.pallas_call(
        kern,
        grid_spec=grid_spec,
        out_shape=jax.ShapeDtypeStruct((batch, seq, n_heads * dv), F32),
        compiler_params=pltpu.CompilerParams(
            dimension_semantics=("parallel", "parallel", "arbitrary"),
            vmem_limit_bytes=VMEM_LIMIT),
        name="diff_attn",
    )(lam_f.reshape(1).astype(F32), qkv, qkv, qkv, subln.reshape(1, dv).astype(F32))


BAND = 128
DILATIONS = (1, 4, 16)


def _dilated_kernel(q_ref, kp_ref, kc_ref, vp_ref, vc_ref, o_ref, kbuf_ref, vbuf_ref, st_ref,
                    *, chunk, n_heads):
    pair = pl.program_id(1)
    ci = pl.program_id(2)
    kbuf_ref[:chunk, :] = kp_ref[...]
    kbuf_ref[chunk:, :] = kc_ref[...]
    vbuf_ref[:chunk, :] = vp_ref[...]
    vbuf_ref[chunk:, :] = vc_ref[...]

    qi = lax.broadcasted_iota(jnp.int32, (BAND, 2 * BAND), 0)
    kj = lax.broadcasted_iota(jnp.int32, (BAND, 2 * BAND), 1)
    dist = qi + BAND - kj
    in_band = jnp.abs(2 * dist - BAND) <= BAND
    dist_f = dist.astype(F32)
    key_col = lax.broadcasted_iota(jnp.int32, (1, 2 * BAND), 1)
    slopes = [jnp.exp2(-8.0 * (2 * pair + e + 1).astype(F32) / n_heads) for e in range(2)]
    lane_q = lax.broadcasted_iota(jnp.int32, (BAND, LANES), 1)
    own_half = [lane_q < HEAD_DIM, lane_q >= HEAD_DIM]

    for pi, dil in enumerate(DILATIONS):
        span = dil * BAND
        n_blk = chunk // span

        def block(idx, carry, dil=dil, span=span, n_blk=n_blk, pi=pi):
            r = idx // n_blk
            b = idx % n_blk
            q0 = r + b * span
            qb = q_ref[pl.ds(q0, BAND, stride=dil), :]
            kb = kbuf_ref[pl.ds(chunk + q0 - span, 2 * BAND, stride=dil), :]
            vb = vbuf_ref[pl.ds(chunk + q0 - span, 2 * BAND, stride=dil), :]
            first = jnp.logical_and(ci == 0, b == 0)
            before_start = jnp.where(key_col < jnp.where(first, BAND, 0), NEG, 0.0)
            kb16 = kb.astype(BF16)
            vb16 = vb.astype(BF16)
            packed = []
            for e in range(2):
                qe = jnp.where(own_half[e], qb, 0.0).astype(BF16)
                s = _dot_nt(qe, kb16)
                s = jnp.where(in_band, s - (slopes[e] * dil) * dist_f, NEG) + before_start
                m = jnp.max(s, axis=1, keepdims=True)
                p = jnp.exp(s - m)
                l = jnp.sum(p, axis=1, keepdims=True)
                o = _dot(p.astype(BF16), vb16) / l
                packed.append(jnp.where(own_half[e], o, m + jnp.log(l)))
            for e in range(2):
                st_ref[pi, e, pl.ds(q0, BAND, stride=dil), :] = packed[e]
            return carry

        lax.fori_loop(0, dil * n_blk, block, 0, unroll=16)

    rows_per = 256
    lane = lax.broadcasted_iota(jnp.int32, (rows_per, LANES), 1)

    def merge(c, carry):
        rows = pl.ds(pl.multiple_of(c * rows_per, rows_per), rows_per)
        data, lses = [], []
        for pi in range(len(DILATIONS)):
            x0 = st_ref[pi, 0, rows, :]
            x1 = st_ref[pi, 1, rows, :]
            data.append(jnp.where(lane < HEAD_DIM, x0, x1))
            lses.append(pltpu.roll(jnp.where(lane < HEAD_DIM, x1, x0), HEAD_DIM, axis=1))
        top = functools.reduce(jnp.maximum, lses)
        num = jnp.zeros((rows_per, LANES), F32)
        den = jnp.zeros((rows_per, LANES), F32)
        for x, lse in zip(data, lses):
            w = jnp.exp(lse - top)
            num = num + w * x
            den = den + w
        o_ref[rows, :] = num / den
        return carry

    lax.fori_loop(0, chunk // rows_per, merge, 0)


def _dilated_attention(act, batch, seq, col_q, col_k, col_v, n_heads, chunk=2048):
    assert chunk % (max(DILATIONS) * BAND) == 0 and seq % chunk == 0
    prev = lambda c: jnp.maximum(c - 1, 0)
    return pl.pallas_call(
        functools.partial(_dilated_kernel, chunk=chunk, n_heads=n_heads),
        grid=(batch, n_heads // 2, seq // chunk),
        in_specs=[
            pl.BlockSpec((None, chunk, LANES), lambda b, p, c: (b, c, col_q + p)),
            pl.BlockSpec((None, chunk, LANES), lambda b, p, c: (b, prev(c), col_k + p)),
            pl.BlockSpec((None, chunk, LANES), lambda b, p, c: (b, c, col_k + p)),
            pl.BlockSpec((None, chunk, LANES), lambda b, p, c: (b, prev(c), col_v + p)),
            pl.BlockSpec((None, chunk, LANES), lambda b, p, c: (b, c, col_v + p)),
        ],
        out_specs=pl.BlockSpec((None, chunk, LANES), lambda b, p, c: (b, c, p)),
        out_shape=jax.ShapeDtypeStruct((batch, seq, n_heads * HEAD_DIM), F32),
        scratch_shapes=[
            pltpu.VMEM((2 * chunk, LANES), F32),
            pltpu.VMEM((2 * chunk, LANES), F32),
            pltpu.VMEM((len(DILATIONS), 2, chunk, LANES), F32),
        ],
        compiler_params=pltpu.CompilerParams(
            dimension_semantics=("parallel", "parallel", "arbitrary"),
            vmem_limit_bytes=VMEM_LIMIT),
        name="dilated_attn",
    )(act, act, act, act, act)


TAKEN = -3e38


def _topk_member(cur, k):
    n = cur.shape[0]
    row = lax.broadcasted_iota(jnp.int32, cur.shape, 0)
    sel = jnp.zeros(cur.shape, F32)
    for _ in range(k):
        mx = jnp.max(cur, axis=0, keepdims=True)
        first = jnp.min(jnp.where(cur == mx, row, n), axis=0, keepdims=True)
        pick = row == first
        sel = jnp.where(pick, 1.0, sel)
        cur = jnp.where(pick, TAKEN, cur)
    return sel > 0.5


def _moba_kernel(q_ref, k_ref, v_ref, o_ref, kaug_ref, vt_ref, kmean_ref, selb_ref, acc_ref,
                 t_ref, p_ref, *, blk, topk, n_heads):
    pair = pl.program_id(1)
    qi = pl.program_id(2)
    S = k_ref.shape[0]
    nblk = S // blk
    tq = tk = blk

    @pl.when(qi == 0)
    def _():
        jj = lax.broadcasted_iota(jnp.int32, (tk, 1), 0).astype(F32)
        kaug = _aug_const(tk, [jj]).astype(BF16)

        def body(c, carry):
            rows = pl.ds(pl.multiple_of(c * tk, tk), tk)
            kc = k_ref[rows, :]
            vc = v_ref[rows, :].astype(F32).T.astype(BF16)
            for e in range(2):
                kaug_ref[e, rows, :] = jnp.concatenate(
                    [kc[:, e * HEAD_DIM:(e + 1) * HEAD_DIM], kaug], axis=1)
                vt_ref[e, :, rows] = vc[e * HEAD_DIM:(e + 1) * HEAD_DIM, :]
            kmean_ref[pl.ds(c, 1), :] = jnp.mean(kc.astype(F32), axis=0, keepdims=True)
            return carry

        lax.fori_loop(0, nblk, body, 0)

    qt = q_ref[...]
    blk_row = lax.broadcasted_iota(jnp.int32, (nblk, tq), 0)
    past = blk_row < qi
    qs, slopes = [], []
    for e in range(2):
        slope = jnp.exp2(-8.0 * (2 * pair + e + 1).astype(F32) / n_heads)
        qe = qt[:, e * HEAD_DIM:(e + 1) * HEAD_DIM]
        km_hi, km_lo = _split_bf16(kmean_ref[:, e * HEAD_DIM:(e + 1) * HEAD_DIM])
        gate = _dot_nt(km_hi, qe) + _dot_nt(km_lo, qe)
        sel = _topk_member(jnp.where(past, gate, -2e38), topk)
        selb_ref[e] = jnp.where(blk_row == qi, 0.0, jnp.where(past, jnp.where(sel, 0.0, NEG), NEG))
        qs.append(jnp.concatenate([qe, _aug_const(tq, [slope]).astype(BF16)], axis=1))
        slopes.append(slope)

    rows = lambda n: pl.ds(pl.multiple_of(n * tk, tk), tk)
    qpos = qi * tq + lax.broadcasted_iota(jnp.int32, (1, tq), 1)
    causal = lambda n: n * tk + lax.broadcasted_iota(jnp.int32, (tk, 1), 0) <= qpos
    streams = [
        _Stream(qs[e],
                functools.partial(lambda n, e: kaug_ref[e, rows(n), :], e=e),
                functools.partial(lambda n, e: vt_ref[e, :, rows(n)], e=e),
                acc_ref.at[e], t_ref.at[e], p_ref.at[e],
                functools.partial(lambda n, e: slopes[e] * ((n - qi) * tk).astype(F32), e=e),
                bias=functools.partial(lambda n, e: selb_ref[e, pl.ds(n, 1), :], e=e),
                mask=causal)
        for e in range(2)]
    stats = _flash_sweep(streams, qi + 1, lambda pos: jnp.where(pos == 0, qi, pos - 1))
    o_ref[...] = jnp.concatenate([acc_ref[e] / stats[e][1] for e in range(2)], axis=0).T


def _moba_attention(qkv, batch, seq, col_q, col_k, col_v, n_heads, blk, topk):
    kern = functools.partial(_moba_kernel, blk=blk, topk=topk, n_heads=n_heads)
    nblk = seq // blk
    return pl.pallas_call(
        kern,
        grid=(batch, n_heads // 2, nblk),
        in_specs=[
            pl.BlockSpec((None, blk, LANES), lambda b, p, i: (b, i, col_q + p)),
            pl.BlockSpec((None, seq, LANES), lambda b, p, i: (b, 0, col_k + p)),
            pl.BlockSpec((None, seq, LANES), lambda b, p, i: (b, 0, col_v + p)),
        ],
        out_specs=pl.BlockSpec((None, blk, LANES), lambda b, p, i: (b, i, p)),
        out_shape=jax.ShapeDtypeStruct((batch, seq, n_heads * HEAD_DIM), F32),
        scratch_shapes=[
            pltpu.VMEM((2, seq, LANES), BF16),
            pltpu.VMEM((2, HEAD_DIM, seq), BF16),
            pltpu.VMEM((nblk, LANES), F32),
            pltpu.VMEM((2, nblk, blk), F32),
            pltpu.VMEM((2, HEAD_DIM, blk), F32),
            pltpu.VMEM((2, FLASH_UNROLL, blk, blk), F32),
            pltpu.VMEM((2, FLASH_STAGED, blk, blk), BF16),
        ],
        compiler_params=pltpu.CompilerParams(
            dimension_semantics=("parallel", "parallel", "arbitrary"),
            vmem_limit_bytes=VMEM_LIMIT),
        name="moba_attn",
    )(qkv, qkv, qkv)


CMP_STRIDE = 16
CMP_LEN = 32
SLC_BLK = 64
SLC_TOPK = 16
NSA_WINDOW = 512
NSA_TQ = 256
NSA_TK = 256


def _compress_kernel(c_ref, pe_ref, w1_ref, w2_ref, g_ref, o_ref, *, is_key):
    half = c_ref.shape[1]
    n = c_ref.shape[0]
    c = c_ref[...]
    w1 = w1_ref[...]
    a_lo = _dot(c, w1[:half])
    a_hi = _dot(c, w1[half:])
    pe_hi, pe_lo = _split_bf16(pe_ref[...])
    c0 = (_dot(pe_hi, w1) + _dot(pe_lo, w1))[0:1]
    y = a_lo + pltpu.roll(a_hi, n - 1, axis=0) + c0
    hid = (y * jax.nn.sigmoid(y)).astype(BF16)
    if is_key:
        z = _dot(hid, w2_ref[...])
    else:
        z = _dot(hid.astype(F32), w2_ref[...].astype(F32))
    if is_key:
        ms = jnp.sum(z * z, axis=-1, keepdims=True) * (1.0 / HEAD_DIM)
        kn = z * lax.rsqrt(ms + RMS_EPS) * g_ref[...]
        i = lax.broadcasted_iota(jnp.int32, (n, 1), 0)
        aug = _aug_const(n, [(i // 16 * 256).astype(F32), (i % 16 * 16).astype(F32), 1.0])
        o_ref[...] = jnp.concatenate([kn[:, :HEAD_DIM], aug], axis=1).astype(BF16)
    else:
        o_ref[...] = z.T[:HEAD_DIM, :].astype(BF16)


def _compress(chunks, pe, w1, w2, gain, is_key):
    B, n, half = chunks.shape
    Hkv = half // (CMP_STRIDE * HEAD_DIM)
    hidden = w1.shape[1]

    def per_head(t, tail):
        t = t.reshape(2, CMP_STRIDE, 1, HEAD_DIM, *tail)
        onehot = jnp.eye(Hkv, dtype=t.dtype).reshape(Hkv, 1, 1, Hkv, 1, *([1] * len(tail)))
        return (onehot * t[None]).reshape(Hkv, 2 * half, *tail)

    pe8 = jnp.broadcast_to(per_head(pe.reshape(-1).astype(F32), ())[:, None, :], (Hkv, 8, 2 * half))
    w1 = per_head(w1, (hidden,))
    w2p = jnp.pad(w2, ((0, 0), (0, LANES - w2.shape[1]))).astype(BF16)
    g = jnp.pad(gain.astype(F32), (0, LANES - HEAD_DIM)).reshape(1, LANES)
    if is_key:
        out_shape = jax.ShapeDtypeStruct((B, Hkv, n, LANES), BF16)
        out_spec = pl.BlockSpec((None, None, n, LANES), lambda b, h: (b, h, 0, 0))
    else:
        out_shape = jax.ShapeDtypeStruct((B, Hkv, HEAD_DIM, n), BF16)
        out_spec = pl.BlockSpec((None, None, HEAD_DIM, n), lambda b, h: (b, h, 0, 0))
    return pl.pallas_call(
        functools.partial(_compress_kernel, is_key=is_key),
        grid=(B, Hkv),
        in_specs=[
            pl.BlockSpec((None, n, half), lambda b, h: (b, 0, 0)),
            pl.BlockSpec((None, 8, 2 * half), lambda b, h: (h, 0, 0)),
            pl.BlockSpec((None, 2 * half, hidden), lambda b, h: (h, 0, 0)),
            pl.BlockSpec((hidden, LANES), lambda b, h: (0, 0)),
            pl.BlockSpec((1, LANES), lambda b, h: (0, 0)),
        ],
        out_specs=out_spec,
        out_shape=out_shape,
        compiler_params=pltpu.CompilerParams(
            dimension_semantics=("parallel", "parallel"), vmem_limit_bytes=VMEM_LIMIT),
        name="nsa_compress_k" if is_key else "nsa_compress_v",
    )(chunks, pe8, w1.astype(BF16), w2p, g)


CMP_ROWS = 128


def _nsa_cmp_kernel(q_ref, kc_ref, vct_ref, ct_ref, o_ref, selb_ref, used_ref, imp_ref,
                    *, tq, group, n_heads):
    g = pl.program_id(1)
    qi = pl.program_id(2)
    n_cmp = kc_ref.shape[0]
    n_slc = ct_ref.shape[0]
    i0 = qi * tq
    qpos = i0 + lax.broadcasted_iota(jnp.int32, (1, tq), 1)
    qt = q_ref[...]

    def attend(rows):
        cend = lax.broadcasted_iota(jnp.int32, (rows, 1), 0) * CMP_STRIDE + (CMP_LEN - 1)
        valid = cend <= qpos
        kc = kc_ref[:rows, :]
        vct = vct_ref[:, :rows]
        ts = []
        for e in range(group):
            slope = jnp.exp2(-8.0 * (group * g + e + 1).astype(F32) / n_heads)
            aug = _aug_const(tq, [slope, slope, -slope * i0.astype(F32)]).astype(BF16)
            qa = jnp.concatenate([qt[:, e * HEAD_DIM:(e + 1) * HEAD_DIM], aug], axis=1)
            ts.append(_dot_nt(kc, qa))
        psum = jnp.zeros((rows, tq), F32)
        ps = []
        for t in ts:
            t = jnp.where(valid, t, NEG)
            m = jnp.max(t, axis=0, keepdims=True)
            p = jnp.where(valid, jnp.exp(t - m), 0.0)
            l = jnp.sum(p, axis=0, keepdims=True)
            p = p / jnp.where(l > 0, l, 1.0)
            ps.append(p.astype(BF16))
            psum = psum + p
        o_ref[...] = jnp.concatenate([_dot(vct, p) for p in ps], axis=0).T
        p_hi, p_lo = _split_bf16(psum)
        ct = ct_ref[:, :rows]
        imp_ref[...] = _dot(ct, p_hi) + _dot(ct, p_lo)

    needed = (i0 + tq - CMP_LEN) // CMP_STRIDE + 1
    n_var = max(n_cmp // CMP_ROWS, 1)
    for v in range(n_var):
        rows = n_cmp if v == n_var - 1 else (v + 1) * CMP_ROWS
        cond = needed > v * CMP_ROWS
        if v < n_var - 1:
            cond = jnp.logical_and(cond, needed <= (v + 1) * CMP_ROWS)
        pl.when(cond)(functools.partial(attend, rows))

    imp = imp_ref[...]
    jb = lax.broadcasted_iota(jnp.int32, (n_slc, tq), 0)
    qb = qpos // SLC_BLK
    cur = jnp.where(jb > qb, -2e38, imp)
    for forced in (0, qb, qb - 1):
        cur = jnp.where(jb == forced, 3e38, cur)
    sel = _topk_member(cur, min(SLC_TOPK, n_slc))
    selb = jnp.where(jb <= qb, jnp.where(sel, 0.0, NEG), NEG).T
    selb_ref[...] = selb.astype(BF16)
    for c in range(tq // NSA_TQ):
        hit = jnp.max(selb[c * NSA_TQ:(c + 1) * NSA_TQ], axis=0, keepdims=True) == 0.0
        used_ref[c] = jnp.where(hit, 1, 0)


def _nsa_cmp(act, kc, vct, batch, seq, col_q, n_heads, group, tq=256):
    Hkv = n_heads // group
    n_cmp = kc.shape[2]
    n_slc = seq // SLC_BLK
    cs = jnp.arange(n_cmp) * CMP_STRIDE
    ss = jnp.arange(n_slc) * SLC_BLK
    overlap = (cs[None, :] <= ss[:, None] + SLC_BLK - 1) & (cs[None, :] + CMP_LEN - 1 >= ss[:, None])
    overlap = overlap & (cs[None, :] + CMP_LEN <= seq)
    ct = overlap.astype(BF16)
    gw = group * HEAD_DIM
    return pl.pallas_call(
        functools.partial(_nsa_cmp_kernel, tq=tq, group=group, n_heads=n_heads),
        grid=(batch, Hkv, seq // tq),
        in_specs=[
            pl.BlockSpec((None, tq, gw), lambda b, g, i: (b, i, col_q + g)),
            pl.BlockSpec((None, None, n_cmp, LANES), lambda b, g, i: (b, g, 0, 0)),
            pl.BlockSpec((None, None, HEAD_DIM, n_cmp), lambda b, g, i: (b, g, 0, 0)),
            pl.BlockSpec((n_slc, n_cmp), lambda b, g, i: (0, 0)),
        ],
        out_specs=[
            pl.BlockSpec((None, tq, gw), lambda b, g, i: (b, i, g)),
            pl.BlockSpec((None, None, tq, n_slc), lambda b, g, i: (b, g, i, 0)),
            pl.BlockSpec((None, None, tq // NSA_TQ, 1, n_slc), lambda b, g, i: (b, g, i, 0, 0)),
        ],
        out_shape=[
            jax.ShapeDtypeStruct((batch, seq, n_heads * HEAD_DIM), F32),
            jax.ShapeDtypeStruct((batch, Hkv, seq, n_slc), BF16),
            jax.ShapeDtypeStruct((batch, Hkv, seq // NSA_TQ, 1, n_slc), jnp.int32),
        ],
        scratch_shapes=[pltpu.VMEM((n_slc, tq), F32)],
        compiler_params=pltpu.CompilerParams(
            dimension_semantics=("parallel", "parallel", "parallel"),
            vmem_limit_bytes=VMEM_LIMIT),
        name="nsa_cmp",
    )(act, kc, vct, ct)


def _nsa_main_kernel(used_ref, q_ref, selb_ref, kvs_ref, kvw_ref, kconst_ref, ocmp_ref, gate_ref,
                     o_ref, ks_ref, vst_ref, kw_ref, vwt_ref, accs_ref, accw_ref, gt_ref, t_ref,
                     p_ref, tiles_ref, *, tq, tk, group, n_heads):
    g = pl.program_id(1)
    qi = pl.program_id(2)
    S = kvs_ref.shape[0]
    R = group * tq
    i0 = qi * tq

    @pl.when(qi == 0)
    def _():
        def body(c, carry):
            rows = pl.ds(pl.multiple_of(c * tk, tk), tk)
            kvs = kvs_ref[rows, :]
            kvw = kvw_ref[rows, :]
            kconst = kconst_ref[...]
            ks_ref[rows, :HEAD_DIM] = kvs[:, :HEAD_DIM]
            ks_ref[rows, HEAD_DIM:LANES] = kconst[:, :HEAD_DIM]
            kw_ref[rows, :] = jnp.concatenate([kvw[:, :HEAD_DIM], kconst[:, :HEAD_DIM]], axis=1)
            vst_ref[:, rows] = kvs.astype(F32).T[HEAD_DIM:, :].astype(BF16)
            vwt_ref[:, rows] = kvw.astype(F32).T[HEAD_DIM:, :].astype(BF16)
            key_blk = (c * tk + lax.broadcasted_iota(jnp.int32, (tk, n_slc), 0)) // SLC_BLK
            onehot = key_blk == lax.broadcasted_iota(jnp.int32, (tk, n_slc), 1)
            ks_ref[rows, LANES:] = jnp.where(onehot, 1.0, 0.0).astype(BF16)
            return carry

        n_slc = ks_ref.shape[1] - LANES
        lax.fori_loop(0, S // tk, body, 0)

    qt = q_ref[...]
    selb = selb_ref[...]
    qs_parts, qw_parts, slope_parts = [], [], []
    for e in range(group):
        slope = jnp.exp2(-8.0 * (group * g + e + 1).astype(F32) / n_heads)
        aug = _aug_const(tq, [slope]).astype(BF16)
        qe = qt[:, e * HEAD_DIM:(e + 1) * HEAD_DIM]
        qw_parts.append(jnp.concatenate([qe, aug], axis=1))
        qs_parts.append(jnp.concatenate([qe, aug, selb], axis=1))
        slope_parts.append(jnp.full((1, tq), slope, F32))
    q_slc = jnp.concatenate(qs_parts, axis=0)
    q_win = jnp.concatenate(qw_parts, axis=0)
    slope_row = jnp.concatenate(slope_parts, axis=1)
    qpos = i0 + lax.broadcasted_iota(jnp.int32, (1, R), 1) % tq

    nd = i0 // tk
    rows = lambda n: pl.ds(pl.multiple_of(n * tk, tk), tk)
    shift = lambda n: slope_row * (n * tk - i0).astype(F32)
    kpos = lambda n: n * tk + lax.broadcasted_iota(jnp.int32, (tk, 1), 0)
    causal = lambda n: kpos(n) <= qpos

    n_tiles = S // tk
    base = ((pl.program_id(0) * pl.num_programs(1) + g) * pl.num_programs(2) + qi) * n_tiles

    def collect(n, cnt):
        tiles_ref[cnt] = n
        return cnt + jnp.where(used_ref[base + n] != 0, 1, 0)

    tiles_ref[0] = nd
    n_used = lax.fori_loop(0, nd, collect, 1)
    assert 2 * tk == NSA_WINDOW and tk % tq == 0
    tw, offs = [], []
    for c in range(3):
        n = nd - 2 + c
        t = _dot_nt(kw_ref[rows(jnp.maximum(n, 0)), :], q_win)
        if c == 0:
            t = jnp.where(qpos - kpos(n) < NSA_WINDOW, t, NEG)
        elif c == 2:
            t = jnp.where(kpos(n) <= qpos, t, NEG)
        tw.append(t)
        offs.append(shift(n) + jnp.where(n >= 0, 0.0, NEG))
    mw = functools.reduce(jnp.maximum,
                          [jnp.max(t, axis=0, keepdims=True) + off for t, off in zip(tw, offs)])
    lw = jnp.zeros((1, R), F32)
    o_w = jnp.zeros((HEAD_DIM, R), F32)
    for c in range(3):
        p = jnp.exp(tw[c] - (mw - offs[c]))
        lw = lw + jnp.sum(p, axis=0, keepdims=True)
        o_w = o_w + _dot(vwt_ref[:, rows(jnp.maximum(nd - 2 + c, 0))], p.astype(BF16))
    accw_ref[...] = o_w

    slc = _Stream(q_slc, lambda n: ks_ref[rows(n), :], lambda n: vst_ref[:, rows(n)],
                  accs_ref, t_ref, p_ref, shift, mask=causal)
    ((_, ls),) = _flash_sweep([slc], n_used, lambda pos: tiles_ref[pos])

    gz = gate_ref[...]
    gt_ref[...] = jax.nn.sigmoid(gz).T
    oc_t = ocmp_ref[...].T
    outs = []
    for e in range(group):
        head = group * g + e
        cols = slice(e * tq, (e + 1) * tq)
        o_s = accs_ref[:, cols] / ls[:, cols]
        o_w = accw_ref[:, cols] / lw[:, cols]
        o_c = oc_t[e * HEAD_DIM:(e + 1) * HEAD_DIM, :]
        g_c = gt_ref[pl.ds(3 * head, 1), :]
        g_s = gt_ref[pl.ds(3 * head + 1, 1), :]
        g_w = gt_ref[pl.ds(3 * head + 2, 1), :]
        outs.append(g_c * o_c + g_s * o_s + g_w * o_w)
    o_ref[...] = jnp.concatenate(outs, axis=0).T


def _nsa_main(act, selb, used, ocmp, gates, batch, seq, col_q, col_kvs, col_kvw, col_gate, n_heads,
              group):
    tq, tk = NSA_TQ, NSA_TK
    Hkv = n_heads // group
    n_slc = seq // SLC_BLK
    gw = group * HEAD_DIM
    R = group * tq
    jj = jnp.arange(tk, dtype=F32)
    kconst = jnp.zeros((tk, HEAD_DIM), F32).at[:, 0].set(jj).astype(BF16)
    blk_per_tile = tk // SLC_BLK
    used_tiles = jnp.max(used.reshape(batch, Hkv, seq // tq, n_slc // blk_per_tile, blk_per_tile),
                         axis=-1).reshape(-1).astype(jnp.int32)
    grid_spec = pltpu.PrefetchScalarGridSpec(
        num_scalar_prefetch=1,
        grid=(batch, Hkv, seq // tq),
        in_specs=[
            pl.BlockSpec((None, tq, gw), lambda b, g, i, u: (b, i, col_q + g)),
            pl.BlockSpec((None, None, tq, n_slc), lambda b, g, i, u: (b, g, i, 0)),
            pl.BlockSpec((None, seq, LANES), lambda b, g, i, u: (b, 0, col_kvs + g)),
            pl.BlockSpec((None, seq, LANES), lambda b, g, i, u: (b, 0, col_kvw + g)),
            pl.BlockSpec((tk, HEAD_DIM), lambda b, g, i, u: (0, 0)),
            pl.BlockSpec((None, tq, gw), lambda b, g, i, u: (b, i, g)),
            pl.BlockSpec((None, tq, LANES), lambda b, g, i, u: (b, i, col_gate)),
        ],
        out_specs=pl.BlockSpec((None, tq, gw), lambda b, g, i, u: (b, i, g)),
        scratch_shapes=[
            pltpu.VMEM((seq, LANES + n_slc), BF16),
            pltpu.VMEM((HEAD_DIM, seq), BF16),
            pltpu.VMEM((seq, LANES), BF16),
            pltpu.VMEM((HEAD_DIM, seq), BF16),
            pltpu.VMEM((HEAD_DIM, R), F32),
            pltpu.VMEM((HEAD_DIM, R), F32),
            pltpu.VMEM((LANES, tq), F32),
            pltpu.VMEM((FLASH_UNROLL, tk, R), F32), pltpu.VMEM((FLASH_STAGED, tk, R), BF16),
            pltpu.SMEM((seq // tk,), jnp.int32),
        ],
    )
    return pl.pallas_call(
        functools.partial(_nsa_main_kernel, tq=tq, tk=tk, group=group, n_heads=n_heads),
        grid_spec=grid_spec,
        out_shape=jax.ShapeDtypeStruct((batch, seq, n_heads * HEAD_DIM), F32),
        compiler_params=pltpu.CompilerParams(
            dimension_semantics=("parallel", "parallel", "arbitrary"),
            vmem_limit_bytes=VMEM_LIMIT),
        name="nsa_main",
    )(used_tiles, act, selb, act, act, kconst, ocmp, gates)


def _tile_gain(g, width):
    return jnp.tile(g.astype(F32), width // HEAD_DIM)


def _even_layer(x, ln, w_in, qkn, lam, subln, w_out, layer):
    B, S, D = x.shape
    M = B * S
    x2d = x.reshape(M, D)
    W = 512
    aq, ak, av, az, bq, bk, bv, bz = [w_in[:, i * W:(i + 1) * W] for i in range(8)]
    scale = HEAD_DIM ** -0.5
    ones, zeros = jnp.ones((W,), F32), jnp.zeros((W,), F32)
    w_b = jnp.concatenate([bq, bk, bv], axis=1).astype(BF16)
    gain_b = jnp.concatenate([_tile_gain(qkn[2], W) * scale, _tile_gain(qkn[3], W), ones])
    mask_b = np.repeat([1.0, 1.0, 0.0], W)
    act_b = _project(x2d, ln, w_b, gain_b, mask_b, BF16)
    w_a = jnp.concatenate([aq, ak, av, az, bz], axis=1).astype(BF16)
    gain_a = jnp.concatenate([_tile_gain(qkn[0], W) * scale, _tile_gain(qkn[1], W), ones, ones, ones])
    mask_a = np.repeat([1.0, 1.0, 0.0, 0.0, 0.0], W)
    act_a = _project(x2d, ln, w_a, gain_a, mask_a, F32)

    lam_init = 0.8 - 0.6 * math.exp(-0.3 * layer)
    lf = lam.astype(F32)
    lam_f = jnp.exp(jnp.sum(lf[0] * lf[1])) - jnp.exp(jnp.sum(lf[2] * lf[3])) + lam_init
    ob = _diff_attention(act_b.reshape(B, S, 3 * W), lam_f, subln, B, S, 0, 4, 8, 4, 1.0 - lam_init)
    oa = _dilated_attention(act_a.reshape(B, S, 5 * W), B, S, 0, 4, 8, 8)
    out = _out_project(x2d, oa.reshape(M, W), ob.reshape(M, W), act_a, 3, 4, w_out)
    return out.reshape(B, S, D)


def _odd_layer(x, ln, w_in, qkn, phi_pe, phi_w1, phi_w2, w_out):
    B, S, D = x.shape
    M = B * S
    x2d = x.reshape(M, D)
    W, KW = 512, 128
    offs = [0]
    for s in (W, W, W, W, W, KW, KW, KW, KW, KW, KW, W, 24):
        offs.append(offs[-1] + s)
    cq, ck, cv, cz, dq, dkc, dvc, dks, dvs, dkw, dvw, dz, dg = [
        w_in[:, offs[i]:offs[i + 1]] for i in range(13)]
    scale = HEAD_DIM ** -0.5
    hd = HEAD_DIM
    ones = lambda n: jnp.ones((n,), F32)
    zeros = lambda n: jnp.zeros((n,), F32)
    kv_pair = lambda k, v: jnp.concatenate([k[:, :hd], v[:, :hd], k[:, hd:], v[:, hd:]], axis=1)
    w_b = jnp.concatenate([cq, ck, cv, dq, kv_pair(dks, dvs), kv_pair(dkw, dvw), dkc, dvc,
                           jnp.zeros((D, 2 * KW), w_in.dtype)], axis=1).astype(BF16)
    kv_gain = lambda g: jnp.concatenate([g, ones(hd), g, ones(hd)])
    kv_mask = np.repeat([1.0, 0.0, 1.0, 0.0], hd)
    gain_b = jnp.concatenate([_tile_gain(qkn[0], W) * scale, _tile_gain(qkn[1], W), ones(W),
                              _tile_gain(qkn[2], W) * scale, kv_gain(qkn[4].astype(F32)),
                              kv_gain(qkn[5].astype(F32)), ones(4 * KW)])
    mask_b = np.concatenate([np.repeat([1.0, 1.0, 0.0, 1.0], W), kv_mask, kv_mask, np.zeros(4 * KW)])
    act_b = _project(x2d, ln, w_b, gain_b, mask_b, BF16).reshape(B, S, 6 * W)
    w_g = jnp.concatenate([cz, dz, dg, jnp.zeros((D, W - 24), w_in.dtype)], axis=1).astype(BF16)
    act_g = _project(x2d, ln, w_g, ones(3 * W), np.zeros(3 * W), F32)

    oc = _moba_attention(act_b, B, S, 0, 4, 8, 8, 256, 3)

    def chunks(t):
        return t.reshape(B, S // CMP_STRIDE, CMP_STRIDE * 2 * hd)

    kc = _compress(chunks(act_b[:, :, 20 * KW:21 * KW]), phi_pe[0], phi_w1[0], phi_w2[0], qkn[3], True)
    vct = _compress(chunks(act_b[:, :, 21 * KW:22 * KW]), phi_pe[1], phi_w1[1], phi_w2[1], qkn[3], False)
    ocmp, selb, used = _nsa_cmp(act_b, kc, vct, B, S, 6, 8, 4)
    od = _nsa_main(act_b, selb, used, ocmp, act_g.reshape(B, S, 3 * W), B, S, 6, 16, 18, 8, 8, 4)
    out = _out_project(x2d, oc.reshape(M, W), od.reshape(M, W), act_g, 0, 1, w_out)
    return out.reshape(B, S, D)


def kernel(x, ln_e, w_in_e, qkn_e, lam_e, subln_e, w_out_e, ln_o, w_in_o, qkn_o, phi_pe, phi_w1, phi_w2, w_out_o):
    n_layers = ln_e.shape[0] + ln_o.shape[0]
    for layer in range(n_layers):
        i = layer // 2
        if layer % 2 == 0:
            x = _even_layer(x, ln_e[i], w_in_e[i], qkn_e[i], lam_e[i], subln_e[i], w_out_e[i], layer)
        else:
            x = _odd_layer(x, ln_o[i], w_in_o[i], qkn_o[i], phi_pe[i], phi_w1[i], phi_w2[i], w_out_o[i])
    return x
```
